```python
import math
import jax, jax.numpy as jnp
from jax import lax
import numpy as np


D_MODEL = 1024
BATCH = 8
SEQ = 2048
DEPTH = 1

HEAD_DIM = 64
SB_HEADS = 8
SB_WIDTH = SB_HEADS * HEAD_DIM
MOBA_HEADS = 8
MOBA_WIDTH = MOBA_HEADS * HEAD_DIM
N_BRANCHES = 2
QKV_WIDTH = 3 * SB_WIDTH + 3 * MOBA_WIDTH
IN_WIDTH = QKV_WIDTH + N_BRANCHES * D_MODEL
SB_Q_BLOCK = 128
MOBA_BLOCK = 256
MOBA_TOPK = 3
MOBA_Q_CHUNK = 16
ROPE_THETA = 500000.0
ROPE_DIM = HEAD_DIM // 4
N_EXPERTS = 32
TOP_K = 4
D_FF = D_MODEL
SWIGLU_LIMIT = 7.0
SWIGLU_ALPHA = 1.702
EXPERT_ROWS = 256
LN_EPS = 1e-5
DEEPNORM_ALPHA = (2 * DEPTH) ** 0.25
DEEPNORM_BETA = (8 * DEPTH) ** -0.25

kernel_name = "hybrid_stickbreak_moba_gptoss_moe_deepnorm"


def layer_norm(x, g, b):
    xf = x.astype(jnp.float32)
    mu = jnp.mean(xf, axis=-1, keepdims=True)
    var = jnp.mean(jnp.square(xf - mu), axis=-1, keepdims=True)
    y = (xf - mu) * lax.rsqrt(var + LN_EPS) * g.astype(jnp.float32) + b.astype(jnp.float32)
    return y.astype(x.dtype)


def partial_rope(x, positions):
    half = ROPE_DIM // 2
    inv_freq = ROPE_THETA ** (-jnp.arange(half, dtype=jnp.float32) / half)
    ang = positions.astype(jnp.float32)[:, None, :, None] * inv_freq
    cos, sin = jnp.cos(ang), jnp.sin(ang)
    xr = x[..., :ROPE_DIM].astype(jnp.float32)
    x1, x2 = xr[..., :half], xr[..., half:]
    rot = jnp.concatenate([x1 * cos - x2 * sin, x2 * cos + x1 * sin], axis=-1).astype(x.dtype)
    return jnp.concatenate([rot, x[..., ROPE_DIM:]], axis=-1)


def stick_breaking_attention(q, k, v):
    s_len = q.shape[2]
    scale = HEAD_DIM ** -0.5
    outs = []
    for start in range(0, s_len, SB_Q_BLOCK):
        stop = start + SB_Q_BLOCK
        qb = q[:, :, start:stop]
        kb = k[:, :, :stop]
        vb = v[:, :, :stop]
        z = jnp.einsum('bhqd,bhkd->bhqk', qb, kb, preferred_element_type=jnp.float32) * scale
        t_pos = jnp.arange(start, stop)[:, None]
        s_pos = jnp.arange(stop)[None, :]
        past = s_pos < t_pos
        log_beta = jax.nn.log_sigmoid(z)
        log_1m = jnp.where(past, jax.nn.log_sigmoid(-z), 0.0)
        rc = lax.cumsum(log_1m, axis=3, reverse=True)
        after = jnp.concatenate([rc[..., 1:], jnp.zeros_like(rc[..., :1])], axis=-1)
        w = jnp.where(past, jnp.exp(log_beta + after), 0.0)
        outs.append(jnp.einsum('bhqk,bhkd->bhqd', w.astype(v.dtype), vb))
    return jnp.concatenate(outs, axis=2)


def moba_attention(q, k, v):
    b, h, s_len, dh = q.shape
    n_blk = -(-s_len // MOBA_BLOCK)
    pad = n_blk * MOBA_BLOCK - s_len
    kp = jnp.pad(k, ((0, 0), (0, 0), (0, pad), (0, 0)))
    vp = jnp.pad(v, ((0, 0), (0, 0), (0, pad), (0, 0)))
    k_blocks = kp.reshape(b, h, n_blk, MOBA_BLOCK, dh)
    v_blocks = vp.reshape(b, h, n_blk, MOBA_BLOCK, dh)
    k_mean = jnp.mean(k_blocks.astype(jnp.float32), axis=3)
    n_cols = max(n_blk, MOBA_TOPK)
    k_mean = jnp.pad(k_mean, ((0, 0), (0, 0), (0, n_cols - n_blk), (0, 0)))
    bidx = jnp.arange(b)[:, None, None, None]
    hidx = jnp.arange(h)[None, :, None, None]
    scale = dh ** -0.5
    sel_len = MOBA_TOPK * MOBA_BLOCK

    def chunk(c):
        start = c * MOBA_Q_CHUNK
        qc = lax.dynamic_slice_in_dim(q, start, MOBA_Q_CHUNK, axis=2)
        t_pos = start + jnp.arange(MOBA_Q_CHUNK)
        own = start // MOBA_BLOCK
        gate = jnp.einsum('bhqd,bhnd->bhqn', qc.astype(jnp.float32), k_mean)
        gate = jnp.where(jnp.arange(n_cols) < own, gate, -jnp.inf)
        _, sel = lax.top_k(gate, MOBA_TOPK)
        sel_valid = jnp.arange(MOBA_TOPK) < own
        sel = jnp.minimum(sel, n_blk - 1)
        k_sel = k_blocks[bidx, hidx, sel]
        v_sel = v_blocks[bidx, hidx, sel]
        s_sel = jnp.einsum('bhqd,bhqrjd->bhqrj', qc, k_sel, preferred_element_type=jnp.float32) * scale
        s_sel = jnp.where(sel_valid[:, None], s_sel, -jnp.inf).reshape(b, h, MOBA_Q_CHUNK, sel_len)
        k_own = lax.dynamic_slice_in_dim(kp, own * MOBA_BLOCK, MOBA_BLOCK, axis=2)
        v_own = lax.dynamic_slice_in_dim(vp, own * MOBA_BLOCK, MOBA_BLOCK, axis=2)
        s_own = jnp.einsum('bhqd,bhkd->bhqk', qc, k_own, preferred_element_type=jnp.float32) * scale
        own_pos = own * MOBA_BLOCK + jnp.arange(MOBA_BLOCK)
        s_own = jnp.where(own_pos[None, :] <= t_pos[:, None], s_own, -jnp.inf)
        p = jax.nn.softmax(jnp.concatenate([s_sel, s_own], axis=-1), axis=-1)
        p_sel = p[..., :sel_len].reshape(b, h, MOBA_Q_CHUNK, MOBA_TOPK, MOBA_BLOCK).astype(v.dtype)
        p_own = p[..., sel_len:].astype(v.dtype)
        return (jnp.einsum('bhqrj,bhqrjd->bhqd', p_sel, v_sel)
                + jnp.einsum('bhqk,bhkd->bhqd', p_own, v_own))

    out = lax.map(chunk, jnp.arange(s_len // MOBA_Q_CHUNK))
    return out.transpose(1, 2, 0, 3, 4).reshape(b, h, s_len, dh)


def gated_parallel_mixer(x, positions, w_in, b_gate, w_branch_sb, w_branch_moba, w_out):
    b, s_len, _ = x.shape
    proj = x @ w_in
    qkv_sb = proj[..., :3 * SB_WIDTH]
    qkv_moba = proj[..., 3 * SB_WIDTH:QKV_WIDTH]
    gate_pre = proj[..., QKV_WIDTH:] + b_gate

    def heads(t, n_heads):
        return t.reshape(b, s_len, n_heads, HEAD_DIM).transpose(0, 2, 1, 3)

    def merge(t):
        return t.transpose(0, 2, 1, 3).reshape(b, s_len, -1)

    q_sb, k_sb, v_sb = [heads(t, SB_HEADS) for t in jnp.split(qkv_sb, 3, axis=-1)]
    q_mb, k_mb, v_mb = [heads(t, MOBA_HEADS) for t in jnp.split(qkv_moba, 3, axis=-1)]
    q_mb = partial_rope(q_mb, positions)
    k_mb = partial_rope(k_mb, positions)

    y_sb = merge(stick_breaking_attention(q_sb, k_sb, v_sb)) @ w_branch_sb
    y_mb = merge(moba_attention(q_mb, k_mb, v_mb)) @ w_branch_moba
    g = jax.nn.sigmoid(gate_pre.astype(jnp.float32)).astype(x.dtype)
    g_sb, g_mb = g[..., :D_MODEL], g[..., D_MODEL:]
    return (g_sb * y_sb + g_mb * y_mb) @ w_out


def routed_expert_ffn(x, w_router, b_router, w_gate_up, b_gate_up, w_down, b_down):
    b, s_len, d = x.shape
    n_tok = b * s_len
    h = x.reshape(n_tok, d)
    logits = (h @ w_router + b_router).astype(jnp.float32)
    top_val, top_idx = lax.top_k(logits, TOP_K)
    top_w = jax.nn.softmax(top_val, axis=-1)
    n_assign = n_tok * TOP_K
    flat_e = top_idx.reshape(n_assign)
    flat_tok = jnp.repeat(jnp.arange(n_tok, dtype=jnp.int32), TOP_K)
    flat_w = top_w.reshape(n_assign)
    order = jnp.argsort(flat_e, stable=True)
    sorted_e = flat_e[order]
    counts = jnp.bincount(flat_e, length=N_EXPERTS)
    padded = (counts + EXPERT_ROWS - 1) // EXPERT_ROWS * EXPERT_ROWS
    start = jnp.cumsum(counts) - counts
    padded_end = jnp.cumsum(padded)
    padded_start = padded_end - padded
    dest = padded_start[sorted_e] + jnp.arange(n_assign) - start[sorted_e]
    n_blocks = -(-(n_assign + N_EXPERTS * (EXPERT_ROWS - 1)) // EXPERT_ROWS)
    n_rows = n_blocks * EXPERT_ROWS
    row_tok = jnp.zeros((n_rows,), jnp.int32).at[dest].set(flat_tok[order])
    row_w = jnp.zeros((n_rows,), jnp.float32).at[dest].set(flat_w[order])
    block_first = jnp.arange(n_blocks) * EXPERT_ROWS
    block_expert = jnp.minimum(jnp.sum(block_first[:, None] >= padded_end[None, :], axis=1), N_EXPERTS - 1)

    def expert_block(args):
        tok, e = args
        xb = h[tok]
        gu = xb @ w_gate_up[e] + b_gate_up[e]
        gate = jnp.minimum(gu[:, 0::2], SWIGLU_LIMIT)
        up = jnp.clip(gu[:, 1::2], -SWIGLU_LIMIT, SWIGLU_LIMIT)
        act = (up + 1.0) * gate * jax.nn.sigmoid(SWIGLU_ALPHA * gate)
        return act @ w_down[e] + b_down[e]

    rows_out = lax.map(expert_block, (row_tok.reshape(n_blocks, EXPERT_ROWS), block_expert))
    weighted = rows_out.reshape(n_rows, d).astype(jnp.float32) * row_w[:, None]
    y = jax.ops.segment_sum(weighted, row_tok, num_segments=n_tok)
    return y.reshape(b, s_len, d).astype(x.dtype)


def _normal(k, shape, scale):
    return jax.random.normal(k, shape, jnp.float32) * scale


def setup_inputs(seed: int = 0) -> dict:
    key = jax.random.key(seed)
    ks = jax.random.split(key, 20)
    L = DEPTH
    x = jax.random.normal(ks[0], (BATCH, SEQ, D_MODEL), jnp.float32)
    offsets = jax.random.randint(ks[1], (BATCH, 1), 0, SEQ, dtype=jnp.int32)
    positions = offsets + jnp.arange(SEQ, dtype=jnp.int32)[None, :]
    col_scale = jnp.concatenate([
        jnp.ones((2 * SB_WIDTH,), jnp.float32),
        jnp.full((SB_WIDTH,), DEEPNORM_BETA, jnp.float32),
        jnp.ones((2 * MOBA_WIDTH,), jnp.float32),
        jnp.full((MOBA_WIDTH,), DEEPNORM_BETA, jnp.float32),
        jnp.ones((N_BRANCHES * D_MODEL,), jnp.float32)])
    w_in = _normal(ks[2], (L, D_MODEL, IN_WIDTH), D_MODEL ** -0.5) * col_scale
    b_gate = _normal(ks[3], (L, N_BRANCHES * D_MODEL), 0.01)
    w_branch_sb = _normal(ks[4], (L, SB_WIDTH, D_MODEL), SB_WIDTH ** -0.5 * DEEPNORM_BETA)
    w_branch_moba = _normal(ks[5], (L, MOBA_WIDTH, D_MODEL), MOBA_WIDTH ** -0.5 * DEEPNORM_BETA)
    w_out = _normal(ks[6], (L, D_MODEL, D_MODEL), D_MODEL ** -0.5 * DEEPNORM_BETA)
    ln_mix_g = 1.0 + _normal(ks[7], (L, D_MODEL), 0.02)
    ln_mix_b = _normal(ks[8], (L, D_MODEL), 0.02)
    w_router = _normal(ks[9], (L, D_MODEL, N_EXPERTS), D_MODEL ** -0.5)
    b_router = _normal(ks[10], (L, N_EXPERTS), 0.01)
    w_gate_up = _normal(ks[11], (L, N_EXPERTS, D_MODEL, 2 * D_FF), D_MODEL ** -0.5)
    b_gate_up = _normal(ks[12], (L, N_EXPERTS, 2 * D_FF), 0.01)
    w_down = _normal(ks[13], (L, N_EXPERTS, D_FF, D_MODEL), D_FF ** -0.5 * DEEPNORM_BETA)
    b_down = _normal(ks[14], (L, N_EXPERTS, D_MODEL), 0.01)
    ln_ffn_g = 1.0 + _normal(ks[15], (L, D_MODEL), 0.02)
    ln_ffn_b = _normal(ks[16], (L, D_MODEL), 0.02)
    return {"x": x, "positions": positions, "w_in": w_in, "b_gate": b_gate,
            "w_branch_sb": w_branch_sb, "w_branch_moba": w_branch_moba, "w_out": w_out,
            "ln_mix_g": ln_mix_g, "ln_mix_b": ln_mix_b, "w_router": w_router,
            "b_router": b_router, "w_gate_up": w_gate_up, "b_gate_up": b_gate_up,
            "w_down": w_down, "b_down": b_down, "ln_ffn_g": ln_ffn_g, "ln_ffn_b": ln_ffn_b}


def reference(x, positions, w_in, b_gate, w_branch_sb, w_branch_moba, w_out, ln_mix_g, ln_mix_b,
              w_router, b_router, w_gate_up, b_gate_up, w_down, b_down, ln_ffn_g, ln_ffn_b):
    h = x
    for layer in range(DEPTH):
        mix = gated_parallel_mixer(h, positions, w_in[layer], b_gate[layer], w_branch_sb[layer],
                                   w_branch_moba[layer], w_out[layer])
        h = layer_norm(DEEPNORM_ALPHA * h + mix, ln_mix_g[layer], ln_mix_b[layer])
        ffn = routed_expert_ffn(h, w_router[layer], b_router[layer], w_gate_up[layer],
                                b_gate_up[layer], w_down[layer], b_down[layer])
        h = layer_norm(DEEPNORM_ALPHA * h + ffn, ln_ffn_g[layer], ln_ffn_b[layer])
    return h
```

```python
import functools

import jax
import jax.numpy as jnp
from jax import lax
from jax.experimental import pallas as pl
from jax.experimental.pallas import tpu as pltpu

F32 = jnp.float32
BF16 = jnp.bfloat16
I32 = jnp.int32

HEAD_DIM = 64
N_HEADS = 8
WIDTH = N_HEADS * HEAD_DIM
LANES = 128
SUBLANES = 8
N_PAIRS = WIDTH // LANES
KV_BLOCK = 256
MOBA_TOPK = 3
ROPE_THETA = 500000.0
ROPE_HALF = 8
N_EXPERTS = 32
TOP_K = 4
SWIGLU_LIMIT = 7.0
SWIGLU_ALPHA = 1.702
EXPERT_ROWS = 256
LN_EPS = 1e-5
DEPTH = 1
DEEPNORM_ALPHA = (2 * DEPTH) ** 0.25
QK_SCALE = HEAD_DIM ** -0.5
NEG_INF = float("-inf")
VMEM_LIMIT = 56 * 1024 * 1024


def _dot(a, b):
    return jnp.dot(a, b, preferred_element_type=F32)


def _qkv_kernel(x_ref, pos_ref, invf_ref, w_ref,
                qt_sb_ref, k_sb_ref, vt_sb_ref, qt_mb_ref, k_mb_ref, vt_mb_ref, *, tm):
    xb = x_ref[...].astype(BF16)
    ang = invf_ref[...] * pos_ref[0]
    cos, sin = jnp.cos(ang), jnp.sin(ang)

    def rope_t(t):
        parts = []
        for base in (0, HEAD_DIM):
            x1 = t[base:base + ROPE_HALF]
            x2 = t[base + ROPE_HALF:base + 2 * ROPE_HALF]
            parts += [x1 * cos - x2 * sin, x2 * cos + x1 * sin, t[base + 2 * ROPE_HALF:base + HEAD_DIM]]
        return jnp.concatenate(parts, axis=0)

    def store_t(ref, p, t):
        tb = t.astype(BF16)
        for blk in range(tm // KV_BLOCK):
            ref[blk, p] = tb[:, blk * KV_BLOCK:(blk + 1) * KV_BLOCK]

    for sec in range(6):
        for half in range(2):
            c0 = sec * WIDTH + half * 2 * LANES
            r = _dot(xb, w_ref[:, c0:c0 + 2 * LANES])
            for q in range(2):
                p = half * 2 + q
                t = r[:, q * LANES:(q + 1) * LANES]
                if sec == 0:
                    store_t(qt_sb_ref, p, (t * QK_SCALE).T)
                elif sec == 1:
                    k_sb_ref[:, p * LANES:(p + 1) * LANES] = t.astype(BF16)
                elif sec == 2:
                    store_t(vt_sb_ref, p, t.T)
                elif sec == 3:
                    store_t(qt_mb_ref, p, rope_t(t.T) * QK_SCALE)
                elif sec == 4:
                    k_mb_ref[:, p * LANES:(p + 1) * LANES] = rope_t(t.T).T.astype(BF16)
                else:
                    store_t(vt_mb_ref, p, t.T)


def _qkv_call(x2, pos3, invf, w_qkv, tm):
    t_tok, d = x2.shape
    nblk = t_tok // KV_BLOCK
    bpt = tm // KV_BLOCK
    t_shape = jax.ShapeDtypeStruct((nblk, N_PAIRS, LANES, KV_BLOCK), BF16)
    n_shape = jax.ShapeDtypeStruct((t_tok, WIDTH), BF16)
    t_spec = pl.BlockSpec((bpt, N_PAIRS, LANES, KV_BLOCK), lambda i: (i, 0, 0, 0))
    n_spec = pl.BlockSpec((tm, WIDTH), lambda i: (i, 0))
    return pl.pallas_call(
        functools.partial(_qkv_kernel, tm=tm),
        out_shape=(t_shape, n_shape, t_shape, t_shape, n_shape, t_shape),
        grid=(t_tok // tm,),
        in_specs=[pl.BlockSpec((tm, d), lambda i: (i, 0)),
                  pl.BlockSpec((1, 1, tm), lambda i: (i, 0, 0)),
                  pl.BlockSpec((ROPE_HALF, 1), lambda i: (0, 0)),
                  pl.BlockSpec((d, 6 * WIDTH), lambda i: (0, 0))],
        out_specs=(t_spec, n_spec, t_spec, t_spec, n_spec, t_spec),
        compiler_params=pltpu.CompilerParams(dimension_semantics=("arbitrary",),
                                             vmem_limit_bytes=VMEM_LIMIT),
        name="qkv_proj",
    )(x2, pos3, invf, w_qkv)


def _head_rows(h):
    row = lax.broadcasted_iota(I32, (LANES, KV_BLOCK), 0)
    return (row >= HEAD_DIM * h) & (row < HEAD_DIM * (h + 1))


def _sb_kernel(qt_ref, k_ref, vt_ref, tri_ref, o_ref):
    i = pl.program_id(2)
    qt = qt_ref[...]
    key = lax.broadcasted_iota(I32, (KV_BLOCK, KV_BLOCK), 0)
    qry = lax.broadcasted_iota(I32, (KV_BLOCK, KV_BLOCK), 1)
    past = key < qry
    tri = tri_ref[...]

    def block(j, qth, acc, carry, diagonal):
        kb = k_ref[pl.ds(pl.multiple_of(j * KV_BLOCK, KV_BLOCK), KV_BLOCK), :]
        z = _dot(kb, qth)
        sp = jnp.maximum(z, 0.0) + jnp.log(1.0 + jnp.exp(-jnp.abs(z)))
        spm = jnp.where(past, sp, 0.0) if diagonal else sp
        after = _dot(tri, spm.astype(BF16)) + carry
        w = jnp.exp(z - sp + after)
        if diagonal:
            w = jnp.where(past, w, 0.0)
        acc = acc + _dot(vt_ref[j], w.astype(BF16))
        carry = carry - jnp.sum(spm, axis=0, keepdims=True)
        return acc, carry

    outs = []
    for h in range(2):
        qth = jnp.where(_head_rows(h), qt, jnp.zeros_like(qt))
        acc = jnp.zeros((LANES, KV_BLOCK), F32)
        carry = jnp.zeros((1, KV_BLOCK), F32)
        acc, carry = block(i, qth, acc, carry, True)
        acc, carry = lax.fori_loop(
            0, i, lambda t, c, qth=qth: block(i - 1 - t, qth, c[0], c[1], False), (acc, carry))
        outs.append(acc)
    acc_t = jnp.where(_head_rows(0), outs[0], outs[1])
    o_ref[...] = acc_t.T.astype(BF16)


def _attn_specs(seq):
    nb = seq // KV_BLOCK
    qt_spec = pl.BlockSpec((None, None, LANES, KV_BLOCK), lambda b, p, i: (b * nb + i, p, 0, 0))
    k_spec = pl.BlockSpec((seq, LANES), lambda b, p, i: (b, p))
    vt_spec = pl.BlockSpec((nb, None, LANES, KV_BLOCK), lambda b, p, i: (b, p, 0, 0))
    o_spec = pl.BlockSpec((KV_BLOCK, LANES), lambda b, p, i: (b * nb + i, p))
    return qt_spec, k_spec, vt_spec, o_spec


def _sb_call(qt, k, vt, tri, batch, seq):
    qt_spec, k_spec, vt_spec, o_spec = _attn_specs(seq)
    return pl.pallas_call(
        _sb_kernel,
        out_shape=jax.ShapeDtypeStruct((batch * seq, WIDTH), BF16),
        grid=(batch, N_PAIRS, seq // KV_BLOCK),
        in_specs=[qt_spec, k_spec, vt_spec, pl.BlockSpec((KV_BLOCK, KV_BLOCK), lambda b, p, i: (0, 0))],
        out_specs=o_spec,
        compiler_params=pltpu.CompilerParams(
            dimension_semantics=("arbitrary", "arbitrary", "arbitrary"), vmem_limit_bytes=VMEM_LIMIT),
        name="stickbreak_attn",
    )(qt, k, vt, tri)


def _moba_kernel(qt_ref, k_ref, vt_ref, avg_ref, o_ref, km_ref, bias_ref, *, nb):
    i = pl.program_id(2)

    @pl.when(i == 0)
    def _():
        km_ref[...] = _dot(avg_ref[...], k_ref[...])

    qt = qt_ref[...]
    key = lax.broadcasted_iota(I32, (KV_BLOCK, KV_BLOCK), 0)
    qry = lax.broadcasted_iota(I32, (KV_BLOCK, KV_BLOCK), 1)
    causal = key <= qry
    blk = lax.broadcasted_iota(I32, (nb, KV_BLOCK), 0)
    km = km_ref[...].astype(BF16)

    outs = []
    for h in range(2):
        qth = jnp.where(_head_rows(h), qt, jnp.zeros_like(qt))
        valid = blk < i
        g = jnp.where(valid, _dot(km, qth), NEG_INF)
        rank = jnp.zeros((nb, KV_BLOCK), F32)
        for jp in range(nb):
            gj = g[jp:jp + 1, :]
            better = jnp.where(gj > g, 1.0, jnp.where(gj == g, jnp.where(blk > jp, 1.0, 0.0), 0.0))
            rank = rank + better
        bias_ref[h] = jnp.where(valid, jnp.where(rank < MOBA_TOPK, 0.0, NEG_INF), NEG_INF)

        kb = k_ref[pl.ds(pl.multiple_of(i * KV_BLOCK, KV_BLOCK), KV_BLOCK), :]
        s = jnp.where(causal, _dot(kb, qth), NEG_INF)
        m = jnp.max(s, axis=0, keepdims=True)
        p = jnp.exp(s - m)
        l = jnp.sum(p, axis=0, keepdims=True)
        acc = _dot(vt_ref[i], p.astype(BF16))

        def body(j, c, qth=qth, h=h):
            m, l, acc = c
            kb = k_ref[pl.ds(pl.multiple_of(j * KV_BLOCK, KV_BLOCK), KV_BLOCK), :]
            s = _dot(kb, qth) + bias_ref[h, pl.ds(j, 1), :]
            m_new = jnp.maximum(m, jnp.max(s, axis=0, keepdims=True))
            a = jnp.exp(m - m_new)
            p = jnp.exp(s - m_new)
            l = l * a + jnp.sum(p, axis=0, keepdims=True)
            acc = acc * a + _dot(vt_ref[j], p.astype(BF16))
            return m_new, l, acc

        m, l, acc = lax.fori_loop(0, i, body, (m, l, acc))
        outs.append(acc / l)
    acc_t = jnp.where(_head_rows(0), outs[0], outs[1])
    o_ref[...] = acc_t.T.astype(BF16)


def _moba_call(qt, k, vt, avg, batch, seq):
    nb = seq // KV_BLOCK
    qt_spec, k_spec, vt_spec, o_spec = _attn_specs(seq)
    return pl.pallas_call(
        functools.partial(_moba_kernel, nb=nb),
        out_shape=jax.ShapeDtypeStruct((batch * seq, WIDTH), BF16),
        grid=(batch, N_PAIRS, nb),
        in_specs=[qt_spec, k_spec, vt_spec, pl.BlockSpec((nb, seq), lambda b, p, i: (0, 0))],
        out_specs=o_spec,
        scratch_shapes=[pltpu.VMEM((nb, LANES), F32), pltpu.VMEM((2, nb, KV_BLOCK), F32)],
        compiler_params=pltpu.CompilerParams(
            dimension_semantics=("arbitrary", "arbitrary", "arbitrary"), vmem_limit_bytes=VMEM_LIMIT),
        name="moba_attn",
    )(qt, k, vt, avg)


def _layer_norm(r, g, b):
    mu = jnp.mean(r, axis=-1, keepdims=True)
    d = r - mu
    var = jnp.mean(d * d, axis=-1, keepdims=True)
    return d * lax.rsqrt(var + LN_EPS) * g + b


def _mixer_kernel(x_ref, asb_ref, amb_ref, wg_ref, bg_ref, wbs_ref, wbm_ref, wo_ref, lng_ref, lnb_ref,
                  wr_ref, br_ref, h_ref, idx_ref, cnt_ref, tw_ref, counts_ref, carry_ref, *, tm, d):
    @pl.when(pl.program_id(0) == 0)
    def _():
        carry_ref[...] = jnp.zeros_like(carry_ref)

    x = x_ref[...]
    g = jax.nn.sigmoid(_dot(x.astype(BF16), wg_ref[...]) + bg_ref[...])
    mixed = g[:, :d] * _dot(asb_ref[...], wbs_ref[...]) + g[:, d:] * _dot(amb_ref[...], wbm_ref[...])
    r = DEEPNORM_ALPHA * x + _dot(mixed.astype(BF16), wo_ref[...])
    h = _layer_norm(r, lng_ref[...], lnb_ref[...])
    h_ref[...] = h

    logits = _dot(h.astype(BF16), wr_ref[...]) + br_ref[...]
    lane = lax.broadcasted_iota(I32, (tm, LANES), 1)
    vals, idxs = [], []
    member = jnp.zeros((tm, LANES), F32)
    for _ in range(TOP_K):
        mx = jnp.max(logits, axis=-1, keepdims=True)
        ik = jnp.min(jnp.where(logits == mx, lane, LANES), axis=-1, keepdims=True)
        hit = lane == ik
        vals.append(mx)
        idxs.append(ik)
        member = jnp.where(hit, 1.0, member)
        logits = jnp.where(hit, NEG_INF, logits)
    es = [jnp.exp(v - vals[0]) for v in vals]
    den = es[0] + es[1] + es[2] + es[3]

    rt = lax.broadcasted_iota(I32, (tm, tm), 0)
    ct = lax.broadcasted_iota(I32, (tm, tm), 1)
    lower = jnp.where(ct < rt, 1.0, 0.0).astype(BF16)
    cnt = _dot(lower, member.astype(BF16)) + carry_ref[...]
    carry_ref[...] = carry_ref[...] + jnp.sum(member, axis=0, keepdims=True)
    counts_ref[...] = carry_ref[...].astype(I32)

    lane8 = lax.broadcasted_iota(I32, (tm, 8), 1)
    idx_out = jnp.zeros((tm, 8), I32)
    cnt_out = jnp.zeros((tm, 8), I32)
    tw_out = jnp.zeros((tm, 8), F32)
    for k in range(TOP_K):
        ck = jnp.sum(jnp.where(lane == idxs[k], cnt, 0.0), axis=-1, keepdims=True)
        idx_out = jnp.where(lane8 == k, idxs[k], idx_out)
        cnt_out = jnp.where(lane8 == k, ck.astype(I32), cnt_out)
        tw_out = jnp.where(lane8 == k, es[k] / den, tw_out)
    idx_ref[...] = idx_out
    cnt_ref[...] = cnt_out
    tw_ref[...] = tw_out


def _mixer_call(x2, a_sb, a_mb, wg, bg, wbs, wbm, wo, lng, lnb, wr, br, tm):
    t_tok, d = x2.shape
    const = lambda shape: pl.BlockSpec(shape, lambda i: (0,) * len(shape))
    row8 = pl.BlockSpec((tm, 8), lambda i: (i, 0))
    return pl.pallas_call(
        functools.partial(_mixer_kernel, tm=tm, d=d),
        out_shape=(jax.ShapeDtypeStruct((t_tok, d), F32),
                   jax.ShapeDtypeStruct((t_tok, 8), I32),
                   jax.ShapeDtypeStruct((t_tok, 8), I32),
                   jax.ShapeDtypeStruct((t_tok, 8), F32),
                   jax.ShapeDtypeStruct((1, LANES), I32)),
        grid=(t_tok // tm,),
        in_specs=[pl.BlockSpec((tm, d), lambda i: (i, 0)),
                  pl.BlockSpec((tm, WIDTH), lambda i: (i, 0)),
                  pl.BlockSpec((tm, WIDTH), lambda i: (i, 0)),
                  const((d, 2 * d)), const((1, 2 * d)),
                  const((WIDTH, d)), const((WIDTH, d)), const((d, d)),
                  const((1, d)), const((1, d)),
                  const((d, LANES)), const((1, LANES))],
        out_specs=(pl.BlockSpec((tm, d), lambda i: (i, 0)), row8, row8, row8, const((1, LANES))),
        scratch_shapes=[pltpu.VMEM((1, LANES), F32)],
        compiler_params=pltpu.CompilerParams(dimension_semantics=("arbitrary",),
                                             vmem_limit_bytes=VMEM_LIMIT),
        name="mixer_ln_router",
    )(x2, a_sb, a_mb, wg, bg, wbs, wbm, wo, lng, lnb, wr, br)


def _row_copy(src_ref, s, dst_ref, t, sem):
    return pltpu.make_async_copy(src_ref.at[pl.ds(s, 1)], dst_ref.at[pl.ds(t, 1)], sem)


def _dispatch_kernel(idx_ref, cnt_ref, start_ref, counts_ref, h_ref, xg_ref, zero_ref, sem, zsem, *, tm, n_pad):
    base = pl.program_id(0) * tm

    @pl.when(pl.program_id(0) == 0)
    def _():
        zero_ref[...] = jnp.zeros_like(zero_ref)

        def pad_expert(e, _):
            n = (EXPERT_ROWS - counts_ref[e] % EXPERT_ROWS) % EXPERT_ROWS
            off = start_ref[e] + counts_ref[e]
            for s in range(SUBLANES - 1):
                @pl.when(s < (n & (SUBLANES - 1)))
                def _(row=off + s):
                    _row_copy(zero_ref, 0, xg_ref, row, zsem).start()
            off = off + (n & (SUBLANES - 1))
            size = SUBLANES
            while size < EXPERT_ROWS:
                @pl.when((n & size) != 0)
                def _(off=off, size=size):
                    dst = xg_ref.at[pl.ds(pl.multiple_of(off, SUBLANES), size)]
                    pltpu.make_async_copy(zero_ref.at[pl.ds(0, size)], dst, zsem).start()
                off = off + (n & size)
                size *= 2
            return off

        used = lax.fori_loop(0, N_EXPERTS, pad_expert, 0)

        def pad_block(b, _):
            first = pl.multiple_of(used + b * EXPERT_ROWS, EXPERT_ROWS)
            pltpu.make_async_copy(zero_ref, xg_ref.at[pl.ds(first, EXPERT_ROWS)], zsem).start()
            return 0

        lax.fori_loop(0, (xg_ref.shape[0] - used) // EXPERT_ROWS, pad_block, 0)
        pltpu.make_async_copy(xg_ref.at[pl.ds(0, n_pad)], xg_ref.at[pl.ds(0, n_pad)], zsem).wait()

    def issue(t, _):
        for k in range(TOP_K):
            a = t * TOP_K + k
            dest = start_ref[idx_ref[a]] + cnt_ref[a]
            _row_copy(h_ref, base + t, xg_ref, dest, sem).start()
        return 0

    lax.fori_loop(0, tm, issue, 0)
    pltpu.make_async_copy(xg_ref.at[pl.ds(0, tm * TOP_K)], xg_ref.at[pl.ds(0, tm * TOP_K)], sem).wait()


def _dispatch_call(idx_flat, cnt_flat, start, counts, h, n_rows, tm):
    t_tok, d = h.shape
    smem_blk = pl.BlockSpec((tm * TOP_K,), lambda i: (i,), memory_space=pltpu.SMEM)
    smem = pl.BlockSpec(memory_space=pltpu.SMEM)
    return pl.pallas_call(
        functools.partial(_dispatch_kernel, tm=tm, n_pad=n_rows - t_tok * TOP_K),
        out_shape=jax.ShapeDtypeStruct((n_rows, d), F32),
        grid=(t_tok // tm,),
        in_specs=[smem_blk, smem_blk, smem, smem, pl.BlockSpec(memory_space=pl.ANY)],
        out_specs=pl.BlockSpec(memory_space=pl.ANY),
        scratch_shapes=[pltpu.VMEM((EXPERT_ROWS, d), F32), pltpu.SemaphoreType.DMA(()),
                        pltpu.SemaphoreType.DMA(())],
        compiler_params=pltpu.CompilerParams(dimension_semantics=("arbitrary",),
                                             vmem_limit_bytes=VMEM_LIMIT),
        name="moe_dispatch",
    )(idx_flat, cnt_flat, start, counts, h)


def _expert_kernel(be_ref, nu_ref, x_ref, wg_ref, wu_ref, wd_ref, bg_ref, bu_ref, bd_ref, o_ref):
    i = pl.program_id(0)

    @pl.when(i < nu_ref[0])
    def _():
        x = x_ref[...].astype(BF16)
        gate = jnp.minimum(_dot(x, wg_ref[...]) + bg_ref[...], SWIGLU_LIMIT)
        up = jnp.clip(_dot(x, wu_ref[...]) + bu_ref[...], -SWIGLU_LIMIT, SWIGLU_LIMIT)
        act = (up + 1.0) * gate * jax.nn.sigmoid(SWIGLU_ALPHA * gate)
        o_ref[...] = _dot(act.astype(BF16), wd_ref[...]) + bd_ref[...]

    @pl.when(i >= nu_ref[0])
    def _():
        o_ref[...] = jnp.zeros_like(o_ref)


def _expert_call(block_expert, n_used, xg, wg, wu, wd, bg, bu, bd):
    n_rows, d = xg.shape
    f = wg.shape[-1]
    n_blocks = n_rows // EXPERT_ROWS
    blk = lambda i, be, nu: (jnp.minimum(i, nu[0] - 1), 0)
    wsel = lambda i, be, nu: (be[i], 0, 0)
    return pl.pallas_call(
        _expert_kernel,
        out_shape=jax.ShapeDtypeStruct((n_rows, d), F32),
        grid_spec=pltpu.PrefetchScalarGridSpec(
            num_scalar_prefetch=2,
            grid=(n_blocks,),
            in_specs=[pl.BlockSpec((EXPERT_ROWS, d), blk),
                      pl.BlockSpec((None, d, f), wsel),
                      pl.BlockSpec((None, d, f), wsel),
                      pl.BlockSpec((None, f, d), wsel),
                      pl.BlockSpec((None, 1, f), wsel),
                      pl.BlockSpec((None, 1, f), wsel),
                      pl.BlockSpec((None, 1, d), wsel)],
            out_specs=pl.BlockSpec((EXPERT_ROWS, d), lambda i, be, nu: (i, 0))),
        compiler_params=pltpu.CompilerParams(dimension_semantics=("arbitrary",),
                                             vmem_limit_bytes=VMEM_LIMIT),
        name="expert_ffn",
    )(block_expert, n_used, xg, wg, wu, wd, bg, bu, bd)


def _combine_kernel(idx_ref, cnt_ref, start_ref, h_ref, tw_ref, lng_ref, lnb_ref, rows_ref, o_ref,
                    buf_ref, sem, *, tm):
    def issue(t, _):
        for k in range(TOP_K):
            a = t * TOP_K + k
            src = start_ref[idx_ref[a]] + cnt_ref[a]
            _row_copy(rows_ref, src, buf_ref.at[k], t, sem).start()
        return 0

    lax.fori_loop(0, tm, issue, 0)
    pltpu.make_async_copy(buf_ref, buf_ref, sem).wait()
    tw = tw_ref[...]
    y = tw[:, 0:1] * buf_ref[0]
    for k in range(1, TOP_K):
        y = y + tw[:, k:k + 1] * buf_ref[k]
    o_ref[...] = _layer_norm(DEEPNORM_ALPHA * h_ref[...] + y, lng_ref[...], lnb_ref[...])


def _combine_call(idx_flat, cnt_flat, start, h, tw, lng, lnb, rows, tm):
    t_tok, d = h.shape
    smem_blk = pl.BlockSpec((tm * TOP_K,), lambda i: (i,), memory_space=pltpu.SMEM)
    return pl.pallas_call(
        functools.partial(_combine_kernel, tm=tm),
        out_shape=jax.ShapeDtypeStruct((t_tok, d), F32),
        grid=(t_tok // tm,),
        in_specs=[smem_blk, smem_blk,
                  pl.BlockSpec(memory_space=pltpu.SMEM),
                  pl.BlockSpec((tm, d), lambda i: (i, 0)),
                  pl.BlockSpec((tm, 8), lambda i: (i, 0)),
                  pl.BlockSpec((1, d), lambda i: (0, 0)),
                  pl.BlockSpec((1, d), lambda i: (0, 0)),
                  pl.BlockSpec(memory_space=pl.ANY)],
        out_specs=pl.BlockSpec((tm, d), lambda i: (i, 0)),
        scratch_shapes=[pltpu.VMEM((TOP_K, tm, d), F32), pltpu.SemaphoreType.DMA(())],
        compiler_params=pltpu.CompilerParams(dimension_semantics=("arbitrary",),
                                             vmem_limit_bytes=VMEM_LIMIT),
        name="moe_combine_ln",
    )(idx_flat, cnt_flat, start, h, tw, lng, lnb, rows)


def _layer(x2, pos, batch, seq, w_in, b_gate, w_branch_sb, w_branch_moba, w_out, ln_mix_g, ln_mix_b,
           w_router, b_router, w_gate_up, b_gate_up, w_down, b_down, ln_ffn_g, ln_ffn_b):
    t_tok, d = x2.shape
    nb = seq // KV_BLOCK
    qkv_w = 6 * WIDTH
    tm_a = 512 if t_tok % 512 == 0 else KV_BLOCK

    w_qkv = w_in[:, :qkv_w].astype(BF16)
    w_g = w_in[:, qkv_w:].astype(BF16)
    pos3 = pos.astype(F32).reshape(t_tok // tm_a, 1, tm_a)
    invf = (ROPE_THETA ** (-jnp.arange(ROPE_HALF, dtype=F32) / ROPE_HALF)).reshape(ROPE_HALF, 1)
    ki = jnp.arange(KV_BLOCK)
    tri = jnp.where(ki[None, :] > ki[:, None], -1.0, 0.0).astype(BF16)
    avg = jnp.where(jnp.arange(seq)[None, :] // KV_BLOCK == jnp.arange(nb)[:, None],
                    1.0 / KV_BLOCK, 0.0).astype(BF16)

    qt_sb, k_sb, vt_sb, qt_mb, k_mb, vt_mb = _qkv_call(x2, pos3, invf, w_qkv, tm_a)
    a_sb = _sb_call(qt_sb, k_sb, vt_sb, tri, batch, seq)
    a_mb = _moba_call(qt_mb, k_mb, vt_mb, avg, batch, seq)

    w_r = jnp.zeros((d, LANES), F32).at[:, :N_EXPERTS].set(w_router).astype(BF16)
    b_r = jnp.full((1, LANES), NEG_INF, F32).at[0, :N_EXPERTS].set(b_router)
    h, idx8, cnt8, tw8, counts = _mixer_call(
        x2, a_sb, a_mb, w_g, b_gate.reshape(1, -1), w_branch_sb.astype(BF16), w_branch_moba.astype(BF16),
        w_out.astype(BF16), ln_mix_g.reshape(1, -1), ln_mix_b.reshape(1, -1), w_r, b_r, tm_a)

    counts = counts[0, :N_EXPERTS]
    padded = (counts + EXPERT_ROWS - 1) // EXPERT_ROWS * EXPERT_ROWS
    padded_end = jnp.cumsum(padded)
    start = (padded_end - padded).astype(I32)
    n_assign = t_tok * TOP_K
    n_blocks = -(-(n_assign + N_EXPERTS * (EXPERT_ROWS - 1)) // EXPERT_ROWS)
    block_first = jnp.arange(n_blocks, dtype=I32) * EXPERT_ROWS
    block_expert = jnp.minimum(jnp.sum(block_first[:, None] >= padded_end[None, :], axis=1),
                               N_EXPERTS - 1).astype(I32)
    n_used = (padded_end[-1:] // EXPERT_ROWS).astype(I32)

    idx_flat = idx8[:, :TOP_K].reshape(-1)
    cnt_flat = cnt8[:, :TOP_K].reshape(-1)
    xg = _dispatch_call(idx_flat, cnt_flat, start, counts, h, n_blocks * EXPERT_ROWS, tm_a)

    f = w_down.shape[1]
    w_gu = w_gate_up.astype(BF16)
    rows = _expert_call(block_expert, n_used, xg,
                        w_gu[:, :, 0::2], w_gu[:, :, 1::2], w_down.astype(BF16),
                        b_gate_up[:, 0::2].reshape(N_EXPERTS, 1, f), b_gate_up[:, 1::2].reshape(N_EXPERTS, 1, f),
                        b_down.reshape(N_EXPERTS, 1, d))
    return _combine_call(idx_flat, cnt_flat, start, h, tw8, ln_ffn_g.reshape(1, -1),
                         ln_ffn_b.reshape(1, -1), rows, KV_BLOCK)


def kernel(x, positions, w_in, b_gate, w_branch_sb, w_branch_moba, w_out, ln_mix_g, ln_mix_b, w_router,
           b_router, w_gate_up, b_gate_up, w_down, b_down, ln_ffn_g, ln_ffn_b):
    batch, seq, d = x.shape
    h = x.reshape(batch * seq, d)
    pos = positions.reshape(batch * seq)
    for layer in range(w_in.shape[0]):
        h = _layer(h, pos, batch, seq, w_in[layer], b_gate[layer], w_branch_sb[layer], w_branch_moba[layer],
                   w_out[layer], ln_mix_g[layer], ln_mix_b[layer], w_router[layer], b_router[layer],
                   w_gate_up[layer], b_gate_up[layer], w_down[layer], b_down[layer],
                   ln_ffn_g[layer], ln_ffn_b[layer])
    return h.reshape(batch, seq, d)
```

```python
import functools

import jax
import jax.numpy as jnp
from jax import lax
from jax.experimental import pallas as pl
from jax.experimental.pallas import tpu as pltpu

F32 = jnp.float32
BF16 = jnp.bfloat16
I32 = jnp.int32

HEAD_DIM = 64
N_HEADS = 8
WIDTH = N_HEADS * HEAD_DIM
LANES = 128
SUBLANES = 8
N_PAIRS = WIDTH // LANES
KV_BLOCK = 256
MOBA_TOPK = 3
ROPE_THETA = 500000.0
ROPE_HALF = 8
N_EXPERTS = 32
TOP_K = 4
SWIGLU_LIMIT = 7.0
SWIGLU_ALPHA = 1.702
EXPERT_ROWS = 256
LN_EPS = 1e-5
DEPTH = 1
DEEPNORM_ALPHA = (2 * DEPTH) ** 0.25
QK_SCALE = HEAD_DIM ** -0.5
NEG_INF = float("-inf")
VMEM_LIMIT = 56 * 1024 * 1024


def _dot(a, b):
    return jnp.dot(a, b, preferred_element_type=F32)


def _qkv_kernel(x_ref, pos_ref, invf_ref, w_ref,
                qt_sb_ref, k_sb_ref, vt_sb_ref, qt_mb_ref, k_mb_ref, vt_mb_ref, *, tm):
    xb = x_ref[...].astype(BF16)
    ang = invf_ref[...] * pos_ref[0]
    cos, sin = jnp.cos(ang), jnp.sin(ang)

    def rope_t(t):
        parts = []
        for base in (0, HEAD_DIM):
            x1 = t[base:base + ROPE_HALF]
            x2 = t[base + ROPE_HALF:base + 2 * ROPE_HALF]
            parts += [x1 * cos - x2 * sin, x2 * cos + x1 * sin, t[base + 2 * ROPE_HALF:base + HEAD_DIM]]
        return jnp.concatenate(parts, axis=0)

    def store_t(ref, p, t):
        tb = t.astype(BF16)
        for blk in range(tm // KV_BLOCK):
            ref[blk, p] = tb[:, blk * KV_BLOCK:(blk + 1) * KV_BLOCK]

    for sec in range(6):
        for half in range(2):
            c0 = sec * WIDTH + half * 2 * LANES
            r = _dot(xb, w_ref[:, c0:c0 + 2 * LANES])
            for q in range(2):
                p = half * 2 + q
                t = r[:, q * LANES:(q + 1) * LANES]
                if sec == 0:
                    store_t(qt_sb_ref, p, (t * QK_SCALE).T)
                elif sec == 1:
                    k_sb_ref[:, p * LANES:(p + 1) * LANES] = t.astype(BF16)
                elif sec == 2:
                    store_t(vt_sb_ref, p, t.T)
                elif sec == 3:
                    store_t(qt_mb_ref, p, rope_t(t.T) * QK_SCALE)
                elif sec == 4:
                    k_mb_ref[:, p * LANES:(p + 1) * LANES] = rope_t(t.T).T.astype(BF16)
                else:
                    store_t(vt_mb_ref, p, t.T)


def _qkv_call(x2, pos3, invf, w_qkv, tm):
    t_tok, d = x2.shape
    nblk = t_tok // KV_BLOCK
    bpt = tm // KV_BLOCK
    t_shape = jax.ShapeDtypeStruct((nblk, N_PAIRS, LANES, KV_BLOCK), BF16)
    n_shape = jax.ShapeDtypeStruct((t_tok, WIDTH), BF16)
    t_spec = pl.BlockSpec((bpt, N_PAIRS, LANES, KV_BLOCK), lambda i: (i, 0, 0, 0))
    n_spec = pl.BlockSpec((tm, WIDTH), lambda i: (i, 0))
    return pl.pallas_call(
        functools.partial(_qkv_kernel, tm=tm),
        out_shape=(t_shape, n_shape, t_shape, t_shape, n_shape, t_shape),
        grid=(t_tok // tm,),
        in_specs=[pl.BlockSpec((tm, d), lambda i: (i, 0)),
                  pl.BlockSpec((1, 1, tm), lambda i: (i, 0, 0)),
                  pl.BlockSpec((ROPE_HALF, 1), lambda i: (0, 0)),
                  pl.BlockSpec((d, 6 * WIDTH), lambda i: (0, 0))],
        out_specs=(t_spec, n_spec, t_spec, t_spec, n_spec, t_spec),
        compiler_params=pltpu.CompilerParams(dimension_semantics=("arbitrary",),
                                             vmem_limit_bytes=VMEM_LIMIT),
        name="qkv_proj",
    )(x2, pos3, invf, w_qkv)


def _head_rows(h):
    row = lax.broadcasted_iota(I32, (LANES, KV_BLOCK), 0)
    return (row >= HEAD_DIM * h) & (row < HEAD_DIM * (h + 1))


def _sb_kernel(qt_ref, k_ref, vt_ref, tri_ref, o_ref, acc_ref):
    i = pl.program_id(2)
    qt = qt_ref[...]
    qths = [jnp.where(_head_rows(h), qt, jnp.zeros_like(qt)) for h in range(2)]
    key = lax.broadcasted_iota(I32, (KV_BLOCK, KV_BLOCK), 0)
    qry = lax.broadcasted_iota(I32, (KV_BLOCK, KV_BLOCK), 1)
    past = key < qry
    tri = tri_ref[...]
    acc_ref[...] = jnp.zeros_like(acc_ref)

    def block(j, h, carry, diagonal):
        kb = k_ref[pl.ds(pl.multiple_of(j * KV_BLOCK, KV_BLOCK), KV_BLOCK), :]
        z = _dot(kb, qths[h])
        sp = jnp.maximum(z, 0.0) + jnp.log(1.0 + jnp.exp(-jnp.abs(z)))
        spm = jnp.where(past, sp, 0.0) if diagonal else sp
        after = _dot(tri, spm.astype(BF16)) + carry
        w = jnp.exp(z - sp + after)
        if diagonal:
            w = jnp.where(past, w, 0.0)
        rows = slice(h * HEAD_DIM, (h + 1) * HEAD_DIM)
        acc_ref[rows, :] += _dot(vt_ref[j, rows, :], w.astype(BF16))
        return carry - jnp.sum(spm, axis=0, keepdims=True)

    def both(j, c, diagonal):
        return block(j, 0, c[0], diagonal), block(j, 1, c[1], diagonal)

    def pair(t, c):
        j = i - 1 - 2 * t
        return both(j - 1, both(j, c, False), False)

    zero = jnp.zeros((1, KV_BLOCK), F32)
    c = both(i, (zero, zero), True)
    c = lax.fori_loop(0, i // 2, pair, c)

    @pl.when(i % 2 == 1)
    def _():
        both(0, c, False)

    o_ref[...] = acc_ref[...].T.astype(BF16)


def _attn_specs(seq):
    nb = seq // KV_BLOCK
    qt_spec = pl.BlockSpec((None, None, LANES, KV_BLOCK), lambda b, p, i: (b * nb + i, p, 0, 0))
    k_spec = pl.BlockSpec((seq, LANES), lambda b, p, i: (b, p))
    vt_spec = pl.BlockSpec((nb, None, LANES, KV_BLOCK), lambda b, p, i: (b, p, 0, 0))
    o_spec = pl.BlockSpec((KV_BLOCK, LANES), lambda b, p, i: (b * nb + i, p))
    return qt_spec, k_spec, vt_spec, o_spec


def _sb_call(qt, k, vt, tri, batch, seq):
    qt_spec, k_spec, vt_spec, o_spec = _attn_specs(seq)
    return pl.pallas_call(
        _sb_kernel,
        out_shape=jax.ShapeDtypeStruct((batch * seq, WIDTH), BF16),
        grid=(batch, N_PAIRS, seq // KV_BLOCK),
        in_specs=[qt_spec, k_spec, vt_spec, pl.BlockSpec((KV_BLOCK, KV_BLOCK), lambda b, p, i: (0, 0))],
        out_specs=o_spec,
        scratch_shapes=[pltpu.VMEM((LANES, KV_BLOCK), F32)],
        compiler_params=pltpu.CompilerParams(
            dimension_semantics=("arbitrary", "arbitrary", "arbitrary"), vmem_limit_bytes=VMEM_LIMIT),
        name="stickbreak_attn",
    )(qt, k, vt, tri)


def _moba_kernel(qt_ref, k_ref, vt_ref, avg_ref, o_ref, km_ref, bias_ref, acc_ref, *, nb):
    i = pl.program_id(2)

    @pl.when(i == 0)
    def _():
        km_ref[...] = _dot(avg_ref[...], k_ref[...])

    qt = qt_ref[...]
    key = lax.broadcasted_iota(I32, (KV_BLOCK, KV_BLOCK), 0)
    qry = lax.broadcasted_iota(I32, (KV_BLOCK, KV_BLOCK), 1)
    causal = key <= qry
    blk = lax.broadcasted_iota(I32, (nb, KV_BLOCK), 0)
    km = km_ref[...].astype(BF16)

    qths = [jnp.where(_head_rows(h), qt, jnp.zeros_like(qt)) for h in range(2)]
    head_rows = [slice(h * HEAD_DIM, (h + 1) * HEAD_DIM) for h in range(2)]
    for h in range(2):
        valid = blk < i
        g = jnp.where(valid, _dot(km, qths[h]), NEG_INF)
        rank = jnp.zeros((nb, KV_BLOCK), F32)
        for jp in range(nb):
            gj = g[jp:jp + 1, :]
            better = jnp.where(gj > g, 1.0, jnp.where(gj == g, jnp.where(blk > jp, 1.0, 0.0), 0.0))
            rank = rank + better
        bias_ref[h] = jnp.where(valid, jnp.where(rank < MOBA_TOPK, 0.0, NEG_INF), NEG_INF)

    def keys(j):
        return k_ref[pl.ds(pl.multiple_of(j * KV_BLOCK, KV_BLOCK), KV_BLOCK), :]

    def own(h):
        s = jnp.where(causal, _dot(keys(i), qths[h]), NEG_INF)
        m = jnp.max(s, axis=0, keepdims=True)
        p = jnp.exp(s - m)
        acc_ref[head_rows[h], :] = _dot(vt_ref[i, head_rows[h], :], p.astype(BF16))
        return m, jnp.sum(p, axis=0, keepdims=True)

    def block(j, h, ml):
        m, l = ml
        s = _dot(keys(j), qths[h]) + bias_ref[h, pl.ds(j, 1), :]
        m_new = jnp.maximum(m, jnp.max(s, axis=0, keepdims=True))
        a = jnp.exp(m - m_new)
        p = jnp.exp(s - m_new)
        acc_ref[head_rows[h], :] = acc_ref[head_rows[h], :] * a + _dot(vt_ref[j, head_rows[h], :], p.astype(BF16))
        return m_new, l * a + jnp.sum(p, axis=0, keepdims=True)

    def both(j, c):
        return block(j, 0, c[0]), block(j, 1, c[1])

    c = (own(0), own(1))
    c = lax.fori_loop(0, i // 2, lambda t, c: both(2 * t + 1, both(2 * t, c)), c)
    c = lax.cond(i % 2 == 1, lambda c: both(i - 1, c), lambda c: c, c)
    for h in range(2):
        acc_ref[head_rows[h], :] = acc_ref[head_rows[h], :] / c[h][1]
    o_ref[...] = acc_ref[...].T.astype(BF16)


def _moba_call(qt, k, vt, avg, batch, seq):
    nb = seq // KV_BLOCK
    qt_spec, k_spec, vt_spec, o_spec = _attn_specs(seq)
    return pl.pallas_call(
        functools.partial(_moba_kernel, nb=nb),
        out_shape=jax.ShapeDtypeStruct((batch * seq, WIDTH), BF16),
        grid=(batch, N_PAIRS, nb),
        in_specs=[qt_spec, k_spec, vt_spec, pl.BlockSpec((nb, seq), lambda b, p, i: (0, 0))],
        out_specs=o_spec,
        scratch_shapes=[pltpu.VMEM((nb, LANES), F32), pltpu.VMEM((2, nb, KV_BLOCK), F32),
                        pltpu.VMEM((LANES, KV_BLOCK), F32)],
        compiler_params=pltpu.CompilerParams(
            dimension_semantics=("arbitrary", "arbitrary", "arbitrary"), vmem_limit_bytes=VMEM_LIMIT),
        name="moba_attn",
    )(qt, k, vt, avg)


def _layer_norm(r, g, b):
    mu = jnp.mean(r, axis=-1, keepdims=True)
    d = r - mu
    var = jnp.mean(d * d, axis=-1, keepdims=True)
    return d * lax.rsqrt(var + LN_EPS) * g + b


def _mixer_kernel(x_ref, asb_ref, amb_ref, wg_ref, bg_ref, wbs_ref, wbm_ref, wo_ref, lng_ref, lnb_ref,
                  wr_ref, br_ref, h_ref, idx_ref, cnt_ref, tw_ref, counts_ref, carry_ref, *, tm, d):
    @pl.when(pl.program_id(0) == 0)
    def _():
        carry_ref[...] = jnp.zeros_like(carry_ref)

    x = x_ref[...]
    g = jax.nn.sigmoid(_dot(x.astype(BF16), wg_ref[...]) + bg_ref[...])
    mixed = g[:, :d] * _dot(asb_ref[...], wbs_ref[...]) + g[:, d:] * _dot(amb_ref[...], wbm_ref[...])
    r = DEEPNORM_ALPHA * x + _dot(mixed.astype(BF16), wo_ref[...])
    h = _layer_norm(r, lng_ref[...], lnb_ref[...])
    h_ref[...] = h

    logits = _dot(h.astype(BF16), wr_ref[...]) + br_ref[...]
    lane = lax.broadcasted_iota(I32, (tm, LANES), 1)
    vals, idxs = [], []
    member = jnp.zeros((tm, LANES), F32)
    for _ in range(TOP_K):
        mx = jnp.max(logits, axis=-1, keepdims=True)
        ik = jnp.min(jnp.where(logits == mx, lane, LANES), axis=-1, keepdims=True)
        hit = lane == ik
        vals.append(mx)
        idxs.append(ik)
        member = jnp.where(hit, 1.0, member)
        logits = jnp.where(hit, NEG_INF, logits)
    es = [jnp.exp(v - vals[0]) for v in vals]
    den = es[0] + es[1] + es[2] + es[3]

    rt = lax.broadcasted_iota(I32, (tm, tm), 0)
    ct = lax.broadcasted_iota(I32, (tm, tm), 1)
    lower = jnp.where(ct < rt, 1.0, 0.0).astype(BF16)
    cnt = _dot(lower, member.astype(BF16)) + carry_ref[...]
    carry_ref[...] = carry_ref[...] + jnp.sum(member, axis=0, keepdims=True)
    counts_ref[...] = carry_ref[...].astype(I32)

    lane8 = lax.broadcasted_iota(I32, (tm, 8), 1)
    idx_out = jnp.zeros((tm, 8), I32)
    cnt_out = jnp.zeros((tm, 8), I32)
    tw_out = jnp.zeros((tm, 8), F32)
    for k in range(TOP_K):
        ck = jnp.sum(jnp.where(lane == idxs[k], cnt, 0.0), axis=-1, keepdims=True)
        idx_out = jnp.where(lane8 == k, idxs[k], idx_out)
        cnt_out = jnp.where(lane8 == k, ck.astype(I32), cnt_out)
        tw_out = jnp.where(lane8 == k, es[k] / den, tw_out)
    idx_ref[...] = idx_out
    cnt_ref[...] = cnt_out
    tw_ref[...] = tw_out


def _mixer_call(x2, a_sb, a_mb, wg, bg, wbs, wbm, wo, lng, lnb, wr, br, tm):
    t_tok, d = x2.shape
    const = lambda shape: pl.BlockSpec(shape, lambda i: (0,) * len(shape))
    row8 = pl.BlockSpec((tm, 8), lambda i: (i, 0))
    return pl.pallas_call(
        functools.partial(_mixer_kernel, tm=tm, d=d),
        out_shape=(jax.ShapeDtypeStruct((t_tok, d), F32),
                   jax.ShapeDtypeStruct((t_tok, 8), I32),
                   jax.ShapeDtypeStruct((t_tok, 8), I32),
                   jax.ShapeDtypeStruct((t_tok, 8), F32),
                   jax.ShapeDtypeStruct((1, LANES), I32)),
        grid=(t_tok // tm,),
        in_specs=[pl.BlockSpec((tm, d), lambda i: (i, 0)),
                  pl.BlockSpec((tm, WIDTH), lambda i: (i, 0)),
                  pl.BlockSpec((tm, WIDTH), lambda i: (i, 0)),
                  const((d, 2 * d)), const((1, 2 * d)),
                  const((WIDTH, d)), const((WIDTH, d)), const((d, d)),
                  const((1, d)), const((1, d)),
                  const((d, LANES)), const((1, LANES))],
        out_specs=(pl.BlockSpec((tm, d), lambda i: (i, 0)), row8, row8, row8, const((1, LANES))),
        scratch_shapes=[pltpu.VMEM((1, LANES), F32)],
        compiler_params=pltpu.CompilerParams(dimension_semantics=("arbitrary",),
                                             vmem_limit_bytes=VMEM_LIMIT),
        name="mixer_ln_router",
    )(x2, a_sb, a_mb, wg, bg, wbs, wbm, wo, lng, lnb, wr, br)


def _row_copy(src_ref, s, dst_ref, t, sem):
    return pltpu.make_async_copy(src_ref.at[pl.ds(s, 1)], dst_ref.at[pl.ds(t, 1)], sem)


def _dispatch_kernel(idx_ref, cnt_ref, start_ref, counts_ref, h_ref, xg_ref, zero_ref, sem, zsem, *, tm, n_pad):
    @pl.when(pl.program_id(0) == 0)
    def _():
        zero_ref[...] = jnp.zeros_like(zero_ref)

        def pad_expert(e, _):
            n = (EXPERT_ROWS - counts_ref[e] % EXPERT_ROWS) % EXPERT_ROWS
            off = start_ref[e] + counts_ref[e]
            for s in range(SUBLANES - 1):
                @pl.when(s < (n & (SUBLANES - 1)))
                def _(row=off + s):
                    _row_copy(zero_ref, 0, xg_ref, row, zsem).start()
            off = off + (n & (SUBLANES - 1))
            size = SUBLANES
            while size < EXPERT_ROWS:
                @pl.when((n & size) != 0)
                def _(off=off, size=size):
                    dst = xg_ref.at[pl.ds(pl.multiple_of(off, SUBLANES), size)]
                    pltpu.make_async_copy(zero_ref.at[pl.ds(0, size)], dst, zsem).start()
                off = off + (n & size)
                size *= 2
            return off

        used = lax.fori_loop(0, N_EXPERTS, pad_expert, 0)

        def pad_block(b, _):
            first = pl.multiple_of(used + b * EXPERT_ROWS, EXPERT_ROWS)
            pltpu.make_async_copy(zero_ref, xg_ref.at[pl.ds(first, EXPERT_ROWS)], zsem).start()
            return 0

        lax.fori_loop(0, (xg_ref.shape[0] - used) // EXPERT_ROWS, pad_block, 0)
        pltpu.make_async_copy(xg_ref.at[pl.ds(0, n_pad)], xg_ref.at[pl.ds(0, n_pad)], zsem).wait()

    def issue(t, _):
        for k in range(TOP_K):
            a = t * TOP_K + k
            dest = start_ref[idx_ref[a]] + cnt_ref[a]
            _row_copy(h_ref, t, xg_ref, dest, sem).start()
        return 0

    lax.fori_loop(0, tm, issue, 0)
    pltpu.make_async_copy(xg_ref.at[pl.ds(0, tm * TOP_K)], xg_ref.at[pl.ds(0, tm * TOP_K)], sem).wait()


def _dispatch_call(idx_flat, cnt_flat, start, counts, h, n_rows, tm):
    t_tok, d = h.shape
    smem_blk = pl.BlockSpec((tm * TOP_K,), lambda i: (i,), memory_space=pltpu.SMEM)
    smem = pl.BlockSpec(memory_space=pltpu.SMEM)
    return pl.pallas_call(
        functools.partial(_dispatch_kernel, tm=tm, n_pad=n_rows - t_tok * TOP_K),
        out_shape=jax.ShapeDtypeStruct((n_rows, d), F32),
        grid=(t_tok // tm,),
        in_specs=[smem_blk, smem_blk, smem, smem, pl.BlockSpec((tm, d), lambda i: (i, 0))],
        out_specs=pl.BlockSpec(memory_space=pl.ANY),
        scratch_shapes=[pltpu.VMEM((EXPERT_ROWS, d), F32), pltpu.SemaphoreType.DMA(()),
                        pltpu.SemaphoreType.DMA(())],
        compiler_params=pltpu.CompilerParams(dimension_semantics=("arbitrary",),
                                             vmem_limit_bytes=VMEM_LIMIT),
        name="moe_dispatch",
    )(idx_flat, cnt_flat, start, counts, h)


def _expert_kernel(be_ref, nu_ref, x_ref, wgu_ref, wd_ref, bg_ref, bu_ref, bd_ref, o_ref,
                   wgt_ref, wut_ref, wdb_ref, tmp_ref):
    i = pl.program_id(0)
    live = i < nu_ref[0]
    chunk = 2 * LANES

    @pl.when(live & ((i == 0) | (be_ref[i] != be_ref[jnp.maximum(i - 1, 0)])))
    def _():
        for c in range(wgu_ref.shape[1] // chunk):
            rows = slice(c * LANES, (c + 1) * LANES)
            for k in range(wgu_ref.shape[0] // LANES):
                cols = slice(k * LANES, (k + 1) * LANES)
                slot = (c * (wgu_ref.shape[0] // LANES) + k) % tmp_ref.shape[0]
                tmp_ref[slot] = wgu_ref[cols, c * chunk:(c + 1) * chunk].T
                wgt_ref[rows, cols] = tmp_ref[slot, pl.ds(0, LANES, stride=2), :].astype(BF16)
                wut_ref[rows, cols] = tmp_ref[slot, pl.ds(1, LANES, stride=2), :].astype(BF16)
        wdb_ref[...] = wd_ref[...].astype(BF16)

    @pl.when(live)
    def _():
        x = x_ref[...].astype(BF16)
        nt = (((1,), (1,)), ((), ()))
        gate = lax.dot_general(x, wgt_ref[...], nt, preferred_element_type=F32) + bg_ref[...]
        up = lax.dot_general(x, wut_ref[...], nt, preferred_element_type=F32) + bu_ref[...]
        gate = jnp.minimum(gate, SWIGLU_LIMIT)
        up = jnp.clip(up, -SWIGLU_LIMIT, SWIGLU_LIMIT)
        act = (up + 1.0) * gate * jax.nn.sigmoid(SWIGLU_ALPHA * gate)
        o_ref[...] = _dot(act.astype(BF16), wdb_ref[...]) + bd_ref[...]

    @pl.when(jnp.logical_not(live))
    def _():
        o_ref[...] = jnp.zeros_like(o_ref)


def _expert_call(block_expert, n_used, xg, w_gate_up, w_down, bg, bu, bd):
    n_rows, d = xg.shape
    f = w_down.shape[1]
    n_blocks = n_rows // EXPERT_ROWS
    blk = lambda i, be, nu: (jnp.minimum(i, nu[0] - 1), 0)
    wsel = lambda i, be, nu: (be[i], 0, 0)
    return pl.pallas_call(
        _expert_kernel,
        out_shape=jax.ShapeDtypeStruct((n_rows, d), F32),
        grid_spec=pltpu.PrefetchScalarGridSpec(
            num_scalar_prefetch=2,
            grid=(n_blocks,),
            in_specs=[pl.BlockSpec((EXPERT_ROWS, d), blk),
                      pl.BlockSpec((None, d, 2 * f), wsel),
                      pl.BlockSpec((None, f, d), wsel),
                      pl.BlockSpec((None, 1, f), wsel),
                      pl.BlockSpec((None, 1, f), wsel),
                      pl.BlockSpec((None, 1, d), wsel)],
            out_specs=pl.BlockSpec((EXPERT_ROWS, d), lambda i, be, nu: (i, 0)),
            scratch_shapes=[pltpu.VMEM((f, d), BF16), pltpu.VMEM((f, d), BF16), pltpu.VMEM((f, d), BF16),
                            pltpu.VMEM((4, 2 * LANES, LANES), F32)]),
        compiler_params=pltpu.CompilerParams(dimension_semantics=("arbitrary",),
                                             vmem_limit_bytes=VMEM_LIMIT),
        name="expert_ffn",
    )(block_expert, n_used, xg, w_gate_up, w_down, bg, bu, bd)


def _combine_kernel(idx_ref, cnt_ref, start_ref, h_ref, tw_ref, lng_ref, lnb_ref, rows_ref, o_ref,
                    buf_ref, sem, *, tm):
    def issue(t, _):
        for k in range(TOP_K):
            a = t * TOP_K + k
            src = start_ref[idx_ref[a]] + cnt_ref[a]
            _row_copy(rows_ref, src, buf_ref.at[k], t, sem).start()
        return 0

    lax.fori_loop(0, tm, issue, 0)
    pltpu.make_async_copy(buf_ref, buf_ref, sem).wait()
    tw = tw_ref[...]
    y = tw[:, 0:1] * buf_ref[0]
    for k in range(1, TOP_K):
        y = y + tw[:, k:k + 1] * buf_ref[k]
    o_ref[...] = _layer_norm(DEEPNORM_ALPHA * h_ref[...] + y, lng_ref[...], lnb_ref[...])


def _combine_call(idx_flat, cnt_flat, start, h, tw, lng, lnb, rows, tm):
    t_tok, d = h.shape
    smem_blk = pl.BlockSpec((tm * TOP_K,), lambda i: (i,), memory_space=pltpu.SMEM)
    return pl.pallas_call(
        functools.partial(_combine_kernel, tm=tm),
        out_shape=jax.ShapeDtypeStruct((t_tok, d), F32),
        grid=(t_tok // tm,),
        in_specs=[smem_blk, smem_blk,
                  pl.BlockSpec(memory_space=pltpu.SMEM),
                  pl.BlockSpec((tm, d), lambda i: (i, 0)),
                  pl.BlockSpec((tm, 8), lambda i: (i, 0)),
                  pl.BlockSpec((1, d), lambda i: (0, 0)),
                  pl.BlockSpec((1, d), lambda i: (0, 0)),
                  pl.BlockSpec(memory_space=pl.ANY)],
        out_specs=pl.BlockSpec((tm, d), lambda i: (i, 0)),
        scratch_shapes=[pltpu.VMEM((TOP_K, tm, d), F32), pltpu.SemaphoreType.DMA(())],
        compiler_params=pltpu.CompilerParams(dimension_semantics=("arbitrary",),
                                             vmem_limit_bytes=VMEM_LIMIT),
        name="moe_combine_ln",
    )(idx_flat, cnt_flat, start, h, tw, lng, lnb, rows)


def _layer(x2, pos, batch, seq, w_in, b_gate, w_branch_sb, w_branch_moba, w_out, ln_mix_g, ln_mix_b,
           w_router, b_router, w_gate_up, b_gate_up, w_down, b_down, ln_ffn_g, ln_ffn_b):
    t_tok, d = x2.shape
    nb = seq // KV_BLOCK
    qkv_w = 6 * WIDTH
    tm_a = 512 if t_tok % 512 == 0 else KV_BLOCK

    w_qkv = w_in[:, :qkv_w].astype(BF16)
    w_g = w_in[:, qkv_w:].astype(BF16)
    pos3 = pos.astype(F32).reshape(t_tok // tm_a, 1, tm_a)
    invf = (ROPE_THETA ** (-jnp.arange(ROPE_HALF, dtype=F32) / ROPE_HALF)).reshape(ROPE_HALF, 1)
    ki = jnp.arange(KV_BLOCK)
    tri = jnp.where(ki[None, :] > ki[:, None], -1.0, 0.0).astype(BF16)
    avg = jnp.where(jnp.arange(seq)[None, :] // KV_BLOCK == jnp.arange(nb)[:, None],
                    1.0 / KV_BLOCK, 0.0).astype(BF16)

    qt_sb, k_sb, vt_sb, qt_mb, k_mb, vt_mb = _qkv_call(x2, pos3, invf, w_qkv, tm_a)
    a_sb = _sb_call(qt_sb, k_sb, vt_sb, tri, batch, seq)
    a_mb = _moba_call(qt_mb, k_mb, vt_mb, avg, batch, seq)

    w_r = jnp.zeros((d, LANES), F32).at[:, :N_EXPERTS].set(w_router).astype(BF16)
    b_r = jnp.full((1, LANES), NEG_INF, F32).at[0, :N_EXPERTS].set(b_router)
    h, idx8, cnt8, tw8, counts = _mixer_call(
        x2, a_sb, a_mb, w_g, b_gate.reshape(1, -1), w_branch_sb.astype(BF16), w_branch_moba.astype(BF16),
        w_out.astype(BF16), ln_mix_g.reshape(1, -1), ln_mix_b.reshape(1, -1), w_r, b_r, tm_a)

    counts = counts[0, :N_EXPERTS]
    padded = (counts + EXPERT_ROWS - 1) // EXPERT_ROWS * EXPERT_ROWS
    padded_end = jnp.cumsum(padded)
    start = (padded_end - padded).astype(I32)
    n_assign = t_tok * TOP_K
    n_blocks = -(-(n_assign + N_EXPERTS * (EXPERT_ROWS - 1)) // EXPERT_ROWS)
    block_first = jnp.arange(n_blocks, dtype=I32) * EXPERT_ROWS
    block_expert = jnp.minimum(jnp.sum(block_first[:, None] >= padded_end[None, :], axis=1),
                               N_EXPERTS - 1).astype(I32)
    n_used = (padded_end[-1:] // EXPERT_ROWS).astype(I32)

    idx_flat = idx8[:, :TOP_K].reshape(-1)
    cnt_flat = cnt8[:, :TOP_K].reshape(-1)
    xg = _dispatch_call(idx_flat, cnt_flat, start, counts, h, n_blocks * EXPERT_ROWS, tm_a)

    f = w_down.shape[1]
    rows = _expert_call(block_expert, n_used, xg, w_gate_up, w_down,
                        b_gate_up[:, 0::2].reshape(N_EXPERTS, 1, f), b_gate_up[:, 1::2].reshape(N_EXPERTS, 1, f),
                        b_down.reshape(N_EXPERTS, 1, d))
    return _combine_call(idx_flat, cnt_flat, start, h, tw8, ln_ffn_g.reshape(1, -1),
                         ln_ffn_b.reshape(1, -1), rows, KV_BLOCK)


def kernel(x, positions, w_in, b_gate, w_branch_sb, w_branch_moba, w_out, ln_mix_g, ln_mix_b, w_router,
           b_router, w_gate_up, b_gate_up, w_down, b_down, ln_ffn_g, ln_ffn_b):
    batch, seq, d = x.shape
    h = x.reshape(batch * seq, d)
    pos = positions.reshape(batch * seq)
    for layer in range(w_in.shape[0]):
        h = _layer(h, pos, batch, seq, w_in[layer], b_gate[layer], w_branch_sb[layer], w_branch_moba[layer],
                   w_out[layer], ln_mix_g[layer], ln_mix_b[layer], w_router[layer], b_router[layer],
                   w_gate_up[layer], b_gate_up[layer], w_down[layer], b_down[layer],
                   ln_ffn_g[layer], ln_ffn_b[layer])
    return h.reshape(batch, seq, d)
```

```python
import functools

import jax
import jax.numpy as jnp
from jax import lax
from jax.experimental import pallas as pl
from jax.experimental.pallas import tpu as pltpu

F32 = jnp.float32
BF16 = jnp.bfloat16
I32 = jnp.int32

HEAD_DIM = 64
N_HEADS = 8
WIDTH = N_HEADS * HEAD_DIM
LANES = 128
SUBLANES = 8
N_PAIRS = WIDTH // LANES
KV_BLOCK = 256
MOBA_TOPK = 3
ROPE_THETA = 500000.0
ROPE_HALF = 8
N_EXPERTS = 32
TOP_K = 4
SWIGLU_LIMIT = 7.0
SWIGLU_ALPHA = 1.702
EXPERT_ROWS = 256
LN_EPS = 1e-5
DEPTH = 1
DEEPNORM_ALPHA = (2 * DEPTH) ** 0.25
QK_SCALE = HEAD_DIM ** -0.5
NEG_INF = float("-inf")
VMEM_LIMIT = 56 * 1024 * 1024


def _dot(a, b):
    return jnp.dot(a, b, preferred_element_type=F32)


def _qkv_kernel(x_ref, pos_ref, invf_ref, w_ref,
                qt_sb_ref, k_sb_ref, vt_sb_ref, qt_mb_ref, k_mb_ref, vt_mb_ref, *, tm):
    xb = x_ref[...].astype(BF16)
    ang = invf_ref[...] * pos_ref[0]
    cos, sin = jnp.cos(ang), jnp.sin(ang)

    def rope_t(t):
        parts = []
        for base in (0, HEAD_DIM):
            x1 = t[base:base + ROPE_HALF]
            x2 = t[base + ROPE_HALF:base + 2 * ROPE_HALF]
            parts += [x1 * cos - x2 * sin, x2 * cos + x1 * sin, t[base + 2 * ROPE_HALF:base + HEAD_DIM]]
        return jnp.concatenate(parts, axis=0)

    def store_t(ref, p, t):
        tb = t.astype(BF16)
        for blk in range(tm // KV_BLOCK):
            ref[blk, p] = tb[:, blk * KV_BLOCK:(blk + 1) * KV_BLOCK]

    for sec in range(6):
        for half in range(2):
            c0 = sec * WIDTH + half * 2 * LANES
            r = _dot(xb, w_ref[:, c0:c0 + 2 * LANES])
            for q in range(2):
                p = half * 2 + q
                t = r[:, q * LANES:(q + 1) * LANES]
                if sec == 0:
                    store_t(qt_sb_ref, p, (t * QK_SCALE).T)
                elif sec == 1:
                    k_sb_ref[:, p * LANES:(p + 1) * LANES] = t.astype(BF16)
                elif sec == 2:
                    store_t(vt_sb_ref, p, t.T)
                elif sec == 3:
                    store_t(qt_mb_ref, p, rope_t(t.T) * QK_SCALE)
                elif sec == 4:
                    k_mb_ref[:, p * LANES:(p + 1) * LANES] = rope_t(t.T).T.astype(BF16)
                else:
                    store_t(vt_mb_ref, p, t.T)


def _qkv_call(x2, pos3, invf, w_qkv, tm):
    t_tok, d = x2.shape
    nblk = t_tok // KV_BLOCK
    bpt = tm // KV_BLOCK
    t_shape = jax.ShapeDtypeStruct((nblk, N_PAIRS, LANES, KV_BLOCK), BF16)
    n_shape = jax.ShapeDtypeStruct((t_tok, WIDTH), BF16)
    t_spec = pl.BlockSpec((bpt, N_PAIRS, LANES, KV_BLOCK), lambda i: (i, 0, 0, 0))
    n_spec = pl.BlockSpec((tm, WIDTH), lambda i: (i, 0))
    return pl.pallas_call(
        functools.partial(_qkv_kernel, tm=tm),
        out_shape=(t_shape, n_shape, t_shape, t_shape, n_shape, t_shape),
        grid=(t_tok // tm,),
        in_specs=[pl.BlockSpec((tm, d), lambda i: (i, 0)),
                  pl.BlockSpec((1, 1, tm), lambda i: (i, 0, 0)),
                  pl.BlockSpec((ROPE_HALF, 1), lambda i: (0, 0)),
                  pl.BlockSpec((d, 6 * WIDTH), lambda i: (0, 0))],
        out_specs=(t_spec, n_spec, t_spec, t_spec, n_spec, t_spec),
        compiler_params=pltpu.CompilerParams(dimension_semantics=("arbitrary",),
                                             vmem_limit_bytes=VMEM_LIMIT),
        name="qkv_proj",
    )(x2, pos3, invf, w_qkv)


def _head_rows(h):
    row = lax.broadcasted_iota(I32, (LANES, KV_BLOCK), 0)
    return (row >= HEAD_DIM * h) & (row < HEAD_DIM * (h + 1))


def _neg_abs(x):
    return pltpu.bitcast(pltpu.bitcast(x, jnp.uint32) | jnp.uint32(0x80000000), F32)


def _tile_specs(seq):
    nb = seq // KV_BLOCK
    tiles_spec = pl.BlockSpec((nb, None, LANES, KV_BLOCK), lambda b, p, i: (b, p, 0, 0))
    col_spec = pl.BlockSpec((seq, LANES), lambda b, p, i: (b, p))
    return tiles_spec, col_spec


def _chain_list(i, nb, past_block):
    tiles = (i, nb - 1 - i)
    chains = [dict(j=tiles[a], a=a, h=h, diagonal=True, first=None) for a in range(2) for h in range(2)]
    for t in range(nb - 1):
        first = t < i
        for h in range(2):
            chains.append(dict(j=past_block(first, t), a=jnp.where(first, 0, 1), h=h, diagonal=False, first=first))
    return tiles, chains


def _skewed(chains, stages):
    n = len(chains)
    for slot in range(n + max(lag for _, lag in stages)):
        for stage, lag in stages:
            if 0 <= slot - lag < n:
                stage(chains[slot - lag])


def _sb_kernel(qt_ref, k_ref, vt_ref, tri_ref, o_ref, q_scr, acc_scr):
    i = pl.program_id(2)
    nb = qt_ref.shape[0]
    tiles, chains = _chain_list(i, nb, lambda first, t: jnp.where(first, i - 1 - t, nb - 2 - t))
    for a in range(2):
        qt = qt_ref[tiles[a]]
        for h in range(2):
            q_scr[a, h] = jnp.where(_head_rows(h), qt, jnp.zeros_like(qt))
    acc_scr[...] = jnp.zeros_like(acc_scr)
    key = lax.broadcasted_iota(I32, (KV_BLOCK, KV_BLOCK), 0)
    qry = lax.broadcasted_iota(I32, (KV_BLOCK, KV_BLOCK), 1)
    past = key < qry
    tri = tri_ref[...]

    zero = jnp.zeros((1, KV_BLOCK), F32)
    carries = [[zero, zero], [zero, zero]]

    def scores(c):
        kb = k_ref[pl.ds(pl.multiple_of(c["j"] * KV_BLOCK, KV_BLOCK), KV_BLOCK), :]
        c["z"] = _dot(kb, q_scr[c["a"], c["h"]])

    def softplus(c):
        z = c.pop("z")
        sp = jnp.maximum(z, 0.0) + jnp.log(1.0 + jnp.exp(_neg_abs(z)))
        spm = jnp.where(past, sp, 0.0) if c["diagonal"] else sp
        c["log_beta"] = z - sp
        c["spm"] = spm.astype(BF16)
        h, first = c["h"], c["first"]
        if first is None:
            c["carry"] = zero
            carries[c["a"]][h] = -jnp.sum(spm, axis=0, keepdims=True)
        else:
            c["carry"] = jnp.where(first, carries[0][h], carries[1][h])
            cout = c["carry"] - jnp.sum(spm, axis=0, keepdims=True)
            carries[0][h] = jnp.where(first, cout, carries[0][h])
            carries[1][h] = jnp.where(first, carries[1][h], cout)

    def suffix(c):
        c["after"] = _dot(tri, c.pop("spm"))

    def weights(c):
        w = jnp.exp(c.pop("log_beta") + c.pop("after"))
        if c["diagonal"]:
            w = jnp.where(past, w, 0.0)
        c["w"] = w.astype(BF16)

    def values(c):
        rows = slice(c["h"] * HEAD_DIM, (c["h"] + 1) * HEAD_DIM)
        pv = _dot(vt_ref[c["j"], rows, :], c.pop("w")) * jnp.exp(c.pop("carry"))
        acc_scr[c["a"], rows, :] += pv

    _skewed(chains, ((scores, 0), (suffix, 2), (values, 4), (softplus, 1), (weights, 3)))
    for a in range(2):
        o_ref[pl.ds(pl.multiple_of(tiles[a] * KV_BLOCK, KV_BLOCK), KV_BLOCK), :] = acc_scr[a].T.astype(BF16)


def _sb_call(qt, k, vt, tri, batch, seq):
    nb = seq // KV_BLOCK
    tiles_spec, col_spec = _tile_specs(seq)
    return pl.pallas_call(
        _sb_kernel,
        out_shape=jax.ShapeDtypeStruct((batch * seq, WIDTH), BF16),
        grid=(batch, N_PAIRS, nb // 2),
        in_specs=[tiles_spec, col_spec, tiles_spec,
                  pl.BlockSpec((KV_BLOCK, KV_BLOCK), lambda b, p, i: (0, 0))],
        out_specs=col_spec,
        scratch_shapes=[pltpu.VMEM((2, 2, LANES, KV_BLOCK), BF16), pltpu.VMEM((2, LANES, KV_BLOCK), F32)],
        compiler_params=pltpu.CompilerParams(
            dimension_semantics=("arbitrary", "arbitrary", "arbitrary"), vmem_limit_bytes=VMEM_LIMIT),
        name="stickbreak_attn",
    )(qt, k, vt, tri)


def _moba_kernel(qt_ref, k_ref, vt_ref, avg_ref, o_ref, km_ref, bias_scr, q_scr, acc_scr, s_scr):
    i = pl.program_id(2)
    nb = qt_ref.shape[0]

    @pl.when(i == 0)
    def _():
        km_ref[...] = _dot(avg_ref[...], k_ref[...])

    tiles, chains = _chain_list(i, nb, lambda first, t: jnp.where(first, t, t - i))
    key = lax.broadcasted_iota(I32, (KV_BLOCK, KV_BLOCK), 0)
    qry = lax.broadcasted_iota(I32, (KV_BLOCK, KV_BLOCK), 1)
    causal = key <= qry
    blk = lax.broadcasted_iota(I32, (nb, KV_BLOCK), 0)
    km = km_ref[...].astype(BF16)
    for a in range(2):
        qt = qt_ref[tiles[a]]
        valid = blk < tiles[a]
        for h in range(2):
            qth = jnp.where(_head_rows(h), qt, jnp.zeros_like(qt))
            q_scr[a, h] = qth
            g = jnp.where(valid, _dot(km, qth), NEG_INF)
            rank = jnp.zeros((nb, KV_BLOCK), F32)
            for jp in range(nb):
                gj = g[jp:jp + 1, :]
                better = jnp.where(gj > g, 1.0, jnp.where(gj == g, jnp.where(blk > jp, 1.0, 0.0), 0.0))
                rank = rank + better
            bias_scr[a, h] = jnp.where(valid, jnp.where(rank < MOBA_TOPK, 0.0, NEG_INF), NEG_INF)

    for n, c in enumerate(chains):
        c["n"] = n
    low = jnp.full((SUBLANES, KV_BLOCK), NEG_INF, F32)
    maxima = [[low, low], [low, low]]
    sums = [[0.0, 0.0], [0.0, 0.0]]
    acc_scr[...] = jnp.zeros_like(acc_scr)

    def scores(c):
        kb = k_ref[pl.ds(pl.multiple_of(c["j"] * KV_BLOCK, KV_BLOCK), KV_BLOCK), :]
        c["s"] = _dot(kb, q_scr[c["a"], c["h"]])

    def mask(c):
        h, first = c["h"], c["first"]
        if first is None:
            s = jnp.where(causal, c.pop("s"), NEG_INF)
        else:
            s = c.pop("s") + bias_scr[c["a"], h, pl.ds(c["j"], 1), :]
        s_scr[c["n"]] = s
        top = jnp.max(s.reshape(KV_BLOCK // SUBLANES, SUBLANES, KV_BLOCK), axis=0)
        if first is None:
            maxima[c["a"]][h] = top
        else:
            maxima[0][h] = jnp.where(first, jnp.maximum(maxima[0][h], top), maxima[0][h])
            maxima[1][h] = jnp.where(first, maxima[1][h], jnp.maximum(maxima[1][h], top))

    _skewed(chains, ((scores, 0), (mask, 1)))
    maxima = [[jnp.max(maxima[a][h], axis=0, keepdims=True) for h in range(2)] for a in range(2)]

    def weights(c):
        h, first = c["h"], c["first"]
        m = maxima[c["a"]][h] if first is None else jnp.where(first, maxima[0][h], maxima[1][h])
        p = jnp.exp(s_scr[c["n"]] - m)
        l = jnp.sum(p, axis=0, keepdims=True)
        if first is None:
            sums[c["a"]][h] = sums[c["a"]][h] + l
        else:
            sums[0][h] = sums[0][h] + jnp.where(first, l, 0.0)
            sums[1][h] = sums[1][h] + jnp.where(first, 0.0, l)
        c["p"] = p.astype(BF16)

    def values(c):
        rows = slice(c["h"] * HEAD_DIM, (c["h"] + 1) * HEAD_DIM)
        acc_scr[c["a"], rows, :] += _dot(vt_ref[c["j"], rows, :], c.pop("p"))

    _skewed(chains, ((values, 1), (weights, 0)))
    for a in range(2):
        for h in range(2):
            rows = slice(h * HEAD_DIM, (h + 1) * HEAD_DIM)
            acc_scr[a, rows, :] = acc_scr[a, rows, :] / sums[a][h]
        o_ref[pl.ds(pl.multiple_of(tiles[a] * KV_BLOCK, KV_BLOCK), KV_BLOCK), :] = acc_scr[a].T.astype(BF16)


def _moba_call(qt, k, vt, avg, batch, seq):
    nb = seq // KV_BLOCK
    tiles_spec, col_spec = _tile_specs(seq)
    return pl.pallas_call(
        _moba_kernel,
        out_shape=jax.ShapeDtypeStruct((batch * seq, WIDTH), BF16),
        grid=(batch, N_PAIRS, nb // 2),
        in_specs=[tiles_spec, col_spec, tiles_spec, pl.BlockSpec((nb, seq), lambda b, p, i: (0, 0))],
        out_specs=col_spec,
        scratch_shapes=[pltpu.VMEM((nb, LANES), F32), pltpu.VMEM((2, 2, nb, KV_BLOCK), F32),
                        pltpu.VMEM((2, 2, LANES, KV_BLOCK), BF16), pltpu.VMEM((2, LANES, KV_BLOCK), F32),
                        pltpu.VMEM((2 * (nb + 1), KV_BLOCK, KV_BLOCK), F32)],
        compiler_params=pltpu.CompilerParams(
            dimension_semantics=("arbitrary", "arbitrary", "arbitrary"), vmem_limit_bytes=VMEM_LIMIT),
        name="moba_attn",
    )(qt, k, vt, avg)


def _layer_norm(r, g, b):
    mu = jnp.mean(r, axis=-1, keepdims=True)
    d = r - mu
    var = jnp.mean(d * d, axis=-1, keepdims=True)
    return d * lax.rsqrt(var + LN_EPS) * g + b


def _mixer_kernel(x_ref, asb_ref, amb_ref, wg_ref, bg_ref, wbs_ref, wbm_ref, wo_ref, lng_ref, lnb_ref,
                  wr_ref, br_ref, h_ref, idx_ref, cnt_ref, tw_ref, counts_ref, carry_ref, *, tm, d):
    @pl.when(pl.program_id(0) == 0)
    def _():
        carry_ref[...] = jnp.zeros_like(carry_ref)

    x = x_ref[...]
    g = jax.nn.sigmoid(_dot(x.astype(BF16), wg_ref[...]) + bg_ref[...])
    mixed = g[:, :d] * _dot(asb_ref[...], wbs_ref[...]) + g[:, d:] * _dot(amb_ref[...], wbm_ref[...])
    r = DEEPNORM_ALPHA * x + _dot(mixed.astype(BF16), wo_ref[...])
    h = _layer_norm(r, lng_ref[...], lnb_ref[...])
    h_ref[...] = h

    logits = _dot(h.astype(BF16), wr_ref[...]) + br_ref[...]
    lane = lax.broadcasted_iota(I32, (tm, LANES), 1)
    vals, idxs = [], []
    member = jnp.zeros((tm, LANES), F32)
    for _ in range(TOP_K):
        mx = jnp.max(logits, axis=-1, keepdims=True)
        ik = jnp.min(jnp.where(logits == mx, lane, LANES), axis=-1, keepdims=True)
        hit = lane == ik
        vals.append(mx)
        idxs.append(ik)
        member = jnp.where(hit, 1.0, member)
        logits = jnp.where(hit, NEG_INF, logits)
    es = [jnp.exp(v - vals[0]) for v in vals]
    den = es[0] + es[1] + es[2] + es[3]

    rt = lax.broadcasted_iota(I32, (tm, tm), 0)
    ct = lax.broadcasted_iota(I32, (tm, tm), 1)
    lower = jnp.where(ct < rt, 1.0, 0.0).astype(BF16)
    cnt = _dot(lower, member.astype(BF16)) + carry_ref[...]
    carry_ref[...] = carry_ref[...] + jnp.sum(member, axis=0, keepdims=True)
    counts_ref[...] = carry_ref[...].astype(I32)

    lane8 = lax.broadcasted_iota(I32, (tm, 8), 1)
    idx_out = jnp.zeros((tm, 8), I32)
    cnt_out = jnp.zeros((tm, 8), I32)
    tw_out = jnp.zeros((tm, 8), F32)
    for k in range(TOP_K):
        ck = jnp.sum(jnp.where(lane == idxs[k], cnt, 0.0), axis=-1, keepdims=True)
        idx_out = jnp.where(lane8 == k, idxs[k], idx_out)
        cnt_out = jnp.where(lane8 == k, ck.astype(I32), cnt_out)
        tw_out = jnp.where(lane8 == k, es[k] / den, tw_out)
    idx_ref[...] = idx_out
    cnt_ref[...] = cnt_out
    tw_ref[...] = tw_out


def _mixer_call(x2, a_sb, a_mb, wg, bg, wbs, wbm, wo, lng, lnb, wr, br, tm):
    t_tok, d = x2.shape
    const = lambda shape: pl.BlockSpec(shape, lambda i: (0,) * len(shape))
    row8 = pl.BlockSpec((tm, 8), lambda i: (i, 0))
    return pl.pallas_call(
        functools.partial(_mixer_kernel, tm=tm, d=d),
        out_shape=(jax.ShapeDtypeStruct((t_tok, d), F32),
                   jax.ShapeDtypeStruct((t_tok, 8), I32),
                   jax.ShapeDtypeStruct((t_tok, 8), I32),
                   jax.ShapeDtypeStruct((t_tok, 8), F32),
                   jax.ShapeDtypeStruct((1, LANES), I32)),
        grid=(t_tok // tm,),
        in_specs=[pl.BlockSpec((tm, d), lambda i: (i, 0)),
                  pl.BlockSpec((tm, WIDTH), lambda i: (i, 0)),
                  pl.BlockSpec((tm, WIDTH), lambda i: (i, 0)),
                  const((d, 2 * d)), const((1, 2 * d)),
                  const((WIDTH, d)), const((WIDTH, d)), const((d, d)),
                  const((1, d)), const((1, d)),
                  const((d, LANES)), const((1, LANES))],
        out_specs=(pl.BlockSpec((tm, d), lambda i: (i, 0)), row8, row8, row8, const((1, LANES))),
        scratch_shapes=[pltpu.VMEM((1, LANES), F32)],
        compiler_params=pltpu.CompilerParams(dimension_semantics=("arbitrary",),
                                             vmem_limit_bytes=VMEM_LIMIT),
        name="mixer_ln_router",
    )(x2, a_sb, a_mb, wg, bg, wbs, wbm, wo, lng, lnb, wr, br)


def _rows_kernel(idx_ref, cnt_ref, start_ref, dest_ref):
    idx, cnt = idx_ref[...], cnt_ref[...]
    tm = idx.shape[0]
    lane = lax.broadcasted_iota(I32, (tm, LANES), 1)
    lane8 = lax.broadcasted_iota(I32, (tm, 8), 1)
    start = start_ref[...].astype(F32)
    dest = jnp.zeros((tm, 8), I32)
    for k in range(TOP_K):
        base = jnp.sum(jnp.where(lane == idx[:, k:k + 1], start, 0.0), axis=-1, keepdims=True)
        dest = jnp.where(lane8 == k, base.astype(I32) + cnt[:, k:k + 1], dest)
    dest_ref[...] = dest


def _rows_call(idx8, cnt8, start_row, tm):
    t_tok = idx8.shape[0]
    row8 = pl.BlockSpec((tm, 8), lambda i: (i, 0))
    return pl.pallas_call(
        _rows_kernel,
        out_shape=jax.ShapeDtypeStruct((t_tok, 8), I32),
        grid=(t_tok // tm,),
        in_specs=[row8, row8, pl.BlockSpec((1, LANES), lambda i: (0, 0))],
        out_specs=row8,
        compiler_params=pltpu.CompilerParams(dimension_semantics=("arbitrary",)),
        name="moe_rows",
    )(idx8, cnt8, start_row)


def _row_copy(src_ref, s, dst_ref, t, sem):
    return pltpu.make_async_copy(src_ref.at[pl.ds(s, 1)], dst_ref.at[pl.ds(t, 1)], sem)


def _dispatch_kernel(dest_ref, start_ref, counts_ref, h_ref, xg_ref, zero_ref, sem, zsem, *, tm, n_pad):
    @pl.when(pl.program_id(0) == 0)
    def _():
        zero_ref[...] = jnp.zeros_like(zero_ref)

        def pad_expert(e, _):
            n = (EXPERT_ROWS - counts_ref[e] % EXPERT_ROWS) % EXPERT_ROWS
            off = start_ref[e] + counts_ref[e]
            for s in range(SUBLANES - 1):
                @pl.when(s < (n & (SUBLANES - 1)))
                def _(row=off + s):
                    _row_copy(zero_ref, 0, xg_ref, row, zsem).start()
            off = off + (n & (SUBLANES - 1))
            size = SUBLANES
            while size < EXPERT_ROWS:
                @pl.when((n & size) != 0)
                def _(off=off, size=size):
                    dst = xg_ref.at[pl.ds(pl.multiple_of(off, SUBLANES), size)]
                    pltpu.make_async_copy(zero_ref.at[pl.ds(0, size)], dst, zsem).start()
                off = off + (n & size)
                size *= 2
            return off

        used = lax.fori_loop(0, N_EXPERTS, pad_expert, 0)

        def pad_block(b, _):
            first = pl.multiple_of(used + b * EXPERT_ROWS, EXPERT_ROWS)
            pltpu.make_async_copy(zero_ref, xg_ref.at[pl.ds(first, EXPERT_ROWS)], zsem).start()
            return 0

        lax.fori_loop(0, (xg_ref.shape[0] - used) // EXPERT_ROWS, pad_block, 0)
        pltpu.make_async_copy(xg_ref.at[pl.ds(0, n_pad)], xg_ref.at[pl.ds(0, n_pad)], zsem).wait()

    def issue(t, _):
        for k in range(TOP_K):
            _row_copy(h_ref, t, xg_ref, dest_ref[t * TOP_K + k], sem).start()
        return 0

    lax.fori_loop(0, tm, issue, 0, unroll=2)
    pltpu.make_async_copy(xg_ref.at[pl.ds(0, tm * TOP_K)], xg_ref.at[pl.ds(0, tm * TOP_K)], sem).wait()


def _dispatch_call(dest_flat, start, counts, h, n_rows, tm):
    t_tok, d = h.shape
    smem_blk = pl.BlockSpec((tm * TOP_K,), lambda i: (i,), memory_space=pltpu.SMEM)
    smem = pl.BlockSpec(memory_space=pltpu.SMEM)
    return pl.pallas_call(
        functools.partial(_dispatch_kernel, tm=tm, n_pad=n_rows - t_tok * TOP_K),
        out_shape=jax.ShapeDtypeStruct((n_rows, d), F32),
        grid=(t_tok // tm,),
        in_specs=[smem_blk, smem, smem, pl.BlockSpec((tm, d), lambda i: (i, 0))],
        out_specs=pl.BlockSpec(memory_space=pl.ANY),
        scratch_shapes=[pltpu.VMEM((EXPERT_ROWS, d), F32), pltpu.SemaphoreType.DMA(()),
                        pltpu.SemaphoreType.DMA(())],
        compiler_params=pltpu.CompilerParams(dimension_semantics=("arbitrary",),
                                             vmem_limit_bytes=VMEM_LIMIT),
        name="moe_dispatch",
    )(dest_flat, start, counts, h)


def _expert_kernel(be_ref, nu_ref, x_ref, wgu_ref, wd_ref, bg_ref, bu_ref, bd_ref, o_ref,
                   wgt_ref, wut_ref, wdb_ref, tmp_ref):
    i = pl.program_id(0)
    live = i < nu_ref[0]
    chunk = 2 * LANES

    @pl.when(live & ((i == 0) | (be_ref[i] != be_ref[jnp.maximum(i - 1, 0)])))
    def _():
        for c in range(wgu_ref.shape[1] // chunk):
            rows = slice(c * LANES, (c + 1) * LANES)
            for k in range(wgu_ref.shape[0] // LANES):
                cols = slice(k * LANES, (k + 1) * LANES)
                slot = (c * (wgu_ref.shape[0] // LANES) + k) % tmp_ref.shape[0]
                tmp_ref[slot] = wgu_ref[cols, c * chunk:(c + 1) * chunk].T
                wgt_ref[rows, cols] = tmp_ref[slot, pl.ds(0, LANES, stride=2), :].astype(BF16)
                wut_ref[rows, cols] = tmp_ref[slot, pl.ds(1, LANES, stride=2), :].astype(BF16)
        wdb_ref[...] = wd_ref[...].astype(BF16)

    @pl.when(live)
    def _():
        x = x_ref[...].astype(BF16)
        nt = (((1,), (1,)), ((), ()))
        gate = lax.dot_general(x, wgt_ref[...], nt, preferred_element_type=F32) + bg_ref[...]
        up = lax.dot_general(x, wut_ref[...], nt, preferred_element_type=F32) + bu_ref[...]
        gate = jnp.minimum(gate, SWIGLU_LIMIT)
        up = jnp.clip(up, -SWIGLU_LIMIT, SWIGLU_LIMIT)
        act = (up + 1.0) * gate * jax.nn.sigmoid(SWIGLU_ALPHA * gate)
        o_ref[...] = _dot(act.astype(BF16), wdb_ref[...]) + bd_ref[...]

    @pl.when(jnp.logical_not(live))
    def _():
        o_ref[...] = jnp.zeros_like(o_ref)


def _expert_call(block_expert, n_used, xg, w_gate_up, w_down, bg, bu, bd):
    n_rows, d = xg.shape
    f = w_down.shape[1]
    n_blocks = n_rows // EXPERT_ROWS
    blk = lambda i, be, nu: (jnp.minimum(i, nu[0] - 1), 0)
    wsel = lambda i, be, nu: (be[i], 0, 0)
    return pl.pallas_call(
        _expert_kernel,
        out_shape=jax.ShapeDtypeStruct((n_rows, d), F32),
        grid_spec=pltpu.PrefetchScalarGridSpec(
            num_scalar_prefetch=2,
            grid=(n_blocks,),
            in_specs=[pl.BlockSpec((EXPERT_ROWS, d), blk),
                      pl.BlockSpec((None, d, 2 * f), wsel),
                      pl.BlockSpec((None, f, d), wsel),
                      pl.BlockSpec((None, 1, f), wsel),
                      pl.BlockSpec((None, 1, f), wsel),
                      pl.BlockSpec((None, 1, d), wsel)],
            out_specs=pl.BlockSpec((EXPERT_ROWS, d), lambda i, be, nu: (i, 0)),
            scratch_shapes=[pltpu.VMEM((f, d), BF16), pltpu.VMEM((f, d), BF16), pltpu.VMEM((f, d), BF16),
                            pltpu.VMEM((4, 2 * LANES, LANES), F32)]),
        compiler_params=pltpu.CompilerParams(dimension_semantics=("arbitrary",),
                                             vmem_limit_bytes=VMEM_LIMIT),
        name="expert_ffn",
    )(block_expert, n_used, xg, w_gate_up, w_down, bg, bu, bd)


def _combine_kernel(dest_ref, next_ref, h_ref, tw_ref, lng_ref, lnb_ref, rows_ref, o_ref, buf_ref, sems, *, tm):
    i = pl.program_id(0)
    slot = i % 2

    def gather(rows_of, into):
        def issue(t, _):
            for k in range(TOP_K):
                _row_copy(rows_ref, rows_of[t * TOP_K + k], buf_ref.at[into, k], t, sems.at[into]).start()
            return 0
        lax.fori_loop(0, tm, issue, 0, unroll=2)

    @pl.when(i == 0)
    def _():
        gather(dest_ref, 0)

    @pl.when(i + 1 < pl.num_programs(0))
    def _():
        gather(next_ref, 1 - slot)

    pltpu.make_async_copy(buf_ref.at[slot], buf_ref.at[slot], sems.at[slot]).wait()
    tw = tw_ref[...]
    y = tw[:, 0:1] * buf_ref[slot, 0]
    for k in range(1, TOP_K):
        y = y + tw[:, k:k + 1] * buf_ref[slot, k]
    o_ref[...] = _layer_norm(DEEPNORM_ALPHA * h_ref[...] + y, lng_ref[...], lnb_ref[...])


def _combine_call(dest_flat, h, tw, lng, lnb, rows, tm):
    t_tok, d = h.shape
    n_tiles = t_tok // tm
    return pl.pallas_call(
        functools.partial(_combine_kernel, tm=tm),
        out_shape=jax.ShapeDtypeStruct((t_tok, d), F32),
        grid=(n_tiles,),
        in_specs=[pl.BlockSpec((tm * TOP_K,), lambda i: (i,), memory_space=pltpu.SMEM),
                  pl.BlockSpec((tm * TOP_K,), lambda i: (jnp.minimum(i + 1, n_tiles - 1),),
                               memory_space=pltpu.SMEM),
                  pl.BlockSpec((tm, d), lambda i: (i, 0)),
                  pl.BlockSpec((tm, 8), lambda i: (i, 0)),
                  pl.BlockSpec((1, d), lambda i: (0, 0)),
                  pl.BlockSpec((1, d), lambda i: (0, 0)),
                  pl.BlockSpec(memory_space=pl.ANY)],
        out_specs=pl.BlockSpec((tm, d), lambda i: (i, 0)),
        scratch_shapes=[pltpu.VMEM((2, TOP_K, tm, d), F32), pltpu.SemaphoreType.DMA((2,))],
        compiler_params=pltpu.CompilerParams(dimension_semantics=("arbitrary",),
                                             vmem_limit_bytes=VMEM_LIMIT),
        name="moe_combine_ln",
    )(dest_flat, dest_flat, h, tw, lng, lnb, rows)


def _layer(x2, pos, batch, seq, w_in, b_gate, w_branch_sb, w_branch_moba, w_out, ln_mix_g, ln_mix_b,
           w_router, b_router, w_gate_up, b_gate_up, w_down, b_down, ln_ffn_g, ln_ffn_b):
    t_tok, d = x2.shape
    nb = seq // KV_BLOCK
    qkv_w = 6 * WIDTH
    tm_a = 512 if t_tok % 512 == 0 else KV_BLOCK

    w_qkv = w_in[:, :qkv_w].astype(BF16)
    w_g = w_in[:, qkv_w:].astype(BF16)
    pos3 = pos.astype(F32).reshape(t_tok // tm_a, 1, tm_a)
    invf = (ROPE_THETA ** (-jnp.arange(ROPE_HALF, dtype=F32) / ROPE_HALF)).reshape(ROPE_HALF, 1)
    ki = jnp.arange(KV_BLOCK)
    tri = jnp.where(ki[None, :] > ki[:, None], -1.0, 0.0).astype(BF16)
    avg = jnp.where(jnp.arange(seq)[None, :] // KV_BLOCK == jnp.arange(nb)[:, None],
                    1.0 / KV_BLOCK, 0.0).astype(BF16)

    qt_sb, k_sb, vt_sb, qt_mb, k_mb, vt_mb = _qkv_call(x2, pos3, invf, w_qkv, tm_a)
    a_sb = _sb_call(qt_sb, k_sb, vt_sb, tri, batch, seq)
    a_mb = _moba_call(qt_mb, k_mb, vt_mb, avg, batch, seq)

    w_r = jnp.zeros((d, LANES), F32).at[:, :N_EXPERTS].set(w_router).astype(BF16)
    b_r = jnp.full((1, LANES), NEG_INF, F32).at[0, :N_EXPERTS].set(b_router)
    h, idx8, cnt8, tw8, counts = _mixer_call(
        x2, a_sb, a_mb, w_g, b_gate.reshape(1, -1), w_branch_sb.astype(BF16), w_branch_moba.astype(BF16),
        w_out.astype(BF16), ln_mix_g.reshape(1, -1), ln_mix_b.reshape(1, -1), w_r, b_r, tm_a)

    counts = counts[0, :N_EXPERTS]
    padded = (counts + EXPERT_ROWS - 1) // EXPERT_ROWS * EXPERT_ROWS
    padded_end = jnp.cumsum(padded)
    start = (padded_end - padded).astype(I32)
    n_assign = t_tok * TOP_K
    n_blocks = -(-(n_assign + N_EXPERTS * (EXPERT_ROWS - 1)) // EXPERT_ROWS)
    block_first = jnp.arange(n_blocks, dtype=I32) * EXPERT_ROWS
    block_expert = jnp.minimum(jnp.sum(block_first[:, None] >= padded_end[None, :], axis=1),
                               N_EXPERTS - 1).astype(I32)
    n_used = (padded_end[-1:] // EXPERT_ROWS).astype(I32)

    start_row = jnp.zeros((1, LANES), I32).at[0, :N_EXPERTS].set(start)
    dest_flat = _rows_call(idx8, cnt8, start_row, tm_a)[:, :TOP_K].reshape(-1)
    xg = _dispatch_call(dest_flat, start, counts, h, n_blocks * EXPERT_ROWS, tm_a)

    f = w_down.shape[1]
    rows = _expert_call(block_expert, n_used, xg, w_gate_up, w_down,
                        b_gate_up[:, 0::2].reshape(N_EXPERTS, 1, f), b_gate_up[:, 1::2].reshape(N_EXPERTS, 1, f),
                        b_down.reshape(N_EXPERTS, 1, d))
    return _combine_call(dest_flat, h, tw8, ln_ffn_g.reshape(1, -1), ln_ffn_b.reshape(1, -1), rows, KV_BLOCK)


def kernel(x, positions, w_in, b_gate, w_branch_sb, w_branch_moba, w_out, ln_mix_g, ln_mix_b, w_router,
           b_router, w_gate_up, b_gate_up, w_down, b_down, ln_ffn_g, ln_ffn_b):
    batch, seq, d = x.shape
    h = x.reshape(batch * seq, d)
    pos = positions.reshape(batch * seq)
    for layer in range(w_in.shape[0]):
        h = _layer(h, pos, batch, seq, w_in[layer], b_gate[layer], w_branch_sb[layer], w_branch_moba[layer],
                   w_out[layer], ln_mix_g[layer], ln_mix_b[layer], w_router[layer], b_router[layer],
                   w_gate_up[layer], b_gate_up[layer], w_down[layer], b_down[layer],
                   ln_ffn_g[layer], ln_ffn_b[layer])
    return h.reshape(batch, seq, d)
```

```python
import functools

import jax
import jax.numpy as jnp
from jax import lax
from jax.experimental import pallas as pl
from jax.experimental.pallas import tpu as pltpu

F32 = jnp.float32
BF16 = jnp.bfloat16
I32 = jnp.int32

HEAD_DIM = 64
N_HEADS = 8
WIDTH = N_HEADS * HEAD_DIM
LANES = 128
SUBLANES = 8
N_PAIRS = WIDTH // LANES
KV_BLOCK = 256
MOBA_TOPK = 3
ROPE_THETA = 500000.0
ROPE_HALF = 8
N_EXPERTS = 32
TOP_K = 4
SWIGLU_LIMIT = 7.0
SWIGLU_ALPHA = 1.702
EXPERT_ROWS = 512
MXU_ROWS = 256
LN_EPS = 1e-5
DEPTH = 1
DEEPNORM_ALPHA = (2 * DEPTH) ** 0.25
QK_SCALE = HEAD_DIM ** -0.5
NEG_INF = float("-inf")
VMEM_LIMIT = 56 * 1024 * 1024


def _dot(a, b):
    return jnp.dot(a, b, preferred_element_type=F32)


def _qkv_kernel(x_ref, pos_ref, invf_ref, w_ref,
                qt_sb_ref, k_sb_ref, vt_sb_ref, qt_mb_ref, k_mb_ref, vt_mb_ref, *, tm):
    xb = x_ref[...].astype(BF16)
    ang = invf_ref[...] * pos_ref[0]
    cos, sin = jnp.cos(ang), jnp.sin(ang)

    def rope_t(t):
        parts = []
        for base in (0, HEAD_DIM):
            x1 = t[base:base + ROPE_HALF]
            x2 = t[base + ROPE_HALF:base + 2 * ROPE_HALF]
            parts += [x1 * cos - x2 * sin, x2 * cos + x1 * sin, t[base + 2 * ROPE_HALF:base + HEAD_DIM]]
        return jnp.concatenate(parts, axis=0)

    def store_t(ref, p, t):
        tb = t.astype(BF16)
        for blk in range(tm // KV_BLOCK):
            ref[blk, p] = tb[:, blk * KV_BLOCK:(blk + 1) * KV_BLOCK]

    for sec in range(6):
        for half in range(2):
            c0 = sec * WIDTH + half * 2 * LANES
            r = _dot(xb, w_ref[:, c0:c0 + 2 * LANES])
            for q in range(2):
                p = half * 2 + q
                t = r[:, q * LANES:(q + 1) * LANES]
                if sec == 0:
                    store_t(qt_sb_ref, p, (t * QK_SCALE).T)
                elif sec == 1:
                    k_sb_ref[:, p * LANES:(p + 1) * LANES] = t.astype(BF16)
                elif sec == 2:
                    store_t(vt_sb_ref, p, t.T)
                elif sec == 3:
                    store_t(qt_mb_ref, p, rope_t(t.T) * QK_SCALE)
                elif sec == 4:
                    k_mb_ref[:, p * LANES:(p + 1) * LANES] = rope_t(t.T).T.astype(BF16)
                else:
                    store_t(vt_mb_ref, p, t.T)


def _qkv_call(x2, pos3, invf, w_qkv, tm):
    t_tok, d = x2.shape
    nblk = t_tok // KV_BLOCK
    bpt = tm // KV_BLOCK
    t_shape = jax.ShapeDtypeStruct((nblk, N_PAIRS, LANES, KV_BLOCK), BF16)
    n_shape = jax.ShapeDtypeStruct((t_tok, WIDTH), BF16)
    t_spec = pl.BlockSpec((bpt, N_PAIRS, LANES, KV_BLOCK), lambda i: (i, 0, 0, 0))
    n_spec = pl.BlockSpec((tm, WIDTH), lambda i: (i, 0))
    return pl.pallas_call(
        functools.partial(_qkv_kernel, tm=tm),
        out_shape=(t_shape, n_shape, t_shape, t_shape, n_shape, t_shape),
        grid=(t_tok // tm,),
        in_specs=[pl.BlockSpec((tm, d), lambda i: (i, 0)),
                  pl.BlockSpec((1, 1, tm), lambda i: (i, 0, 0)),
                  pl.BlockSpec((ROPE_HALF, 1), lambda i: (0, 0)),
                  pl.BlockSpec((d, 6 * WIDTH), lambda i: (0, 0))],
        out_specs=(t_spec, n_spec, t_spec, t_spec, n_spec, t_spec),
        compiler_params=pltpu.CompilerParams(dimension_semantics=("arbitrary",),
                                             vmem_limit_bytes=VMEM_LIMIT),
        name="qkv_proj",
    )(x2, pos3, invf, w_qkv)


def _head_rows(h):
    row = lax.broadcasted_iota(I32, (LANES, KV_BLOCK), 0)
    return (row >= HEAD_DIM * h) & (row < HEAD_DIM * (h + 1))


def _tile_specs(seq):
    nb = seq // KV_BLOCK
    tiles_spec = pl.BlockSpec((nb, None, LANES, KV_BLOCK), lambda b, p, i: (b, p, 0, 0))
    col_spec = pl.BlockSpec((seq, LANES), lambda b, p, i: (b, p))
    return tiles_spec, col_spec


def _chain_list(i, nb, past_block):
    tiles = (i, nb - 1 - i)
    chains = [dict(j=tiles[a], a=a, h=h, diagonal=True, first=None) for a in range(2) for h in range(2)]
    for t in range(nb - 1):
        first = t < i
        for h in range(2):
            chains.append(dict(j=past_block(first, t), a=jnp.where(first, 0, 1), h=h, diagonal=False, first=first))
    return tiles, chains


def _skewed(chains, stages):
    n = len(chains)
    for slot in range(n + max(lag for _, lag in stages)):
        for stage, lag in stages:
            if 0 <= slot - lag < n:
                stage(chains[slot - lag])


def _sb_kernel(qt_ref, k_ref, vt_ref, tri_ref, o_ref, q_scr, acc_scr):
    i = pl.program_id(2)
    nb = qt_ref.shape[0]
    tiles, chains = _chain_list(i, nb, lambda first, t: jnp.where(first, i - 1 - t, nb - 2 - t))
    for a in range(2):
        qt = qt_ref[tiles[a]]
        for h in range(2):
            q_scr[a, h] = jnp.where(_head_rows(h), qt, jnp.zeros_like(qt))
    acc_scr[...] = jnp.zeros_like(acc_scr)
    key = lax.broadcasted_iota(I32, (KV_BLOCK, KV_BLOCK), 0)
    qry = lax.broadcasted_iota(I32, (KV_BLOCK, KV_BLOCK), 1)
    past = key < qry
    tri = tri_ref[...]

    zero = jnp.zeros((1, KV_BLOCK), F32)
    carries = [[zero, zero], [zero, zero]]

    def scores(c):
        kb = k_ref[pl.ds(pl.multiple_of(c["j"] * KV_BLOCK, KV_BLOCK), KV_BLOCK), :]
        c["z"] = _dot(kb, q_scr[c["a"], c["h"]])

    def softplus(c):
        z = c.pop("z")
        sp = jnp.maximum(z, 0.0) + jnp.log(1.0 + jnp.exp(-jnp.abs(z)))
        spm = jnp.where(past, sp, 0.0) if c["diagonal"] else sp
        c["log_beta"] = z - sp
        c["spm"] = spm.astype(BF16)
        h, first = c["h"], c["first"]
        if first is None:
            c["carry"] = zero
            carries[c["a"]][h] = -jnp.sum(spm, axis=0, keepdims=True)
        else:
            c["carry"] = jnp.where(first, carries[0][h], carries[1][h])
            cout = c["carry"] - jnp.sum(spm, axis=0, keepdims=True)
            carries[0][h] = jnp.where(first, cout, carries[0][h])
            carries[1][h] = jnp.where(first, carries[1][h], cout)

    def suffix(c):
        c["after"] = _dot(tri, c.pop("spm"))

    def weights(c):
        w = jnp.exp(c.pop("log_beta") + c.pop("after"))
        if c["diagonal"]:
            w = jnp.where(past, w, 0.0)
        c["w"] = w.astype(BF16)

    def values(c):
        rows = slice(c["h"] * HEAD_DIM, (c["h"] + 1) * HEAD_DIM)
        pv = _dot(vt_ref[c["j"], rows, :], c.pop("w")) * jnp.exp(c.pop("carry"))
        acc_scr[c["a"], rows, :] += pv

    _skewed(chains, ((scores, 0), (suffix, 2), (values, 4), (softplus, 1), (weights, 3)))
    for a in range(2):
        o_ref[pl.ds(pl.multiple_of(tiles[a] * KV_BLOCK, KV_BLOCK), KV_BLOCK), :] = acc_scr[a].T.astype(BF16)


def _sb_call(qt, k, vt, tri, batch, seq):
    nb = seq // KV_BLOCK
    tiles_spec, col_spec = _tile_specs(seq)
    return pl.pallas_call(
        _sb_kernel,
        out_shape=jax.ShapeDtypeStruct((batch * seq, WIDTH), BF16),
        grid=(batch, N_PAIRS, nb // 2),
        in_specs=[tiles_spec, col_spec, tiles_spec,
                  pl.BlockSpec((KV_BLOCK, KV_BLOCK), lambda b, p, i: (0, 0))],
        out_specs=col_spec,
        scratch_shapes=[pltpu.VMEM((2, 2, LANES, KV_BLOCK), BF16), pltpu.VMEM((2, LANES, KV_BLOCK), F32)],
        compiler_params=pltpu.CompilerParams(
            dimension_semantics=("arbitrary", "arbitrary", "arbitrary"), vmem_limit_bytes=VMEM_LIMIT),
        name="stickbreak_attn",
    )(qt, k, vt, tri)


def _moba_kernel(qt_ref, k_ref, vt_ref, avg_ref, o_ref, km_ref, bias_scr, q_scr, acc_scr, s_scr):
    i = pl.program_id(2)
    nb = qt_ref.shape[0]

    @pl.when(i == 0)
    def _():
        km_ref[...] = _dot(avg_ref[...], k_ref[...])

    tiles, chains = _chain_list(i, nb, lambda first, t: jnp.where(first, t, t - i))
    key = lax.broadcasted_iota(I32, (KV_BLOCK, KV_BLOCK), 0)
    qry = lax.broadcasted_iota(I32, (KV_BLOCK, KV_BLOCK), 1)
    causal = key <= qry
    blk = lax.broadcasted_iota(I32, (nb, KV_BLOCK), 0)
    km = km_ref[...].astype(BF16)
    for a in range(2):
        qt = qt_ref[tiles[a]]
        valid = blk < tiles[a]
        for h in range(2):
            qth = jnp.where(_head_rows(h), qt, jnp.zeros_like(qt))
            q_scr[a, h] = qth
            g = jnp.where(valid, _dot(km, qth), NEG_INF)
            rank = jnp.zeros((nb, KV_BLOCK), F32)
            for jp in range(nb):
                gj = g[jp:jp + 1, :]
                better = jnp.where(gj > g, 1.0, jnp.where(gj == g, jnp.where(blk > jp, 1.0, 0.0), 0.0))
                rank = rank + better
            bias_scr[a, h] = jnp.where(valid, jnp.where(rank < MOBA_TOPK, 0.0, NEG_INF), NEG_INF)

    for n, c in enumerate(chains):
        c["n"] = n
    low = jnp.full((SUBLANES, KV_BLOCK), NEG_INF, F32)
    maxima = [[low, low], [low, low]]
    sums = [[0.0, 0.0], [0.0, 0.0]]
    acc_scr[...] = jnp.zeros_like(acc_scr)

    def scores(c):
        kb = k_ref[pl.ds(pl.multiple_of(c["j"] * KV_BLOCK, KV_BLOCK), KV_BLOCK), :]
        c["s"] = _dot(kb, q_scr[c["a"], c["h"]])

    def mask(c):
        h, first = c["h"], c["first"]
        if first is None:
            s = jnp.where(causal, c.pop("s"), NEG_INF)
        else:
            s = c.pop("s") + bias_scr[c["a"], h, pl.ds(c["j"], 1), :]
        s_scr[c["n"]] = s
        top = jnp.max(s.reshape(KV_BLOCK // SUBLANES, SUBLANES, KV_BLOCK), axis=0)
        if first is None:
            maxima[c["a"]][h] = top
        else:
            maxima[0][h] = jnp.where(first, jnp.maximum(maxima[0][h], top), maxima[0][h])
            maxima[1][h] = jnp.where(first, maxima[1][h], jnp.maximum(maxima[1][h], top))

    _skewed(chains, ((scores, 0), (mask, 1)))
    maxima = [[jnp.max(maxima[a][h], axis=0, keepdims=True) for h in range(2)] for a in range(2)]

    def weights(c):
        h, first = c["h"], c["first"]
        m = maxima[c["a"]][h] if first is None else jnp.where(first, maxima[0][h], maxima[1][h])
        p = jnp.exp(s_scr[c["n"]] - m)
        l = jnp.sum(p, axis=0, keepdims=True)
        if first is None:
            sums[c["a"]][h] = sums[c["a"]][h] + l
        else:
            sums[0][h] = sums[0][h] + jnp.where(first, l, 0.0)
            sums[1][h] = sums[1][h] + jnp.where(first, 0.0, l)
        c["p"] = p.astype(BF16)

    def values(c):
        rows = slice(c["h"] * HEAD_DIM, (c["h"] + 1) * HEAD_DIM)
        acc_scr[c["a"], rows, :] += _dot(vt_ref[c["j"], rows, :], c.pop("p"))

    _skewed(chains, ((values, 1), (weights, 0)))
    for a in range(2):
        for h in range(2):
            rows = slice(h * HEAD_DIM, (h + 1) * HEAD_DIM)
            acc_scr[a, rows, :] = acc_scr[a, rows, :] / sums[a][h]
        o_ref[pl.ds(pl.multiple_of(tiles[a] * KV_BLOCK, KV_BLOCK), KV_BLOCK), :] = acc_scr[a].T.astype(BF16)


def _moba_call(qt, k, vt, avg, batch, seq):
    nb = seq // KV_BLOCK
    tiles_spec, col_spec = _tile_specs(seq)
    return pl.pallas_call(
        _moba_kernel,
        out_shape=jax.ShapeDtypeStruct((batch * seq, WIDTH), BF16),
        grid=(batch, N_PAIRS, nb // 2),
        in_specs=[tiles_spec, col_spec, tiles_spec, pl.BlockSpec((nb, seq), lambda b, p, i: (0, 0))],
        out_specs=col_spec,
        scratch_shapes=[pltpu.VMEM((nb, LANES), F32), pltpu.VMEM((2, 2, nb, KV_BLOCK), F32),
                        pltpu.VMEM((2, 2, LANES, KV_BLOCK), BF16), pltpu.VMEM((2, LANES, KV_BLOCK), F32),
                        pltpu.VMEM((2 * (nb + 1), KV_BLOCK, KV_BLOCK), F32)],
        compiler_params=pltpu.CompilerParams(
            dimension_semantics=("arbitrary", "arbitrary", "arbitrary"), vmem_limit_bytes=VMEM_LIMIT),
        name="moba_attn",
    )(qt, k, vt, avg)


def _layer_norm(r, g, b):
    mu = jnp.mean(r, axis=-1, keepdims=True)
    d = r - mu
    var = jnp.mean(d * d, axis=-1, keepdims=True)
    return d * lax.rsqrt(var + LN_EPS) * g + b


def _mixer_kernel(x_ref, asb_ref, amb_ref, wg_ref, bg_ref, wbs_ref, wbm_ref, wo_ref, lng_ref, lnb_ref,
                  wr_ref, br_ref, h_ref, idx_ref, cnt_ref, tw_ref, counts_ref, carry_ref, *, tm, d):
    @pl.when(pl.program_id(0) == 0)
    def _():
        carry_ref[...] = jnp.zeros_like(carry_ref)

    rows_per = MXU_ROWS
    chains = [dict(rows=slice(c * rows_per, (c + 1) * rows_per)) for c in range(tm // rows_per)]
    lane = lax.broadcasted_iota(I32, (rows_per, LANES), 1)
    lane8 = lax.broadcasted_iota(I32, (rows_per, 8), 1)
    rt = lax.broadcasted_iota(I32, (rows_per, rows_per), 0)
    ct = lax.broadcasted_iota(I32, (rows_per, rows_per), 1)
    lower = jnp.where(ct < rt, 1.0, 0.0).astype(BF16)

    def project(c):
        c["gp"] = _dot(x_ref[c["rows"], :].astype(BF16), wg_ref[...])
        c["ysb"] = _dot(asb_ref[c["rows"], :], wbs_ref[...])
        c["ymb"] = _dot(amb_ref[c["rows"], :], wbm_ref[...])

    def gate(c):
        g = jax.nn.sigmoid(c.pop("gp") + bg_ref[...])
        c["mixed"] = (g[:, :d] * c.pop("ysb") + g[:, d:] * c.pop("ymb")).astype(BF16)

    def out_proj(c):
        c["mix"] = _dot(c.pop("mixed"), wo_ref[...])

    def norm(c):
        h = _layer_norm(DEEPNORM_ALPHA * x_ref[c["rows"], :] + c.pop("mix"), lng_ref[...], lnb_ref[...])
        h_ref[c["rows"], :] = h
        c["hb"] = h.astype(BF16)

    def route(c):
        c["logits"] = _dot(c.pop("hb"), wr_ref[...])

    def top_k(c):
        logits = c.pop("logits") + br_ref[...]
        vals, idxs = [], []
        member = jnp.zeros((rows_per, LANES), F32)
        for _ in range(TOP_K):
            mx = jnp.max(logits, axis=-1, keepdims=True)
            ik = jnp.min(jnp.where(logits == mx, lane, LANES), axis=-1, keepdims=True)
            hit = lane == ik
            vals.append(mx)
            idxs.append(ik)
            member = jnp.where(hit, 1.0, member)
            logits = jnp.where(hit, NEG_INF, logits)
        es = [jnp.exp(v - vals[0]) for v in vals]
        den = es[0] + es[1] + es[2] + es[3]
        idx_out = jnp.zeros((rows_per, 8), I32)
        tw_out = jnp.zeros((rows_per, 8), F32)
        for k in range(TOP_K):
            idx_out = jnp.where(lane8 == k, idxs[k], idx_out)
            tw_out = jnp.where(lane8 == k, es[k] / den, tw_out)
        idx_ref[c["rows"], :] = idx_out
        tw_ref[c["rows"], :] = tw_out
        c["idxs"] = idxs
        c["member"] = member

    def count(c):
        member = c.pop("member")
        cnt = _dot(lower, member.astype(BF16)) + carry_ref[...]
        carry_ref[...] = carry_ref[...] + jnp.sum(member, axis=0, keepdims=True)
        cnt_out = jnp.zeros((rows_per, 8), I32)
        for k, ik in enumerate(c.pop("idxs")):
            ck = jnp.sum(jnp.where(lane == ik, cnt, 0.0), axis=-1, keepdims=True)
            cnt_out = jnp.where(lane8 == k, ck.astype(I32), cnt_out)
        cnt_ref[c["rows"], :] = cnt_out

    _skewed(chains, ((project, 0), (out_proj, 2), (route, 4), (count, 6), (gate, 1), (norm, 3), (top_k, 5)))
    counts_ref[...] = carry_ref[...].astype(I32)


def _mixer_call(x2, a_sb, a_mb, wg, bg, wbs, wbm, wo, lng, lnb, wr, br, tm):
    t_tok, d = x2.shape
    const = lambda shape: pl.BlockSpec(shape, lambda i: (0,) * len(shape))
    row8 = pl.BlockSpec((tm, 8), lambda i: (i, 0))
    return pl.pallas_call(
        functools.partial(_mixer_kernel, tm=tm, d=d),
        out_shape=(jax.ShapeDtypeStruct((t_tok, d), F32),
                   jax.ShapeDtypeStruct((t_tok, 8), I32),
                   jax.ShapeDtypeStruct((t_tok, 8), I32),
                   jax.ShapeDtypeStruct((t_tok, 8), F32),
                   jax.ShapeDtypeStruct((1, LANES), I32)),
        grid=(t_tok // tm,),
        in_specs=[pl.BlockSpec((tm, d), lambda i: (i, 0)),
                  pl.BlockSpec((tm, WIDTH), lambda i: (i, 0)),
                  pl.BlockSpec((tm, WIDTH), lambda i: (i, 0)),
                  const((d, 2 * d)), const((1, 2 * d)),
                  const((WIDTH, d)), const((WIDTH, d)), const((d, d)),
                  const((1, d)), const((1, d)),
                  const((d, LANES)), const((1, LANES))],
        out_specs=(pl.BlockSpec((tm, d), lambda i: (i, 0)), row8, row8, row8, const((1, LANES))),
        scratch_shapes=[pltpu.VMEM((1, LANES), F32)],
        compiler_params=pltpu.CompilerParams(dimension_semantics=("arbitrary",),
                                             vmem_limit_bytes=VMEM_LIMIT),
        name="mixer_ln_router",
    )(x2, a_sb, a_mb, wg, bg, wbs, wbm, wo, lng, lnb, wr, br)


def _rows_kernel(idx_ref, cnt_ref, start_ref, dest_ref):
    idx, cnt = idx_ref[...], cnt_ref[...]
    tm = idx.shape[0]
    lane = lax.broadcasted_iota(I32, (tm, LANES), 1)
    lane8 = lax.broadcasted_iota(I32, (tm, 8), 1)
    start = start_ref[...].astype(F32)
    dest = jnp.zeros((tm, 8), I32)
    for k in range(TOP_K):
        base = jnp.sum(jnp.where(lane == idx[:, k:k + 1], start, 0.0), axis=-1, keepdims=True)
        dest = jnp.where(lane8 == k, base.astype(I32) + cnt[:, k:k + 1], dest)
    dest_ref[...] = dest


def _rows_call(idx8, cnt8, start_row, tm):
    t_tok = idx8.shape[0]
    row8 = pl.BlockSpec((tm, 8), lambda i: (i, 0))
    return pl.pallas_call(
        _rows_kernel,
        out_shape=jax.ShapeDtypeStruct((t_tok, 8), I32),
        grid=(t_tok // tm,),
        in_specs=[row8, row8, pl.BlockSpec((1, LANES), lambda i: (0, 0))],
        out_specs=row8,
        compiler_params=pltpu.CompilerParams(dimension_semantics=("arbitrary",)),
        name="moe_rows",
    )(idx8, cnt8, start_row)


def _row_copy(src_ref, s, dst_ref, t, sem):
    return pltpu.make_async_copy(src_ref.at[pl.ds(s, 1)], dst_ref.at[pl.ds(t, 1)], sem)


def _dispatch_kernel(dest_ref, start_ref, counts_ref, h_ref, xg_ref, zero_ref, sem, zsem, *, tm, n_pad):
    @pl.when(pl.program_id(0) == 0)
    def _():
        zero_ref[...] = jnp.zeros_like(zero_ref)

        def pad_expert(e, _):
            n = (EXPERT_ROWS - counts_ref[e] % EXPERT_ROWS) % EXPERT_ROWS
            off = start_ref[e] + counts_ref[e]
            for s in range(SUBLANES - 1):
                @pl.when(s < (n & (SUBLANES - 1)))
                def _(row=off + s):
                    _row_copy(zero_ref, 0, xg_ref, row, zsem).start()
            off = off + (n & (SUBLANES - 1))
            size = SUBLANES
            while size < EXPERT_ROWS:
                @pl.when((n & size) != 0)
                def _(off=off, size=size):
                    dst = xg_ref.at[pl.ds(pl.multiple_of(off, SUBLANES), size)]
                    pltpu.make_async_copy(zero_ref.at[pl.ds(0, size)], dst, zsem).start()
                off = off + (n & size)
                size *= 2
            return off

        used = lax.fori_loop(0, N_EXPERTS, pad_expert, 0)

        def pad_block(b, _):
            first = pl.multiple_of(used + b * EXPERT_ROWS, EXPERT_ROWS)
            pltpu.make_async_copy(zero_ref, xg_ref.at[pl.ds(first, EXPERT_ROWS)], zsem).start()
            return 0

        lax.fori_loop(0, (xg_ref.shape[0] - used) // EXPERT_ROWS, pad_block, 0)
        pltpu.make_async_copy(xg_ref.at[pl.ds(0, n_pad)], xg_ref.at[pl.ds(0, n_pad)], zsem).wait()

    def issue(t, _):
        for k in range(TOP_K):
            _row_copy(h_ref, t, xg_ref, dest_ref[t * TOP_K + k], sem).start()
        return 0

    lax.fori_loop(0, tm, issue, 0, unroll=2)
    pltpu.make_async_copy(xg_ref.at[pl.ds(0, tm * TOP_K)], xg_ref.at[pl.ds(0, tm * TOP_K)], sem).wait()


def _dispatch_call(dest_flat, start, counts, h, n_rows, tm):
    t_tok, d = h.shape
    smem_blk = pl.BlockSpec((tm * TOP_K,), lambda i: (i,), memory_space=pltpu.SMEM)
    smem = pl.BlockSpec(memory_space=pltpu.SMEM)
    return pl.pallas_call(
        functools.partial(_dispatch_kernel, tm=tm, n_pad=n_rows - t_tok * TOP_K),
        out_shape=jax.ShapeDtypeStruct((n_rows, d), F32),
        grid=(t_tok // tm,),
        in_specs=[smem_blk, smem, smem, pl.BlockSpec((tm, d), lambda i: (i, 0))],
        out_specs=pl.BlockSpec(memory_space=pl.ANY),
        scratch_shapes=[pltpu.VMEM((EXPERT_ROWS, d), F32), pltpu.SemaphoreType.DMA(()),
                        pltpu.SemaphoreType.DMA(())],
        compiler_params=pltpu.CompilerParams(dimension_semantics=("arbitrary",),
                                             vmem_limit_bytes=VMEM_LIMIT),
        name="moe_dispatch",
    )(dest_flat, start, counts, h)


def _expert_kernel(be_ref, nu_ref, x_ref, wgu_ref, wd_ref, bg_ref, bu_ref, bd_ref, o_ref,
                   wgt_ref, wut_ref, wdb_ref, tmp_ref):
    i = pl.program_id(0)
    live = i < nu_ref[0]
    chunk = 2 * LANES

    @pl.when(live & ((i == 0) | (be_ref[i] != be_ref[jnp.maximum(i - 1, 0)])))
    def _():
        for c in range(wgu_ref.shape[1] // chunk):
            rows = slice(c * LANES, (c + 1) * LANES)
            for k in range(wgu_ref.shape[0] // LANES):
                cols = slice(k * LANES, (k + 1) * LANES)
                slot = (c * (wgu_ref.shape[0] // LANES) + k) % tmp_ref.shape[0]
                tmp_ref[slot] = wgu_ref[cols, c * chunk:(c + 1) * chunk].T
                wgt_ref[rows, cols] = tmp_ref[slot, pl.ds(0, LANES, stride=2), :].astype(BF16)
                wut_ref[rows, cols] = tmp_ref[slot, pl.ds(1, LANES, stride=2), :].astype(BF16)
        wdb_ref[...] = wd_ref[...].astype(BF16)

    @pl.when(live)
    def _():
        nt = (((1,), (1,)), ((), ()))
        chains = [dict(rows=slice(c * MXU_ROWS, (c + 1) * MXU_ROWS)) for c in range(EXPERT_ROWS // MXU_ROWS)]

        def gate_up(c):
            x = x_ref[c["rows"], :].astype(BF16)
            c["gate"] = lax.dot_general(x, wgt_ref[...], nt, preferred_element_type=F32)
            c["up"] = lax.dot_general(x, wut_ref[...], nt, preferred_element_type=F32)

        def activation(c):
            gate = jnp.minimum(c.pop("gate") + bg_ref[...], SWIGLU_LIMIT)
            up = jnp.clip(c.pop("up") + bu_ref[...], -SWIGLU_LIMIT, SWIGLU_LIMIT)
            c["act"] = ((up + 1.0) * gate * jax.nn.sigmoid(SWIGLU_ALPHA * gate)).astype(BF16)

        def down(c):
            o_ref[c["rows"], :] = _dot(c.pop("act"), wdb_ref[...]) + bd_ref[...]

        _skewed(chains, ((gate_up, 0), (down, 2), (activation, 1)))

    @pl.when(jnp.logical_not(live))
    def _():
        o_ref[...] = jnp.zeros_like(o_ref)


def _expert_call(block_expert, n_used, xg, w_gate_up, w_down, bg, bu, bd):
    n_rows, d = xg.shape
    f = w_down.shape[1]
    n_blocks = n_rows // EXPERT_ROWS
    blk = lambda i, be, nu: (jnp.minimum(i, nu[0] - 1), 0)
    wsel = lambda i, be, nu: (be[i], 0, 0)
    return pl.pallas_call(
        _expert_kernel,
        out_shape=jax.ShapeDtypeStruct((n_rows, d), F32),
        grid_spec=pltpu.PrefetchScalarGridSpec(
            num_scalar_prefetch=2,
            grid=(n_blocks,),
            in_specs=[pl.BlockSpec((EXPERT_ROWS, d), blk),
                      pl.BlockSpec((None, d, 2 * f), wsel),
                      pl.BlockSpec((None, f, d), wsel),
                      pl.BlockSpec((None, 1, f), wsel),
                      pl.BlockSpec((None, 1, f), wsel),
                      pl.BlockSpec((None, 1, d), wsel)],
            out_specs=pl.BlockSpec((EXPERT_ROWS, d), lambda i, be, nu: (i, 0)),
            scratch_shapes=[pltpu.VMEM((f, d), BF16), pltpu.VMEM((f, d), BF16), pltpu.VMEM((f, d), BF16),
                            pltpu.VMEM((4, 2 * LANES, LANES), F32)]),
        compiler_params=pltpu.CompilerParams(dimension_semantics=("arbitrary",),
                                             vmem_limit_bytes=VMEM_LIMIT),
        name="expert_ffn",
    )(block_expert, n_used, xg, w_gate_up, w_down, bg, bu, bd)


def _combine_kernel(dest_ref, next_ref, h_ref, tw_ref, lng_ref, lnb_ref, rows_ref, o_ref, buf_ref, sems, *, tm):
    i = pl.program_id(0)
    slot = i % 2

    def gather(rows_of, into):
        def issue(t, _):
            for k in range(TOP_K):
                _row_copy(rows_ref, rows_of[t * TOP_K + k], buf_ref.at[into, k], t, sems.at[into]).start()
            return 0
        lax.fori_loop(0, tm, issue, 0, unroll=2)

    @pl.when(i == 0)
    def _():
        gather(dest_ref, 0)

    @pl.when(i + 1 < pl.num_programs(0))
    def _():
        gather(next_ref, 1 - slot)

    pltpu.make_async_copy(buf_ref.at[slot], buf_ref.at[slot], sems.at[slot]).wait()
    tw = tw_ref[...]
    y = tw[:, 0:1] * buf_ref[slot, 0]
    for k in range(1, TOP_K):
        y = y + tw[:, k:k + 1] * buf_ref[slot, k]
    o_ref[...] = _layer_norm(DEEPNORM_ALPHA * h_ref[...] + y, lng_ref[...], lnb_ref[...])


def _combine_call(dest_flat, h, tw, lng, lnb, rows, tm):
    t_tok, d = h.shape
    n_tiles = t_tok // tm
    return pl.pallas_call(
        functools.partial(_combine_kernel, tm=tm),
        out_shape=jax.ShapeDtypeStruct((t_tok, d), F32),
        grid=(n_tiles,),
        in_specs=[pl.BlockSpec((tm * TOP_K,), lambda i: (i,), memory_space=pltpu.SMEM),
                  pl.BlockSpec((tm * TOP_K,), lambda i: (jnp.minimum(i + 1, n_tiles - 1),),
                               memory_space=pltpu.SMEM),
                  pl.BlockSpec((tm, d), lambda i: (i, 0)),
                  pl.BlockSpec((tm, 8), lambda i: (i, 0)),
                  pl.BlockSpec((1, d), lambda i: (0, 0)),
                  pl.BlockSpec((1, d), lambda i: (0, 0)),
                  pl.BlockSpec(memory_space=pl.ANY)],
        out_specs=pl.BlockSpec((tm, d), lambda i: (i, 0)),
        scratch_shapes=[pltpu.VMEM((2, TOP_K, tm, d), F32), pltpu.SemaphoreType.DMA((2,))],
        compiler_params=pltpu.CompilerParams(dimension_semantics=("arbitrary",),
                                             vmem_limit_bytes=VMEM_LIMIT),
        name="moe_combine_ln",
    )(dest_flat, dest_flat, h, tw, lng, lnb, rows)


def _layer(x2, pos, batch, seq, w_in, b_gate, w_branch_sb, w_branch_moba, w_out, ln_mix_g, ln_mix_b,
           w_router, b_router, w_gate_up, b_gate_up, w_down, b_down, ln_ffn_g, ln_ffn_b):
    t_tok, d = x2.shape
    nb = seq // KV_BLOCK
    qkv_w = 6 * WIDTH
    tm_a = 512 if t_tok % 512 == 0 else KV_BLOCK
    tm_mix = 1024 if t_tok % 1024 == 0 else tm_a

    w_qkv = w_in[:, :qkv_w].astype(BF16)
    w_g = w_in[:, qkv_w:].astype(BF16)
    pos3 = pos.astype(F32).reshape(t_tok // tm_a, 1, tm_a)
    invf = (ROPE_THETA ** (-jnp.arange(ROPE_HALF, dtype=F32) / ROPE_HALF)).reshape(ROPE_HALF, 1)
    ki = jnp.arange(KV_BLOCK)
    tri = jnp.where(ki[None, :] > ki[:, None], -1.0, 0.0).astype(BF16)
    avg = jnp.where(jnp.arange(seq)[None, :] // KV_BLOCK == jnp.arange(nb)[:, None],
                    1.0 / KV_BLOCK, 0.0).astype(BF16)

    qt_sb, k_sb, vt_sb, qt_mb, k_mb, vt_mb = _qkv_call(x2, pos3, invf, w_qkv, tm_a)
    a_sb = _sb_call(qt_sb, k_sb, vt_sb, tri, batch, seq)
    a_mb = _moba_call(qt_mb, k_mb, vt_mb, avg, batch, seq)

    w_r = jnp.zeros((d, LANES), F32).at[:, :N_EXPERTS].set(w_router).astype(BF16)
    b_r = jnp.full((1, LANES), NEG_INF, F32).at[0, :N_EXPERTS].set(b_router)
    h, idx8, cnt8, tw8, counts = _mixer_call(
        x2, a_sb, a_mb, w_g, b_gate.reshape(1, -1), w_branch_sb.astype(BF16), w_branch_moba.astype(BF16),
        w_out.astype(BF16), ln_mix_g.reshape(1, -1), ln_mix_b.reshape(1, -1), w_r, b_r, tm_mix)

    counts = counts[0, :N_EXPERTS]
    padded = (counts + EXPERT_ROWS - 1) // EXPERT_ROWS * EXPERT_ROWS
    padded_end = jnp.cumsum(padded)
    start = (padded_end - padded).astype(I32)
    n_assign = t_tok * TOP_K
    n_blocks = -(-(n_assign + N_EXPERTS * (EXPERT_ROWS - 1)) // EXPERT_ROWS)
    block_first = jnp.arange(n_blocks, dtype=I32) * EXPERT_ROWS
    block_expert = jnp.minimum(jnp.sum(block_first[:, None] >= padded_end[None, :], axis=1),
                               N_EXPERTS - 1).astype(I32)
    n_used = (padded_end[-1:] // EXPERT_ROWS).astype(I32)

    start_row = jnp.zeros((1, LANES), I32).at[0, :N_EXPERTS].set(start)
    dest_flat = _rows_call(idx8, cnt8, start_row, tm_mix)[:, :TOP_K].reshape(-1)
    xg = _dispatch_call(dest_flat, start, counts, h, n_blocks * EXPERT_ROWS, tm_a)

    f = w_down.shape[1]
    rows = _expert_call(block_expert, n_used, xg, w_gate_up, w_down,
                        b_gate_up[:, 0::2].reshape(N_EXPERTS, 1, f), b_gate_up[:, 1::2].reshape(N_EXPERTS, 1, f),
                        b_down.reshape(N_EXPERTS, 1, d))
    return _combine_call(dest_flat, h, tw8, ln_ffn_g.reshape(1, -1), ln_ffn_b.reshape(1, -1), rows, KV_BLOCK)


def kernel(x, positions, w_in, b_gate, w_branch_sb, w_branch_moba, w_out, ln_mix_g, ln_mix_b, w_router,
           b_router, w_gate_up, b_gate_up, w_down, b_down, ln_ffn_g, ln_ffn_b):
    batch, seq, d = x.shape
    h = x.reshape(batch * seq, d)
    pos = positions.reshape(batch * seq)
    for layer in range(w_in.shape[0]):
        h = _layer(h, pos, batch, seq, w_in[layer], b_gate[layer], w_branch_sb[layer], w_branch_moba[layer],
                   w_out[layer], ln_mix_g[layer], ln_mix_b[layer], w_router[layer], b_router[layer],
                   w_gate_up[layer], b_gate_up[layer], w_down[layer], b_down[layer],
                   ln_ffn_g[layer], ln_ffn_b[layer])
    return h.reshape(batch, seq, d)
```

```python
import functools

import jax
import jax.numpy as jnp
from jax import lax
from jax.experimental import pallas as pl
from jax.experimental.pallas import tpu as pltpu

F32 = jnp.float32
BF16 = jnp.bfloat16
I32 = jnp.int32

HEAD_DIM = 64
N_HEADS = 8
WIDTH = N_HEADS * HEAD_DIM
LANES = 128
SUBLANES = 8
N_PAIRS = WIDTH // LANES
KV_BLOCK = 256
MOBA_TOPK = 3
ROPE_THETA = 500000.0
ROPE_HALF = 8
N_EXPERTS = 32
TOP_K = 4
SWIGLU_LIMIT = 7.0
SWIGLU_ALPHA = 1.702
EXPERT_ROWS = 512
MXU_ROWS = 256
LN_EPS = 1e-5
DEPTH = 1
DEEPNORM_ALPHA = (2 * DEPTH) ** 0.25
QK_SCALE = HEAD_DIM ** -0.5
NEG_INF = float("-inf")
VMEM_LIMIT = 56 * 1024 * 1024


def _dot(a, b):
    return jnp.dot(a, b, preferred_element_type=F32)


def _qkv_kernel(x_ref, pos_ref, invf_ref, w_ref,
                qt_sb_ref, k_sb_ref, vt_sb_ref, qt_mb_ref, k_mb_ref, vt_mb_ref, *, tm):
    xb = x_ref[...].astype(BF16)
    ang = invf_ref[...] * pos_ref[0]
    cos, sin = jnp.cos(ang), jnp.sin(ang)

    def rope_t(t):
        parts = []
        for base in (0, HEAD_DIM):
            x1 = t[base:base + ROPE_HALF]
            x2 = t[base + ROPE_HALF:base + 2 * ROPE_HALF]
            parts += [x1 * cos - x2 * sin, x2 * cos + x1 * sin, t[base + 2 * ROPE_HALF:base + HEAD_DIM]]
        return jnp.concatenate(parts, axis=0)

    def store_t(ref, p, t):
        tb = t.astype(BF16)
        for blk in range(tm // KV_BLOCK):
            ref[blk, p] = tb[:, blk * KV_BLOCK:(blk + 1) * KV_BLOCK]

    for sec in range(6):
        for half in range(2):
            c0 = sec * WIDTH + half * 2 * LANES
            r = _dot(xb, w_ref[:, c0:c0 + 2 * LANES])
            for q in range(2):
                p = half * 2 + q
                t = r[:, q * LANES:(q + 1) * LANES]
                if sec == 0:
                    store_t(qt_sb_ref, p, (t * QK_SCALE).T)
                elif sec == 1:
                    k_sb_ref[:, p * LANES:(p + 1) * LANES] = t.astype(BF16)
                elif sec == 2:
                    store_t(vt_sb_ref, p, t.T)
                elif sec == 3:
                    store_t(qt_mb_ref, p, rope_t(t.T) * QK_SCALE)
                elif sec == 4:
                    k_mb_ref[:, p * LANES:(p + 1) * LANES] = rope_t(t.T).T.astype(BF16)
                else:
                    store_t(vt_mb_ref, p, t.T)


def _qkv_call(x2, pos3, invf, w_qkv, tm):
    t_tok, d = x2.shape
    nblk = t_tok // KV_BLOCK
    bpt = tm // KV_BLOCK
    t_shape = jax.ShapeDtypeStruct((nblk, N_PAIRS, LANES, KV_BLOCK), BF16)
    n_shape = jax.ShapeDtypeStruct((t_tok, WIDTH), BF16)
    t_spec = pl.BlockSpec((bpt, N_PAIRS, LANES, KV_BLOCK), lambda i: (i, 0, 0, 0))
    n_spec = pl.BlockSpec((tm, WIDTH), lambda i: (i, 0))
    return pl.pallas_call(
        functools.partial(_qkv_kernel, tm=tm),
        out_shape=(t_shape, n_shape, t_shape, t_shape, n_shape, t_shape),
        grid=(t_tok // tm,),
        in_specs=[pl.BlockSpec((tm, d), lambda i: (i, 0)),
                  pl.BlockSpec((1, 1, tm), lambda i: (i, 0, 0)),
                  pl.BlockSpec((ROPE_HALF, 1), lambda i: (0, 0)),
                  pl.BlockSpec((d, 6 * WIDTH), lambda i: (0, 0))],
        out_specs=(t_spec, n_spec, t_spec, t_spec, n_spec, t_spec),
        compiler_params=pltpu.CompilerParams(dimension_semantics=("arbitrary",),
                                             vmem_limit_bytes=VMEM_LIMIT),
        name="qkv_proj",
    )(x2, pos3, invf, w_qkv)


def _head_rows(h):
    row = lax.broadcasted_iota(I32, (LANES, KV_BLOCK), 0)
    return (row >= HEAD_DIM * h) & (row < HEAD_DIM * (h + 1))


def _tile_specs(seq):
    nb = seq // KV_BLOCK
    tiles_spec = pl.BlockSpec((nb, None, LANES, KV_BLOCK), lambda b, p, i: (b, p, 0, 0))
    col_spec = pl.BlockSpec((seq, LANES), lambda b, p, i: (b, p))
    return tiles_spec, col_spec


def _chain_list(i, nb, past_block):
    tiles = (i, nb - 1 - i)
    chains = [dict(j=tiles[a], a=a, h=h, diagonal=True, first=None) for a in range(2) for h in range(2)]
    for t in range(nb - 1):
        first = t < i
        for h in range(2):
            chains.append(dict(j=past_block(first, t), a=jnp.where(first, 0, 1), h=h, diagonal=False, first=first))
    return tiles, chains


def _skewed(chains, stages):
    n = len(chains)
    for slot in range(n + max(lag for _, lag in stages)):
        for stage, lag in stages:
            if 0 <= slot - lag < n:
                stage(chains[slot - lag])


def _sb_kernel(qt_ref, k_ref, vt_ref, tri_ref, o_ref, q_scr, acc_scr):
    i = pl.program_id(2)
    nb = qt_ref.shape[0]
    tiles, chains = _chain_list(i, nb, lambda first, t: jnp.where(first, i - 1 - t, nb - 2 - t))
    for a in range(2):
        qt = qt_ref[tiles[a]]
        for h in range(2):
            q_scr[a, h] = jnp.where(_head_rows(h), qt, jnp.zeros_like(qt))
    acc_scr[...] = jnp.zeros_like(acc_scr)
    key = lax.broadcasted_iota(I32, (KV_BLOCK, KV_BLOCK), 0)
    qry = lax.broadcasted_iota(I32, (KV_BLOCK, KV_BLOCK), 1)
    past = key < qry
    tri = tri_ref[...]

    zero = jnp.zeros((1, KV_BLOCK), F32)
    carries = [[zero, zero], [zero, zero]]

    def scores(c):
        kb = k_ref[pl.ds(pl.multiple_of(c["j"] * KV_BLOCK, KV_BLOCK), KV_BLOCK), :]
        c["z"] = _dot(kb, q_scr[c["a"], c["h"]])

    def softplus(c):
        z = c.pop("z")
        sp = jnp.maximum(z, 0.0) + jnp.log(1.0 + jnp.exp(-jnp.abs(z)))
        spm = jnp.where(past, sp, 0.0) if c["diagonal"] else sp
        c["log_beta"] = z - sp
        c["spm"] = spm.astype(BF16)
        h, first = c["h"], c["first"]
        if first is None:
            c["carry"] = zero
            carries[c["a"]][h] = -jnp.sum(spm, axis=0, keepdims=True)
        else:
            c["carry"] = jnp.where(first, carries[0][h], carries[1][h])
            cout = c["carry"] - jnp.sum(spm, axis=0, keepdims=True)
            carries[0][h] = jnp.where(first, cout, carries[0][h])
            carries[1][h] = jnp.where(first, carries[1][h], cout)

    def suffix(c):
        c["after"] = _dot(tri, c.pop("spm"))

    def weights(c):
        w = jnp.exp(c.pop("log_beta") + c.pop("after"))
        if c["diagonal"]:
            w = jnp.where(past, w, 0.0)
        c["w"] = w.astype(BF16)

    def values(c):
        rows = slice(c["h"] * HEAD_DIM, (c["h"] + 1) * HEAD_DIM)
        pv = _dot(vt_ref[c["j"], rows, :], c.pop("w")) * jnp.exp(c.pop("carry"))
        acc_scr[c["a"], rows, :] += pv

    _skewed(chains, ((scores, 0), (suffix, 2), (values, 4), (softplus, 1), (weights, 3)))
    for a in range(2):
        o_ref[pl.ds(pl.multiple_of(tiles[a] * KV_BLOCK, KV_BLOCK), KV_BLOCK), :] = acc_scr[a].T.astype(BF16)


def _sb_call(qt, k, vt, tri, batch, seq):
    nb = seq // KV_BLOCK
    tiles_spec, col_spec = _tile_specs(seq)
    return pl.pallas_call(
        _sb_kernel,
        out_shape=jax.ShapeDtypeStruct((batch * seq, WIDTH), BF16),
        grid=(batch, N_PAIRS, nb // 2),
        in_specs=[tiles_spec, col_spec, tiles_spec,
                  pl.BlockSpec((KV_BLOCK, KV_BLOCK), lambda b, p, i: (0, 0))],
        out_specs=col_spec,
        scratch_shapes=[pltpu.VMEM((2, 2, LANES, KV_BLOCK), BF16), pltpu.VMEM((2, LANES, KV_BLOCK), F32)],
        compiler_params=pltpu.CompilerParams(
            dimension_semantics=("arbitrary", "arbitrary", "arbitrary"), vmem_limit_bytes=VMEM_LIMIT),
        name="stickbreak_attn",
    )(qt, k, vt, tri)


def _moba_kernel(qt_ref, k_ref, vt_ref, avg_ref, o_ref, km_ref, bias_scr, q_scr, acc_scr, s_scr):
    i = pl.program_id(2)
    nb = qt_ref.shape[0]

    @pl.when(i == 0)
    def _():
        km_ref[...] = _dot(avg_ref[...], k_ref[...])

    tiles, chains = _chain_list(i, nb, lambda first, t: jnp.where(first, t, t - i))
    key = lax.broadcasted_iota(I32, (KV_BLOCK, KV_BLOCK), 0)
    qry = lax.broadcasted_iota(I32, (KV_BLOCK, KV_BLOCK), 1)
    causal = key <= qry
    blk = lax.broadcasted_iota(I32, (nb, KV_BLOCK), 0)
    km = km_ref[...].astype(BF16)
    for a in range(2):
        qt = qt_ref[tiles[a]]
        valid = blk < tiles[a]
        for h in range(2):
            qth = jnp.where(_head_rows(h), qt, jnp.zeros_like(qt))
            q_scr[a, h] = qth
            g = jnp.where(valid, _dot(km, qth), NEG_INF)
            rank = jnp.zeros((nb, KV_BLOCK), F32)
            for jp in range(nb):
                gj = g[jp:jp + 1, :]
                better = jnp.where(gj > g, 1.0, jnp.where(gj == g, jnp.where(blk > jp, 1.0, 0.0), 0.0))
                rank = rank + better
            bias_scr[a, h] = jnp.where(valid, jnp.where(rank < MOBA_TOPK, 0.0, NEG_INF), NEG_INF)

    for n, c in enumerate(chains):
        c["n"] = n
    low = jnp.full((SUBLANES, KV_BLOCK), NEG_INF, F32)
    maxima = [[low, low], [low, low]]
    sums = [[0.0, 0.0], [0.0, 0.0]]
    acc_scr[...] = jnp.zeros_like(acc_scr)

    def scores(c):
        kb = k_ref[pl.ds(pl.multiple_of(c["j"] * KV_BLOCK, KV_BLOCK), KV_BLOCK), :]
        c["s"] = _dot(kb, q_scr[c["a"], c["h"]])

    def mask(c):
        h, first = c["h"], c["first"]
        if first is None:
            s = jnp.where(causal, c.pop("s"), NEG_INF)
        else:
            s = c.pop("s") + bias_scr[c["a"], h, pl.ds(c["j"], 1), :]
        s_scr[c["n"]] = s
        top = jnp.max(s.reshape(KV_BLOCK // SUBLANES, SUBLANES, KV_BLOCK), axis=0)
        if first is None:
            maxima[c["a"]][h] = top
        else:
            maxima[0][h] = jnp.where(first, jnp.maximum(maxima[0][h], top), maxima[0][h])
            maxima[1][h] = jnp.where(first, maxima[1][h], jnp.maximum(maxima[1][h], top))

    _skewed(chains, ((scores, 0), (mask, 1)))
    maxima = [[jnp.max(maxima[a][h], axis=0, keepdims=True) for h in range(2)] for a in range(2)]

    def weights(c):
        h, first = c["h"], c["first"]
        m = maxima[c["a"]][h] if first is None else jnp.where(first, maxima[0][h], maxima[1][h])
        p = jnp.exp(s_scr[c["n"]] - m)
        l = jnp.sum(p, axis=0, keepdims=True)
        if first is None:
            sums[c["a"]][h] = sums[c["a"]][h] + l
        else:
            sums[0][h] = sums[0][h] + jnp.where(first, l, 0.0)
            sums[1][h] = sums[1][h] + jnp.where(first, 0.0, l)
        c["p"] = p.astype(BF16)

    def values(c):
        rows = slice(c["h"] * HEAD_DIM, (c["h"] + 1) * HEAD_DIM)
        acc_scr[c["a"], rows, :] += _dot(vt_ref[c["j"], rows, :], c.pop("p"))

    _skewed(chains, ((values, 1), (weights, 0)))
    for a in range(2):
        for h in range(2):
            rows = slice(h * HEAD_DIM, (h + 1) * HEAD_DIM)
            acc_scr[a, rows, :] = acc_scr[a, rows, :] / sums[a][h]
        o_ref[pl.ds(pl.multiple_of(tiles[a] * KV_BLOCK, KV_BLOCK), KV_BLOCK), :] = acc_scr[a].T.astype(BF16)


def _moba_call(qt, k, vt, avg, batch, seq):
    nb = seq // KV_BLOCK
    tiles_spec, col_spec = _tile_specs(seq)
    return pl.pallas_call(
        _moba_kernel,
        out_shape=jax.ShapeDtypeStruct((batch * seq, WIDTH), BF16),
        grid=(batch, N_PAIRS, nb // 2),
        in_specs=[tiles_spec, col_spec, tiles_spec, pl.BlockSpec((nb, seq), lambda b, p, i: (0, 0))],
        out_specs=col_spec,
        scratch_shapes=[pltpu.VMEM((nb, LANES), F32), pltpu.VMEM((2, 2, nb, KV_BLOCK), F32),
                        pltpu.VMEM((2, 2, LANES, KV_BLOCK), BF16), pltpu.VMEM((2, LANES, KV_BLOCK), F32),
                        pltpu.VMEM((2 * (nb + 1), KV_BLOCK, KV_BLOCK), F32)],
        compiler_params=pltpu.CompilerParams(
            dimension_semantics=("arbitrary", "arbitrary", "arbitrary"), vmem_limit_bytes=VMEM_LIMIT),
        name="moba_attn",
    )(qt, k, vt, avg)


def _layer_norm(r, g, b):
    mu = jnp.mean(r, axis=-1, keepdims=True)
    d = r - mu
    var = jnp.mean(d * d, axis=-1, keepdims=True)
    return d * lax.rsqrt(var + LN_EPS) * g + b


def _mixer_kernel(x_ref, asb_ref, amb_ref, wg_ref, bg_ref, wbs_ref, wbm_ref, wo_ref, lng_ref, lnb_ref,
                  wr_ref, br_ref, h_ref, idx_ref, cnt_ref, tw_ref, counts_ref, carry_ref, *, tm, d):
    @pl.when(pl.program_id(0) == 0)
    def _():
        carry_ref[...] = jnp.zeros_like(carry_ref)

    rows_per = MXU_ROWS
    chains = [dict(rows=slice(c * rows_per, (c + 1) * rows_per)) for c in range(tm // rows_per)]
    lane = lax.broadcasted_iota(I32, (rows_per, LANES), 1)
    lane8 = lax.broadcasted_iota(I32, (rows_per, 8), 1)
    rt = lax.broadcasted_iota(I32, (rows_per, rows_per), 0)
    ct = lax.broadcasted_iota(I32, (rows_per, rows_per), 1)
    lower = jnp.where(ct < rt, 1.0, 0.0).astype(BF16)

    def project(c):
        c["gp"] = _dot(x_ref[c["rows"], :].astype(BF16), wg_ref[...])
        c["ysb"] = _dot(asb_ref[c["rows"], :], wbs_ref[...])
        c["ymb"] = _dot(amb_ref[c["rows"], :], wbm_ref[...])

    def gate(c):
        g = jax.nn.sigmoid(c.pop("gp") + bg_ref[...])
        c["mixed"] = (g[:, :d] * c.pop("ysb") + g[:, d:] * c.pop("ymb")).astype(BF16)

    def out_proj(c):
        c["mix"] = _dot(c.pop("mixed"), wo_ref[...])

    def norm(c):
        h = _layer_norm(DEEPNORM_ALPHA * x_ref[c["rows"], :] + c.pop("mix"), lng_ref[...], lnb_ref[...])
        h_ref[c["rows"], :] = h
        c["hb"] = h.astype(BF16)

    def route(c):
        c["logits"] = _dot(c.pop("hb"), wr_ref[...])

    def top_k(c):
        logits = c.pop("logits") + br_ref[...]
        vals, idxs = [], []
        member = jnp.zeros((rows_per, LANES), F32)
        for _ in range(TOP_K):
            mx = jnp.max(logits, axis=-1, keepdims=True)
            ik = jnp.min(jnp.where(logits == mx, lane, LANES), axis=-1, keepdims=True)
            hit = lane == ik
            vals.append(mx)
            idxs.append(ik)
            member = jnp.where(hit, 1.0, member)
            logits = jnp.where(hit, NEG_INF, logits)
        es = [jnp.exp(v - vals[0]) for v in vals]
        den = es[0] + es[1] + es[2] + es[3]
        idx_out = jnp.zeros((rows_per, 8), I32)
        tw_out = jnp.zeros((rows_per, 8), F32)
        for k in range(TOP_K):
            idx_out = jnp.where(lane8 == k, idxs[k], idx_out)
            tw_out = jnp.where(lane8 == k, es[k] / den, tw_out)
        idx_ref[c["rows"], :] = idx_out
        tw_ref[c["rows"], :] = tw_out
        c["idxs"] = idxs
        c["member"] = member

    def count(c):
        member = c.pop("member")
        cnt = _dot(lower, member.astype(BF16)) + carry_ref[...]
        carry_ref[...] = carry_ref[...] + jnp.sum(member, axis=0, keepdims=True)
        cnt_out = jnp.zeros((rows_per, 8), I32)
        for k, ik in enumerate(c.pop("idxs")):
            ck = jnp.sum(jnp.where(lane == ik, cnt, 0.0), axis=-1, keepdims=True)
            cnt_out = jnp.where(lane8 == k, ck.astype(I32), cnt_out)
        cnt_ref[c["rows"], :] = cnt_out

    _skewed(chains, ((project, 0), (out_proj, 2), (route, 4), (count, 6), (gate, 1), (norm, 3), (top_k, 5)))
    counts_ref[...] = carry_ref[...].astype(I32)


def _mixer_call(x2, a_sb, a_mb, wg, bg, wbs, wbm, wo, lng, lnb, wr, br, tm):
    t_tok, d = x2.shape
    const = lambda shape: pl.BlockSpec(shape, lambda i: (0,) * len(shape))
    row8 = pl.BlockSpec((tm, 8), lambda i: (i, 0))
    return pl.pallas_call(
        functools.partial(_mixer_kernel, tm=tm, d=d),
        out_shape=(jax.ShapeDtypeStruct((t_tok, d), F32),
                   jax.ShapeDtypeStruct((t_tok, 8), I32),
                   jax.ShapeDtypeStruct((t_tok, 8), I32),
                   jax.ShapeDtypeStruct((t_tok, 8), F32),
                   jax.ShapeDtypeStruct((1, LANES), I32)),
        grid=(t_tok // tm,),
        in_specs=[pl.BlockSpec((tm, d), lambda i: (i, 0)),
                  pl.BlockSpec((tm, WIDTH), lambda i: (i, 0)),
                  pl.BlockSpec((tm, WIDTH), lambda i: (i, 0)),
                  const((d, 2 * d)), const((1, 2 * d)),
                  const((WIDTH, d)), const((WIDTH, d)), const((d, d)),
                  const((1, d)), const((1, d)),
                  const((d, LANES)), const((1, LANES))],
        out_specs=(pl.BlockSpec((tm, d), lambda i: (i, 0)), row8, row8, row8, const((1, LANES))),
        scratch_shapes=[pltpu.VMEM((1, LANES), F32)],
        compiler_params=pltpu.CompilerParams(dimension_semantics=("arbitrary",),
                                             vmem_limit_bytes=VMEM_LIMIT),
        name="mixer_ln_router",
    )(x2, a_sb, a_mb, wg, bg, wbs, wbm, wo, lng, lnb, wr, br)


def _rows_kernel(idx_ref, cnt_ref, start_ref, dest_ref):
    idx, cnt = idx_ref[...], cnt_ref[...]
    tm = idx.shape[0]
    lane = lax.broadcasted_iota(I32, (tm, LANES), 1)
    lane8 = lax.broadcasted_iota(I32, (tm, 8), 1)
    start = start_ref[...].astype(F32)
    dest = jnp.zeros((tm, 8), I32)
    for k in range(TOP_K):
        base = jnp.sum(jnp.where(lane == idx[:, k:k + 1], start, 0.0), axis=-1, keepdims=True)
        dest = jnp.where(lane8 == k, base.astype(I32) + cnt[:, k:k + 1], dest)
    dest_ref[...] = dest


def _rows_call(idx8, cnt8, start_row, tm):
    t_tok = idx8.shape[0]
    row8 = pl.BlockSpec((tm, 8), lambda i: (i, 0))
    return pl.pallas_call(
        _rows_kernel,
        out_shape=jax.ShapeDtypeStruct((t_tok, 8), I32),
        grid=(t_tok // tm,),
        in_specs=[row8, row8, pl.BlockSpec((1, LANES), lambda i: (0, 0))],
        out_specs=row8,
        compiler_params=pltpu.CompilerParams(dimension_semantics=("arbitrary",)),
        name="moe_rows",
    )(idx8, cnt8, start_row)


def _row_copy(src_ref, s, dst_ref, t, sem):
    return pltpu.make_async_copy(src_ref.at[pl.ds(s, 1)], dst_ref.at[pl.ds(t, 1)], sem)


def _dispatch_kernel(dest_ref, start_ref, counts_ref, h_ref, xg_ref, zero_ref, sem, zsem, *, tm, n_pad):
    @pl.when(pl.program_id(0) == 0)
    def _():
        zero_ref[...] = jnp.zeros_like(zero_ref)

        def pad_expert(e, _):
            n = (EXPERT_ROWS - counts_ref[e] % EXPERT_ROWS) % EXPERT_ROWS
            off = start_ref[e] + counts_ref[e]
            for s in range(SUBLANES - 1):
                @pl.when(s < (n & (SUBLANES - 1)))
                def _(row=off + s):
                    _row_copy(zero_ref, 0, xg_ref, row, zsem).start()
            off = off + (n & (SUBLANES - 1))
            size = SUBLANES
            while size < EXPERT_ROWS:
                @pl.when((n & size) != 0)
                def _(off=off, size=size):
                    dst = xg_ref.at[pl.ds(pl.multiple_of(off, SUBLANES), size)]
                    pltpu.make_async_copy(zero_ref.at[pl.ds(0, size)], dst, zsem).start()
                off = off + (n & size)
                size *= 2
            return off

        used = lax.fori_loop(0, N_EXPERTS, pad_expert, 0)

        def pad_block(b, _):
            first = pl.multiple_of(used + b * EXPERT_ROWS, EXPERT_ROWS)
            pltpu.make_async_copy(zero_ref, xg_ref.at[pl.ds(first, EXPERT_ROWS)], zsem).start()
            return 0

        lax.fori_loop(0, (xg_ref.shape[0] - used) // EXPERT_ROWS, pad_block, 0)
        pltpu.make_async_copy(xg_ref.at[pl.ds(0, n_pad)], xg_ref.at[pl.ds(0, n_pad)], zsem).wait()

    def issue(t, _):
        for k in range(TOP_K):
            _row_copy(h_ref, t, xg_ref, dest_ref[t * TOP_K + k], sem).start()
        return 0

    lax.fori_loop(0, tm, issue, 0, unroll=2)
    pltpu.make_async_copy(xg_ref.at[pl.ds(0, tm * TOP_K)], xg_ref.at[pl.ds(0, tm * TOP_K)], sem).wait()


def _dispatch_call(dest_flat, start, counts, h, n_rows, tm):
    t_tok, d = h.shape
    smem_blk = pl.BlockSpec((tm * TOP_K,), lambda i: (i,), memory_space=pltpu.SMEM)
    smem = pl.BlockSpec(memory_space=pltpu.SMEM)
    return pl.pallas_call(
        functools.partial(_dispatch_kernel, tm=tm, n_pad=n_rows - t_tok * TOP_K),
        out_shape=jax.ShapeDtypeStruct((n_rows, d), F32),
        grid=(t_tok // tm,),
        in_specs=[smem_blk, smem, smem, pl.BlockSpec((tm, d), lambda i: (i, 0))],
        out_specs=pl.BlockSpec(memory_space=pl.ANY),
        scratch_shapes=[pltpu.VMEM((EXPERT_ROWS, d), F32), pltpu.SemaphoreType.DMA(()),
                        pltpu.SemaphoreType.DMA(())],
        compiler_params=pltpu.CompilerParams(dimension_semantics=("arbitrary",),
                                             vmem_limit_bytes=VMEM_LIMIT),
        name="moe_dispatch",
    )(dest_flat, start, counts, h)


CTL_FIRST = 1
CTL_LAST = 2
CTL_ODD = 4


def _expert_kernel(be_ref, nx_ref, ctl_ref, nu_ref, x_ref, wgu_hbm, wd_hbm, bg_ref, bu_ref, bd_ref, o_ref,
                   wgu_stage, wd_stage, wgt0, wut0, wdb0, wgt1, wut1, wdb1, tmp_ref, sems):
    i = pl.program_id(0)
    live = i < nu_ref[0]
    ctl = ctl_ref[i]
    sets = ((wgt0, wut0, wdb0), (wgt1, wut1, wdb1))
    chunk = 2 * LANES

    def fetch(e):
        return (pltpu.make_async_copy(wgu_hbm.at[e], wgu_stage, sems.at[0]),
                pltpu.make_async_copy(wd_hbm.at[e], wd_stage, sems.at[1]))

    def prepare(wgt_ref, wut_ref, wdb_ref):
        for c in range(wgu_stage.shape[1] // chunk):
            rows = slice(c * LANES, (c + 1) * LANES)
            for k in range(wgu_stage.shape[0] // LANES):
                cols = slice(k * LANES, (k + 1) * LANES)
                slot = (c * (wgu_stage.shape[0] // LANES) + k) % tmp_ref.shape[0]
                tmp_ref[slot] = wgu_stage[cols, c * chunk:(c + 1) * chunk].T
                wgt_ref[rows, cols] = tmp_ref[slot, pl.ds(0, LANES, stride=2), :].astype(BF16)
                wut_ref[rows, cols] = tmp_ref[slot, pl.ds(1, LANES, stride=2), :].astype(BF16)
        wdb_ref[...] = wd_stage[...].astype(BF16)

    def ffn(wgt_ref, wut_ref, wdb_ref):
        nt = (((1,), (1,)), ((), ()))
        chains = [dict(rows=slice(c * MXU_ROWS, (c + 1) * MXU_ROWS)) for c in range(EXPERT_ROWS // MXU_ROWS)]

        def gate_up(c):
            x = x_ref[c["rows"], :].astype(BF16)
            c["gate"] = lax.dot_general(x, wgt_ref[...], nt, preferred_element_type=F32)
            c["up"] = lax.dot_general(x, wut_ref[...], nt, preferred_element_type=F32)

        def activation(c):
            gate = jnp.minimum(c.pop("gate") + bg_ref[...], SWIGLU_LIMIT)
            up = jnp.clip(c.pop("up") + bu_ref[...], -SWIGLU_LIMIT, SWIGLU_LIMIT)
            c["act"] = ((up + 1.0) * gate * jax.nn.sigmoid(SWIGLU_ALPHA * gate)).astype(BF16)

        def down(c):
            o_ref[c["rows"], :] = _dot(c.pop("act"), wdb_ref[...]) + bd_ref[...]

        _skewed(chains, ((gate_up, 0), (down, 2), (activation, 1)))

    @pl.when(i == 0)
    def _():
        for cp in fetch(be_ref[0]):
            cp.start()
        for cp in fetch(be_ref[0]):
            cp.wait()
        prepare(*sets[0])

    @pl.when(live & ((ctl & CTL_FIRST) != 0))
    def _():
        for cp in fetch(nx_ref[i]):
            cp.start()

    for odd in range(2):
        mine = live & (((ctl & CTL_ODD) != 0) == bool(odd))

        @pl.when(mine & ((ctl & CTL_LAST) == 0))
        def _(odd=odd):
            ffn(*sets[odd])

        @pl.when(mine & ((ctl & CTL_LAST) != 0))
        def _(odd=odd):
            for cp in fetch(nx_ref[i]):
                cp.wait()
            ffn(*sets[odd])
            prepare(*sets[1 - odd])

    @pl.when(jnp.logical_not(live))
    def _():
        o_ref[...] = jnp.zeros_like(o_ref)


def _expert_call(block_expert, next_expert, ctl, n_used, xg, w_gate_up, w_down, bg, bu, bd):
    n_rows, d = xg.shape
    f = w_down.shape[1]
    n_blocks = n_rows // EXPERT_ROWS
    blk = lambda i, be, nx, ctl, nu: (jnp.minimum(i, nu[0] - 1), 0)
    bsel = lambda i, be, nx, ctl, nu: (be[i], 0, 0)
    prepared = [pltpu.VMEM((f, d), BF16) for _ in range(6)]
    return pl.pallas_call(
        _expert_kernel,
        out_shape=jax.ShapeDtypeStruct((n_rows, d), F32),
        grid_spec=pltpu.PrefetchScalarGridSpec(
            num_scalar_prefetch=4,
            grid=(n_blocks,),
            in_specs=[pl.BlockSpec((EXPERT_ROWS, d), blk),
                      pl.BlockSpec(memory_space=pl.ANY),
                      pl.BlockSpec(memory_space=pl.ANY),
                      pl.BlockSpec((None, 1, f), bsel),
                      pl.BlockSpec((None, 1, f), bsel),
                      pl.BlockSpec((None, 1, d), bsel)],
            out_specs=pl.BlockSpec((EXPERT_ROWS, d), lambda i, be, nx, ctl, nu: (i, 0)),
            scratch_shapes=[pltpu.VMEM((d, 2 * f), F32), pltpu.VMEM((f, d), F32), *prepared,
                            pltpu.VMEM((4, 2 * LANES, LANES), F32), pltpu.SemaphoreType.DMA((2,))]),
        compiler_params=pltpu.CompilerParams(dimension_semantics=("arbitrary",),
                                             vmem_limit_bytes=VMEM_LIMIT),
        name="expert_ffn",
    )(block_expert, next_expert, ctl, n_used, xg, w_gate_up, w_down, bg, bu, bd)


def _combine_kernel(dest_ref, next_ref, h_ref, tw_ref, lng_ref, lnb_ref, rows_ref, o_ref, buf_ref, sems, *, tm):
    i = pl.program_id(0)
    slot = i % 2

    def gather(rows_of, into):
        def issue(t, _):
            for k in range(TOP_K):
                _row_copy(rows_ref, rows_of[t * TOP_K + k], buf_ref.at[into, k], t, sems.at[into]).start()
            return 0
        lax.fori_loop(0, tm, issue, 0, unroll=2)

    @pl.when(i == 0)
    def _():
        gather(dest_ref, 0)

    @pl.when(i + 1 < pl.num_programs(0))
    def _():
        gather(next_ref, 1 - slot)

    pltpu.make_async_copy(buf_ref.at[slot], buf_ref.at[slot], sems.at[slot]).wait()
    tw = tw_ref[...]
    y = tw[:, 0:1] * buf_ref[slot, 0]
    for k in range(1, TOP_K):
        y = y + tw[:, k:k + 1] * buf_ref[slot, k]
    o_ref[...] = _layer_norm(DEEPNORM_ALPHA * h_ref[...] + y, lng_ref[...], lnb_ref[...])


def _combine_call(dest_flat, h, tw, lng, lnb, rows, tm):
    t_tok, d = h.shape
    n_tiles = t_tok // tm
    return pl.pallas_call(
        functools.partial(_combine_kernel, tm=tm),
        out_shape=jax.ShapeDtypeStruct((t_tok, d), F32),
        grid=(n_tiles,),
        in_specs=[pl.BlockSpec((tm * TOP_K,), lambda i: (i,), memory_space=pltpu.SMEM),
                  pl.BlockSpec((tm * TOP_K,), lambda i: (jnp.minimum(i + 1, n_tiles - 1),),
                               memory_space=pltpu.SMEM),
                  pl.BlockSpec((tm, d), lambda i: (i, 0)),
                  pl.BlockSpec((tm, 8), lambda i: (i, 0)),
                  pl.BlockSpec((1, d), lambda i: (0, 0)),
                  pl.BlockSpec((1, d), lambda i: (0, 0)),
                  pl.BlockSpec(memory_space=pl.ANY)],
        out_specs=pl.BlockSpec((tm, d), lambda i: (i, 0)),
        scratch_shapes=[pltpu.VMEM((2, TOP_K, tm, d), F32), pltpu.SemaphoreType.DMA((2,))],
        compiler_params=pltpu.CompilerParams(dimension_semantics=("arbitrary",),
                                             vmem_limit_bytes=VMEM_LIMIT),
        name="moe_combine_ln",
    )(dest_flat, dest_flat, h, tw, lng, lnb, rows)


def _layer(x2, pos, batch, seq, w_in, b_gate, w_branch_sb, w_branch_moba, w_out, ln_mix_g, ln_mix_b,
           w_router, b_router, w_gate_up, b_gate_up, w_down, b_down, ln_ffn_g, ln_ffn_b):
    t_tok, d = x2.shape
    nb = seq // KV_BLOCK
    qkv_w = 6 * WIDTH
    tm_a = 512 if t_tok % 512 == 0 else KV_BLOCK
    tm_mix = 1024 if t_tok % 1024 == 0 else tm_a

    w_qkv = w_in[:, :qkv_w].astype(BF16)
    w_g = w_in[:, qkv_w:].astype(BF16)
    pos3 = pos.astype(F32).reshape(t_tok // tm_a, 1, tm_a)
    invf = (ROPE_THETA ** (-jnp.arange(ROPE_HALF, dtype=F32) / ROPE_HALF)).reshape(ROPE_HALF, 1)
    ki = jnp.arange(KV_BLOCK)
    tri = jnp.where(ki[None, :] > ki[:, None], -1.0, 0.0).astype(BF16)
    avg = jnp.where(jnp.arange(seq)[None, :] // KV_BLOCK == jnp.arange(nb)[:, None],
                    1.0 / KV_BLOCK, 0.0).astype(BF16)

    qt_sb, k_sb, vt_sb, qt_mb, k_mb, vt_mb = _qkv_call(x2, pos3, invf, w_qkv, tm_a)
    a_sb = _sb_call(qt_sb, k_sb, vt_sb, tri, batch, seq)
    a_mb = _moba_call(qt_mb, k_mb, vt_mb, avg, batch, seq)

    w_r = jnp.zeros((d, LANES), F32).at[:, :N_EXPERTS].set(w_router).astype(BF16)
    b_r = jnp.full((1, LANES), NEG_INF, F32).at[0, :N_EXPERTS].set(b_router)
    h, idx8, cnt8, tw8, counts = _mixer_call(
        x2, a_sb, a_mb, w_g, b_gate.reshape(1, -1), w_branch_sb.astype(BF16), w_branch_moba.astype(BF16),
        w_out.astype(BF16), ln_mix_g.reshape(1, -1), ln_mix_b.reshape(1, -1), w_r, b_r, tm_mix)

    counts = counts[0, :N_EXPERTS]
    padded = (counts + EXPERT_ROWS - 1) // EXPERT_ROWS * EXPERT_ROWS
    padded_end = jnp.cumsum(padded)
    start = (padded_end - padded).astype(I32)
    n_assign = t_tok * TOP_K
    n_blocks = -(-(n_assign + N_EXPERTS * (EXPERT_ROWS - 1)) // EXPERT_ROWS)
    block_first = jnp.arange(n_blocks, dtype=I32) * EXPERT_ROWS
    block_expert = jnp.minimum(jnp.sum(block_first[:, None] >= padded_end[None, :], axis=1),
                               N_EXPERTS - 1).astype(I32)
    n_used = (padded_end[-1:] // EXPERT_ROWS).astype(I32)

    start_row = jnp.zeros((1, LANES), I32).at[0, :N_EXPERTS].set(start)
    dest_flat = _rows_call(idx8, cnt8, start_row, tm_mix)[:, :TOP_K].reshape(-1)
    xg = _dispatch_call(dest_flat, start, counts, h, n_blocks * EXPERT_ROWS, tm_a)

    blk_idx = jnp.arange(n_blocks, dtype=I32)
    used = blk_idx < n_used[0]
    before = jnp.concatenate([block_expert[:1], block_expert[:-1]])
    after = jnp.concatenate([block_expert[1:], block_expert[-1:]])
    first_blk = used & ((blk_idx == 0) | (block_expert != before))
    last_blk = used & ((blk_idx == n_used[0] - 1) | (block_expert != after))
    run_end = (padded_end // EXPERT_ROWS).astype(I32)[block_expert]
    follows = run_end < n_used[0]
    next_expert = block_expert[jnp.minimum(run_end, n_blocks - 1)]
    odd = (jnp.cumsum(first_blk.astype(I32)) - 1) % 2
    ctl = ((first_blk & follows) * CTL_FIRST + (last_blk & follows) * CTL_LAST + odd * CTL_ODD).astype(I32)

    f = w_down.shape[1]
    rows = _expert_call(block_expert, next_expert, ctl, n_used, xg, w_gate_up, w_down,
                        b_gate_up[:, 0::2].reshape(N_EXPERTS, 1, f), b_gate_up[:, 1::2].reshape(N_EXPERTS, 1, f),
                        b_down.reshape(N_EXPERTS, 1, d))
    return _combine_call(dest_flat, h, tw8, ln_ffn_g.reshape(1, -1), ln_ffn_b.reshape(1, -1), rows, KV_BLOCK)


def kernel(x, positions, w_in, b_gate, w_branch_sb, w_branch_moba, w_out, ln_mix_g, ln_mix_b, w_router,
           b_router, w_gate_up, b_gate_up, w_down, b_down, ln_ffn_g, ln_ffn_b):
    batch, seq, d = x.shape
    h = x.reshape(batch * seq, d)
    pos = positions.reshape(batch * seq)
    for layer in range(w_in.shape[0]):
        h = _layer(h, pos, batch, seq, w_in[layer], b_gate[layer], w_branch_sb[layer], w_branch_moba[layer],
                   w_out[layer], ln_mix_g[layer], ln_mix_b[layer], w_router[layer], b_router[layer],
                   w_gate_up[layer], b_gate_up[layer], w_down[layer], b_down[layer],
                   ln_ffn_g[layer], ln_ffn_b[layer])
    return h.reshape(batch, seq, d)
```

```python
import functools

import jax
import jax.numpy as jnp
from jax import lax
from jax.experimental import pallas as pl
from jax.experimental.pallas import tpu as pltpu

F32 = jnp.float32
BF16 = jnp.bfloat16
I32 = jnp.int32

HEAD_DIM = 64
N_HEADS = 8
WIDTH = N_HEADS * HEAD_DIM
LANES = 128
SUBLANES = 8
N_PAIRS = WIDTH // LANES
KV_BLOCK = 256
MOBA_TOPK = 3
ROPE_THETA = 500000.0
ROPE_HALF = 8
N_EXPERTS = 32
TOP_K = 4
SWIGLU_LIMIT = 7.0
SWIGLU_ALPHA = 1.702
EXPERT_ROWS = 512
MXU_ROWS = 256
LN_EPS = 1e-5
DEPTH = 1
DEEPNORM_ALPHA = (2 * DEPTH) ** 0.25
QK_SCALE = HEAD_DIM ** -0.5
NEG_INF = float("-inf")
VMEM_LIMIT = 56 * 1024 * 1024


def _dot(a, b):
    return jnp.dot(a, b, preferred_element_type=F32)


def _qkv_kernel(x_ref, pos_ref, invf_ref, w_ref,
                qt_sb_ref, k_sb_ref, vt_sb_ref, qt_mb_ref, k_mb_ref, vt_mb_ref, *, tm):
    xb = x_ref[...].astype(BF16)
    ang = invf_ref[...] * pos_ref[0]
    cos, sin = jnp.cos(ang), jnp.sin(ang)

    def rope_t(t):
        parts = []
        for base in (0, HEAD_DIM):
            x1 = t[base:base + ROPE_HALF]
            x2 = t[base + ROPE_HALF:base + 2 * ROPE_HALF]
            parts += [x1 * cos - x2 * sin, x2 * cos + x1 * sin, t[base + 2 * ROPE_HALF:base + HEAD_DIM]]
        return jnp.concatenate(parts, axis=0)

    def store_t(ref, p, t):
        tb = t.astype(BF16)
        for blk in range(tm // KV_BLOCK):
            ref[blk, p] = tb[:, blk * KV_BLOCK:(blk + 1) * KV_BLOCK]

    for sec in range(6):
        for half in range(2):
            c0 = sec * WIDTH + half * 2 * LANES
            r = _dot(xb, w_ref[:, c0:c0 + 2 * LANES])
            for q in range(2):
                p = half * 2 + q
                t = r[:, q * LANES:(q + 1) * LANES]
                if sec == 0:
                    store_t(qt_sb_ref, p, (t * QK_SCALE).T)
                elif sec == 1:
                    k_sb_ref[:, p * LANES:(p + 1) * LANES] = t.astype(BF16)
                elif sec == 2:
                    store_t(vt_sb_ref, p, t.T)
                elif sec == 3:
                    store_t(qt_mb_ref, p, rope_t(t.T) * QK_SCALE)
                elif sec == 4:
                    k_mb_ref[:, p * LANES:(p + 1) * LANES] = rope_t(t.T).T.astype(BF16)
                else:
                    store_t(vt_mb_ref, p, t.T)


def _qkv_call(x2, pos3, invf, w_qkv, tm):
    t_tok, d = x2.shape
    nblk = t_tok // KV_BLOCK
    bpt = tm // KV_BLOCK
    t_shape = jax.ShapeDtypeStruct((nblk, N_PAIRS, LANES, KV_BLOCK), BF16)
    n_shape = jax.ShapeDtypeStruct((t_tok, WIDTH), BF16)
    t_spec = pl.BlockSpec((bpt, N_PAIRS, LANES, KV_BLOCK), lambda i: (i, 0, 0, 0))
    n_spec = pl.BlockSpec((tm, WIDTH), lambda i: (i, 0))
    return pl.pallas_call(
        functools.partial(_qkv_kernel, tm=tm),
        out_shape=(t_shape, n_shape, t_shape, t_shape, n_shape, t_shape),
        grid=(t_tok // tm,),
        in_specs=[pl.BlockSpec((tm, d), lambda i: (i, 0)),
                  pl.BlockSpec((1, 1, tm), lambda i: (i, 0, 0)),
                  pl.BlockSpec((ROPE_HALF, 1), lambda i: (0, 0)),
                  pl.BlockSpec((d, 6 * WIDTH), lambda i: (0, 0))],
        out_specs=(t_spec, n_spec, t_spec, t_spec, n_spec, t_spec),
        compiler_params=pltpu.CompilerParams(dimension_semantics=("arbitrary",),
                                             vmem_limit_bytes=VMEM_LIMIT),
        name="qkv_proj",
    )(x2, pos3, invf, w_qkv)


def _head_rows(h):
    row = lax.broadcasted_iota(I32, (LANES, KV_BLOCK), 0)
    return (row >= HEAD_DIM * h) & (row < HEAD_DIM * (h + 1))


def _tile_specs(seq):
    nb = seq // KV_BLOCK
    tiles_spec = pl.BlockSpec((nb, None, LANES, KV_BLOCK), lambda b, p, i: (b, p, 0, 0))
    col_spec = pl.BlockSpec((seq, LANES), lambda b, p, i: (b, p))
    return tiles_spec, col_spec


def _chain_list(i, nb, past_block):
    tiles = (i, nb - 1 - i)
    chains = [dict(j=tiles[a], a=a, h=h, diagonal=True, first=None) for a in range(2) for h in range(2)]
    for t in range(nb - 1):
        first = t < i
        for h in range(2):
            chains.append(dict(j=past_block(first, t), a=jnp.where(first, 0, 1), h=h, diagonal=False, first=first))
    return tiles, chains


def _skewed(chains, stages):
    n = len(chains)
    for slot in range(n + max(lag for _, lag in stages)):
        for stage, lag in stages:
            if 0 <= slot - lag < n:
                stage(chains[slot - lag])


def _sb_kernel(qt_ref, k_ref, vt_ref, tri_ref, o_ref, q_scr, acc_scr):
    i = pl.program_id(2)
    nb = qt_ref.shape[0]
    tiles, chains = _chain_list(i, nb, lambda first, t: jnp.where(first, i - 1 - t, nb - 2 - t))
    for a in range(2):
        qt = qt_ref[tiles[a]]
        for h in range(2):
            q_scr[a, h] = jnp.where(_head_rows(h), qt, jnp.zeros_like(qt))
    acc_scr[...] = jnp.zeros_like(acc_scr)
    key = lax.broadcasted_iota(I32, (KV_BLOCK, KV_BLOCK), 0)
    qry = lax.broadcasted_iota(I32, (KV_BLOCK, KV_BLOCK), 1)
    past = key < qry
    tri = tri_ref[...]

    zero = jnp.zeros((1, KV_BLOCK), F32)
    carries = [[zero, zero], [zero, zero]]

    def scores(c):
        kb = k_ref[pl.ds(pl.multiple_of(c["j"] * KV_BLOCK, KV_BLOCK), KV_BLOCK), :]
        c["z"] = _dot(kb, q_scr[c["a"], c["h"]])

    def softplus(c):
        z = c.pop("z")
        sp = jnp.maximum(z, 0.0) + jnp.log(1.0 + jnp.exp(-jnp.abs(z)))
        spm = jnp.where(past, sp, 0.0) if c["diagonal"] else sp
        c["log_beta"] = z - sp
        c["spm"] = spm.astype(BF16)
        h, first = c["h"], c["first"]
        if first is None:
            c["carry"] = zero
            carries[c["a"]][h] = -jnp.sum(spm, axis=0, keepdims=True)
        else:
            c["carry"] = jnp.where(first, carries[0][h], carries[1][h])
            cout = c["carry"] - jnp.sum(spm, axis=0, keepdims=True)
            carries[0][h] = jnp.where(first, cout, carries[0][h])
            carries[1][h] = jnp.where(first, carries[1][h], cout)

    def suffix(c):
        c["after"] = _dot(tri, c.pop("spm"))

    def weights(c):
        w = jnp.exp(c.pop("log_beta") + c.pop("after"))
        if c["diagonal"]:
            w = jnp.where(past, w, 0.0)
        c["w"] = w.astype(BF16)

    def values(c):
        rows = slice(c["h"] * HEAD_DIM, (c["h"] + 1) * HEAD_DIM)
        pv = _dot(vt_ref[c["j"], rows, :], c.pop("w")) * jnp.exp(c.pop("carry"))
        acc_scr[c["a"], rows, :] += pv

    _skewed(chains, ((scores, 0), (suffix, 2), (values, 4), (softplus, 1), (weights, 3)))
    for a in range(2):
        o_ref[pl.ds(pl.multiple_of(tiles[a] * KV_BLOCK, KV_BLOCK), KV_BLOCK), :] = acc_scr[a].T.astype(BF16)


def _sb_call(qt, k, vt, tri, batch, seq):
    nb = seq // KV_BLOCK
    tiles_spec, col_spec = _tile_specs(seq)
    return pl.pallas_call(
        _sb_kernel,
        out_shape=jax.ShapeDtypeStruct((batch * seq, WIDTH), BF16),
        grid=(batch, N_PAIRS, nb // 2),
        in_specs=[tiles_spec, col_spec, tiles_spec,
                  pl.BlockSpec((KV_BLOCK, KV_BLOCK), lambda b, p, i: (0, 0))],
        out_specs=col_spec,
        scratch_shapes=[pltpu.VMEM((2, 2, LANES, KV_BLOCK), BF16), pltpu.VMEM((2, LANES, KV_BLOCK), F32)],
        compiler_params=pltpu.CompilerParams(
            dimension_semantics=("arbitrary", "arbitrary", "arbitrary"), vmem_limit_bytes=VMEM_LIMIT),
        name="stickbreak_attn",
    )(qt, k, vt, tri)


def _moba_kernel(qt_ref, k_ref, vt_ref, avg_ref, o_ref, km_ref, bias_scr, q_scr, acc_scr, s_scr):
    i = pl.program_id(2)
    nb = qt_ref.shape[0]

    @pl.when(i == 0)
    def _():
        km_ref[...] = _dot(avg_ref[...], k_ref[...])

    tiles, chains = _chain_list(i, nb, lambda first, t: jnp.where(first, t, t - i))
    key = lax.broadcasted_iota(I32, (KV_BLOCK, KV_BLOCK), 0)
    qry = lax.broadcasted_iota(I32, (KV_BLOCK, KV_BLOCK), 1)
    causal = key <= qry
    blk = lax.broadcasted_iota(I32, (nb, KV_BLOCK), 0)
    km = km_ref[...].astype(BF16)
    for a in range(2):
        qt = qt_ref[tiles[a]]
        valid = blk < tiles[a]
        for h in range(2):
            qth = jnp.where(_head_rows(h), qt, jnp.zeros_like(qt))
            q_scr[a, h] = qth
            g = jnp.where(valid, _dot(km, qth), NEG_INF)
            rank = jnp.zeros((nb, KV_BLOCK), F32)
            for jp in range(nb):
                gj = g[jp:jp + 1, :]
                better = jnp.where(gj > g, 1.0, jnp.where(gj == g, jnp.where(blk > jp, 1.0, 0.0), 0.0))
                rank = rank + better
            bias_scr[a, h] = jnp.where(valid, jnp.where(rank < MOBA_TOPK, 0.0, NEG_INF), NEG_INF)

    for n, c in enumerate(chains):
        c["n"] = n
    low = jnp.full((SUBLANES, KV_BLOCK), NEG_INF, F32)
    maxima = [[low, low], [low, low]]
    sums = [[0.0, 0.0], [0.0, 0.0]]
    acc_scr[...] = jnp.zeros_like(acc_scr)

    def scores(c):
        kb = k_ref[pl.ds(pl.multiple_of(c["j"] * KV_BLOCK, KV_BLOCK), KV_BLOCK), :]
        c["s"] = _dot(kb, q_scr[c["a"], c["h"]])

    def mask(c):
        h, first = c["h"], c["first"]
        if first is None:
            s = jnp.where(causal, c.pop("s"), NEG_INF)
        else:
            s = c.pop("s") + bias_scr[c["a"], h, pl.ds(c["j"], 1), :]
        s_scr[c["n"]] = s
        top = jnp.max(s.reshape(KV_BLOCK // SUBLANES, SUBLANES, KV_BLOCK), axis=0)
        if first is None:
            maxima[c["a"]][h] = top
        else:
            maxima[0][h] = jnp.where(first, jnp.maximum(maxima[0][h], top), maxima[0][h])
            maxima[1][h] = jnp.where(first, maxima[1][h], jnp.maximum(maxima[1][h], top))

    _skewed(chains, ((scores, 0), (mask, 1)))
    maxima = [[jnp.max(maxima[a][h], axis=0, keepdims=True) for h in range(2)] for a in range(2)]

    def weights(c):
        h, first = c["h"], c["first"]
        m = maxima[c["a"]][h] if first is None else jnp.where(first, maxima[0][h], maxima[1][h])
        p = jnp.exp(s_scr[c["n"]] - m)
        l = jnp.sum(p, axis=0, keepdims=True)
        if first is None:
            sums[c["a"]][h] = sums[c["a"]][h] + l
        else:
            sums[0][h] = sums[0][h] + jnp.where(first, l, 0.0)
            sums[1][h] = sums[1][h] + jnp.where(first, 0.0, l)
        c["p"] = p.astype(BF16)

    def values(c):
        rows = slice(c["h"] * HEAD_DIM, (c["h"] + 1) * HEAD_DIM)
        acc_scr[c["a"], rows, :] += _dot(vt_ref[c["j"], rows, :], c.pop("p"))

    _skewed(chains, ((values, 1), (weights, 0)))
    for a in range(2):
        for h in range(2):
            rows = slice(h * HEAD_DIM, (h + 1) * HEAD_DIM)
            acc_scr[a, rows, :] = acc_scr[a, rows, :] / sums[a][h]
        o_ref[pl.ds(pl.multiple_of(tiles[a] * KV_BLOCK, KV_BLOCK), KV_BLOCK), :] = acc_scr[a].T.astype(BF16)


def _moba_call(qt, k, vt, avg, batch, seq):
    nb = seq // KV_BLOCK
    tiles_spec, col_spec = _tile_specs(seq)
    return pl.pallas_call(
        _moba_kernel,
        out_shape=jax.ShapeDtypeStruct((batch * seq, WIDTH), BF16),
        grid=(batch, N_PAIRS, nb // 2),
        in_specs=[tiles_spec, col_spec, tiles_spec, pl.BlockSpec((nb, seq), lambda b, p, i: (0, 0))],
        out_specs=col_spec,
        scratch_shapes=[pltpu.VMEM((nb, LANES), F32), pltpu.VMEM((2, 2, nb, KV_BLOCK), F32),
                        pltpu.VMEM((2, 2, LANES, KV_BLOCK), BF16), pltpu.VMEM((2, LANES, KV_BLOCK), F32),
                        pltpu.VMEM((2 * (nb + 1), KV_BLOCK, KV_BLOCK), F32)],
        compiler_params=pltpu.CompilerParams(
            dimension_semantics=("arbitrary", "arbitrary", "arbitrary"), vmem_limit_bytes=VMEM_LIMIT),
        name="moba_attn",
    )(qt, k, vt, avg)


def _layer_norm(r, g, b):
    mu = jnp.mean(r, axis=-1, keepdims=True)
    d = r - mu
    var = jnp.mean(d * d, axis=-1, keepdims=True)
    return d * lax.rsqrt(var + LN_EPS) * g + b


def _mixer_kernel(x_ref, asb_ref, amb_ref, wg_ref, bg_ref, wbs_ref, wbm_ref, wo_ref, lng_ref, lnb_ref,
                  wr_ref, br_ref, h_ref, idx_ref, cnt_ref, tw_ref, counts_ref, carry_ref, *, tm, d):
    @pl.when(pl.program_id(0) == 0)
    def _():
        carry_ref[...] = jnp.zeros_like(carry_ref)

    rows_per = MXU_ROWS
    chains = [dict(rows=slice(c * rows_per, (c + 1) * rows_per)) for c in range(tm // rows_per)]
    expert = lax.broadcasted_iota(I32, (N_EXPERTS, rows_per), 0)
    slot8 = lax.broadcasted_iota(I32, (8, rows_per), 0)
    slot128 = lax.broadcasted_iota(I32, (LANES, rows_per), 0)
    rt = lax.broadcasted_iota(I32, (rows_per, rows_per), 0)
    ct = lax.broadcasted_iota(I32, (rows_per, rows_per), 1)
    earlier = jnp.where(rt < ct, 1.0, 0.0).astype(BF16)

    def project(c):
        c["gp"] = _dot(x_ref[c["rows"], :].astype(BF16), wg_ref[...])
        c["ysb"] = _dot(asb_ref[c["rows"], :], wbs_ref[...])
        c["ymb"] = _dot(amb_ref[c["rows"], :], wbm_ref[...])

    def gate(c):
        g = jax.nn.sigmoid(c.pop("gp") + bg_ref[...])
        c["mixed"] = (g[:, :d] * c.pop("ysb") + g[:, d:] * c.pop("ymb")).astype(BF16)

    def out_proj(c):
        c["mix"] = _dot(c.pop("mixed"), wo_ref[...])

    def norm(c):
        h = _layer_norm(DEEPNORM_ALPHA * x_ref[c["rows"], :] + c.pop("mix"), lng_ref[...], lnb_ref[...])
        h_ref[c["rows"], :] = h
        c["hb"] = h.astype(BF16)

    def route(c):
        nt = (((1,), (1,)), ((), ()))
        c["logits"] = lax.dot_general(wr_ref[...], c.pop("hb"), nt, preferred_element_type=F32)

    def top_k(c):
        bias = br_ref[...]
        logits = c.pop("logits") + jnp.concatenate([bias] * (rows_per // LANES), axis=1)
        vals, idxs = [], []
        member = jnp.zeros((N_EXPERTS, rows_per), F32)
        for _ in range(TOP_K):
            mx = jnp.max(logits, axis=0, keepdims=True)
            ik = jnp.min(jnp.where(logits == mx, expert, N_EXPERTS), axis=0, keepdims=True)
            hit = expert == ik
            vals.append(mx)
            idxs.append(ik)
            member = jnp.where(hit, 1.0, member)
            logits = jnp.where(hit, NEG_INF, logits)
        es = [jnp.exp(v - vals[0]) for v in vals]
        den = es[0] + es[1] + es[2] + es[3]
        idx_out = jnp.zeros((8, rows_per), I32)
        tw_rows = jnp.zeros((LANES, rows_per), F32)
        for k in range(TOP_K):
            idx_out = jnp.where(slot8 == k, idxs[k], idx_out)
            tw_rows = jnp.where(slot128 == k, es[k] / den, tw_rows)
        idx_ref[:, c["rows"]] = idx_out
        tw_ref[c["rows"], :] = tw_rows.T[:, :8]
        c["idxs"] = idxs
        c["member"] = member

    def count(c):
        member = c.pop("member")
        carry = carry_ref[...]
        cnt = _dot(member.astype(BF16), earlier) + jnp.concatenate([carry] * (rows_per // LANES), axis=1)
        carry_ref[...] = carry + jnp.sum(member, axis=1, keepdims=True)
        cnt_out = jnp.zeros((8, rows_per), I32)
        for k, ik in enumerate(c.pop("idxs")):
            ck = jnp.sum(jnp.where(expert == ik, cnt, 0.0), axis=0, keepdims=True)
            cnt_out = jnp.where(slot8 == k, ck.astype(I32), cnt_out)
        cnt_ref[:, c["rows"]] = cnt_out

    _skewed(chains, ((project, 0), (out_proj, 2), (route, 4), (count, 6), (gate, 1), (norm, 3), (top_k, 5)))
    counts_ref[...] = carry_ref[...].astype(I32)


def _mixer_call(x2, a_sb, a_mb, wg, bg, wbs, wbm, wo, lng, lnb, wr, br, tm):
    t_tok, d = x2.shape
    const = lambda shape: pl.BlockSpec(shape, lambda i: (0,) * len(shape))
    col8 = pl.BlockSpec((tm, 8), lambda i: (i, 0))
    row8 = pl.BlockSpec((8, tm), lambda i: (0, i))
    return pl.pallas_call(
        functools.partial(_mixer_kernel, tm=tm, d=d),
        out_shape=(jax.ShapeDtypeStruct((t_tok, d), F32),
                   jax.ShapeDtypeStruct((8, t_tok), I32),
                   jax.ShapeDtypeStruct((8, t_tok), I32),
                   jax.ShapeDtypeStruct((t_tok, 8), F32),
                   jax.ShapeDtypeStruct((N_EXPERTS, LANES), I32)),
        grid=(t_tok // tm,),
        in_specs=[pl.BlockSpec((tm, d), lambda i: (i, 0)),
                  pl.BlockSpec((tm, WIDTH), lambda i: (i, 0)),
                  pl.BlockSpec((tm, WIDTH), lambda i: (i, 0)),
                  const((d, 2 * d)), const((1, 2 * d)),
                  const((WIDTH, d)), const((WIDTH, d)), const((d, d)),
                  const((1, d)), const((1, d)),
                  const((N_EXPERTS, d)), const((N_EXPERTS, LANES))],
        out_specs=(pl.BlockSpec((tm, d), lambda i: (i, 0)), row8, row8, col8, const((N_EXPERTS, LANES))),
        scratch_shapes=[pltpu.VMEM((N_EXPERTS, LANES), F32)],
        compiler_params=pltpu.CompilerParams(dimension_semantics=("arbitrary",),
                                             vmem_limit_bytes=VMEM_LIMIT),
        name="mixer_ln_router",
    )(x2, a_sb, a_mb, wg, bg, wbs, wbm, wo, lng, lnb, wr, br)


CTL_FIRST = 1
CTL_LAST = 2
CTL_ODD = 4


def _plan_kernel(counts_ref, idx_ref, cnt_ref, dest_ref, start_ref, be_ref, nx_ref, ctl_ref, nu_ref, follow_ref):
    def region(e, off):
        start_ref[e] = off
        return off + (counts_ref[e] + EXPERT_ROWS - 1) // EXPERT_ROWS * EXPERT_ROWS

    total = lax.fori_loop(0, N_EXPERTS, region, 0)
    nu_ref[0] = total // EXPERT_ROWS

    def nonempty_after(t, nxt):
        e = N_EXPERTS - 1 - t
        follow_ref[e] = nxt
        return jnp.where(counts_ref[e] > 0, e, nxt)

    lax.fori_loop(0, N_EXPERTS, nonempty_after, -1)

    def clear(b, _):
        be_ref[b] = 0
        nx_ref[b] = 0
        ctl_ref[b] = 0
        return 0

    lax.fori_loop(0, be_ref.shape[0], clear, 0)

    def blocks_of(e, state):
        b0, odd = state
        n = (counts_ref[e] + EXPERT_ROWS - 1) // EXPERT_ROWS
        follows = follow_ref[e] >= 0

        def block(j, _):
            be_ref[b0 + j] = e
            nx_ref[b0 + j] = jnp.maximum(follow_ref[e], 0)
            ctl_ref[b0 + j] = (jnp.where(follows & (j == 0), CTL_FIRST, 0)
                               + jnp.where(follows & (j == n - 1), CTL_LAST, 0) + odd * CTL_ODD)
            return 0

        lax.fori_loop(0, n, block, 0)
        return b0 + n, jnp.where(n > 0, 1 - odd, odd)

    lax.fori_loop(0, N_EXPERTS, blocks_of, (0, 0))

    idx = idx_ref[...]
    dest = cnt_ref[...]
    for e in range(N_EXPERTS):
        dest = dest + jnp.where(idx == e, start_ref[e], 0)
    dest_ref[...] = dest


def _plan_call(counts, idx_t, cnt_t, n_blocks):
    smem = pl.BlockSpec(memory_space=pltpu.SMEM)
    vmem = pl.BlockSpec(memory_space=pltpu.VMEM)
    blocks = jax.ShapeDtypeStruct((n_blocks,), I32)
    return pl.pallas_call(
        _plan_kernel,
        out_shape=(jax.ShapeDtypeStruct(idx_t.shape, I32), jax.ShapeDtypeStruct((N_EXPERTS,), I32),
                   blocks, blocks, blocks, jax.ShapeDtypeStruct((1,), I32)),
        in_specs=[smem, vmem, vmem],
        out_specs=(vmem, smem, smem, smem, smem, smem),
        scratch_shapes=[pltpu.SMEM((N_EXPERTS,), I32)],
        name="moe_plan",
    )(counts, idx_t, cnt_t)


def _row_copy(src_ref, s, dst_ref, t, sem):
    return pltpu.make_async_copy(src_ref.at[pl.ds(s, 1)], dst_ref.at[pl.ds(t, 1)], sem)


def _dispatch_kernel(*refs, tm, n_pad):
    dest_refs = refs[:TOP_K]
    start_ref, counts_ref, h_ref, xg_ref, zero_ref, sem, zsem = refs[TOP_K:]
    @pl.when(pl.program_id(0) == 0)
    def _():
        zero_ref[...] = jnp.zeros_like(zero_ref)

        def pad_expert(e, _):
            n = (EXPERT_ROWS - counts_ref[e] % EXPERT_ROWS) % EXPERT_ROWS
            off = start_ref[e] + counts_ref[e]
            for s in range(SUBLANES - 1):
                @pl.when(s < (n & (SUBLANES - 1)))
                def _(row=off + s):
                    _row_copy(zero_ref, 0, xg_ref, row, zsem).start()
            off = off + (n & (SUBLANES - 1))
            size = SUBLANES
            while size < EXPERT_ROWS:
                @pl.when((n & size) != 0)
                def _(off=off, size=size):
                    dst = xg_ref.at[pl.ds(pl.multiple_of(off, SUBLANES), size)]
                    pltpu.make_async_copy(zero_ref.at[pl.ds(0, size)], dst, zsem).start()
                off = off + (n & size)
                size *= 2
            return off

        used = lax.fori_loop(0, N_EXPERTS, pad_expert, 0)

        def pad_block(b, _):
            first = pl.multiple_of(used + b * EXPERT_ROWS, EXPERT_ROWS)
            pltpu.make_async_copy(zero_ref, xg_ref.at[pl.ds(first, EXPERT_ROWS)], zsem).start()
            return 0

        lax.fori_loop(0, (xg_ref.shape[0] - used) // EXPERT_ROWS, pad_block, 0)
        pltpu.make_async_copy(xg_ref.at[pl.ds(0, n_pad)], xg_ref.at[pl.ds(0, n_pad)], zsem).wait()

    def issue(t, _):
        for k in range(TOP_K):
            _row_copy(h_ref, t, xg_ref, dest_refs[k][t], sem).start()
        return 0

    lax.fori_loop(0, tm, issue, 0, unroll=2)
    pltpu.make_async_copy(xg_ref.at[pl.ds(0, tm * TOP_K)], xg_ref.at[pl.ds(0, tm * TOP_K)], sem).wait()


def _dispatch_call(dest_rows, start, counts, h, n_rows, tm):
    t_tok, d = h.shape
    smem_blk = pl.BlockSpec((tm,), lambda i: (i,), memory_space=pltpu.SMEM)
    smem = pl.BlockSpec(memory_space=pltpu.SMEM)
    return pl.pallas_call(
        functools.partial(_dispatch_kernel, tm=tm, n_pad=n_rows - t_tok * TOP_K),
        out_shape=jax.ShapeDtypeStruct((n_rows, d), F32),
        grid=(t_tok // tm,),
        in_specs=[smem_blk] * TOP_K + [smem, smem, pl.BlockSpec((tm, d), lambda i: (i, 0))],
        out_specs=pl.BlockSpec(memory_space=pl.ANY),
        scratch_shapes=[pltpu.VMEM((EXPERT_ROWS, d), F32), pltpu.SemaphoreType.DMA(()),
                        pltpu.SemaphoreType.DMA(())],
        compiler_params=pltpu.CompilerParams(dimension_semantics=("arbitrary",),
                                             vmem_limit_bytes=VMEM_LIMIT),
        name="moe_dispatch",
    )(*dest_rows, start, counts, h)


def _expert_kernel(be_ref, nx_ref, ctl_ref, nu_ref, x_ref, wgu_hbm, wd_hbm, bg_ref, bu_ref, bd_ref, o_ref,
                   wgu_stage, wd_stage, wgt0, wut0, wdb0, wgt1, wut1, wdb1, tmp_ref, sems):
    i = pl.program_id(0)
    live = i < nu_ref[0]
    ctl = ctl_ref[i]
    sets = ((wgt0, wut0, wdb0), (wgt1, wut1, wdb1))
    chunk = 2 * LANES

    def fetch(e):
        return (pltpu.make_async_copy(wgu_hbm.at[e], wgu_stage, sems.at[0]),
                pltpu.make_async_copy(wd_hbm.at[e], wd_stage, sems.at[1]))

    def prepare(wgt_ref, wut_ref, wdb_ref):
        for c in range(wgu_stage.shape[1] // chunk):
            rows = slice(c * LANES, (c + 1) * LANES)
            for k in range(wgu_stage.shape[0] // LANES):
                cols = slice(k * LANES, (k + 1) * LANES)
                slot = (c * (wgu_stage.shape[0] // LANES) + k) % tmp_ref.shape[0]
                tmp_ref[slot] = wgu_stage[cols, c * chunk:(c + 1) * chunk].T
                wgt_ref[rows, cols] = tmp_ref[slot, pl.ds(0, LANES, stride=2), :].astype(BF16)
                wut_ref[rows, cols] = tmp_ref[slot, pl.ds(1, LANES, stride=2), :].astype(BF16)
        wdb_ref[...] = wd_stage[...].astype(BF16)

    def ffn(wgt_ref, wut_ref, wdb_ref):
        nt = (((1,), (1,)), ((), ()))
        chains = [dict(rows=slice(c * MXU_ROWS, (c + 1) * MXU_ROWS)) for c in range(EXPERT_ROWS // MXU_ROWS)]

        def gate_up(c):
            x = x_ref[c["rows"], :].astype(BF16)
            c["gate"] = lax.dot_general(x, wgt_ref[...], nt, preferred_element_type=F32)
            c["up"] = lax.dot_general(x, wut_ref[...], nt, preferred_element_type=F32)

        def activation(c):
            gate = jnp.minimum(c.pop("gate") + bg_ref[...], SWIGLU_LIMIT)
            up = jnp.clip(c.pop("up") + bu_ref[...], -SWIGLU_LIMIT, SWIGLU_LIMIT)
            c["act"] = ((up + 1.0) * gate * jax.nn.sigmoid(SWIGLU_ALPHA * gate)).astype(BF16)

        def down(c):
            o_ref[c["rows"], :] = _dot(c.pop("act"), wdb_ref[...]) + bd_ref[...]

        _skewed(chains, ((gate_up, 0), (down, 2), (activation, 1)))

    @pl.when(i == 0)
    def _():
        for cp in fetch(be_ref[0]):
            cp.start()
        for cp in fetch(be_ref[0]):
            cp.wait()
        prepare(*sets[0])

    @pl.when(live & ((ctl & CTL_FIRST) != 0))
    def _():
        for cp in fetch(nx_ref[i]):
            cp.start()

    for odd in range(2):
        mine = live & (((ctl & CTL_ODD) != 0) == bool(odd))

        @pl.when(mine & ((ctl & CTL_LAST) == 0))
        def _(odd=odd):
            ffn(*sets[odd])

        @pl.when(mine & ((ctl & CTL_LAST) != 0))
        def _(odd=odd):
            for cp in fetch(nx_ref[i]):
                cp.wait()
            ffn(*sets[odd])
            prepare(*sets[1 - odd])

    @pl.when(jnp.logical_not(live))
    def _():
        o_ref[...] = jnp.zeros_like(o_ref)


def _expert_call(block_expert, next_expert, ctl, n_used, xg, w_gate_up, w_down, bg, bu, bd):
    n_rows, d = xg.shape
    f = w_down.shape[1]
    n_blocks = n_rows // EXPERT_ROWS
    blk = lambda i, be, nx, ctl, nu: (jnp.minimum(i, nu[0] - 1), 0)
    bsel = lambda i, be, nx, ctl, nu: (be[i], 0, 0)
    prepared = [pltpu.VMEM((f, d), BF16) for _ in range(6)]
    return pl.pallas_call(
        _expert_kernel,
        out_shape=jax.ShapeDtypeStruct((n_rows, d), F32),
        grid_spec=pltpu.PrefetchScalarGridSpec(
            num_scalar_prefetch=4,
            grid=(n_blocks,),
            in_specs=[pl.BlockSpec((EXPERT_ROWS, d), blk),
                      pl.BlockSpec(memory_space=pl.ANY),
                      pl.BlockSpec(memory_space=pl.ANY),
                      pl.BlockSpec((None, 1, f), bsel),
                      pl.BlockSpec((None, 1, f), bsel),
                      pl.BlockSpec((None, 1, d), bsel)],
            out_specs=pl.BlockSpec((EXPERT_ROWS, d), lambda i, be, nx, ctl, nu: (i, 0)),
            scratch_shapes=[pltpu.VMEM((d, 2 * f), F32), pltpu.VMEM((f, d), F32), *prepared,
                            pltpu.VMEM((4, 2 * LANES, LANES), F32), pltpu.SemaphoreType.DMA((2,))]),
        compiler_params=pltpu.CompilerParams(dimension_semantics=("arbitrary",),
                                             vmem_limit_bytes=VMEM_LIMIT),
        name="expert_ffn",
    )(block_expert, next_expert, ctl, n_used, xg, w_gate_up, w_down, bg, bu, bd)


def _combine_kernel(*refs, tm):
    dest_refs, next_refs = refs[:TOP_K], refs[TOP_K:2 * TOP_K]
    h_ref, tw_ref, lng_ref, lnb_ref, rows_ref, o_ref, buf_ref, sems = refs[2 * TOP_K:]
    i = pl.program_id(0)
    slot = i % 2

    def gather(rows_of, into):
        def issue(t, _):
            for k in range(TOP_K):
                _row_copy(rows_ref, rows_of[k][t], buf_ref.at[into, k], t, sems.at[into]).start()
            return 0
        lax.fori_loop(0, tm, issue, 0, unroll=2)

    @pl.when(i == 0)
    def _():
        gather(dest_refs, 0)

    @pl.when(i + 1 < pl.num_programs(0))
    def _():
        gather(next_refs, 1 - slot)

    pltpu.make_async_copy(buf_ref.at[slot], buf_ref.at[slot], sems.at[slot]).wait()
    tw = tw_ref[...]
    y = tw[:, 0:1] * buf_ref[slot, 0]
    for k in range(1, TOP_K):
        y = y + tw[:, k:k + 1] * buf_ref[slot, k]
    o_ref[...] = _layer_norm(DEEPNORM_ALPHA * h_ref[...] + y, lng_ref[...], lnb_ref[...])


def _combine_call(dest_rows, h, tw, lng, lnb, rows, tm):
    t_tok, d = h.shape
    n_tiles = t_tok // tm
    return pl.pallas_call(
        functools.partial(_combine_kernel, tm=tm),
        out_shape=jax.ShapeDtypeStruct((t_tok, d), F32),
        grid=(n_tiles,),
        in_specs=[pl.BlockSpec((tm,), lambda i: (i,), memory_space=pltpu.SMEM)] * TOP_K
        + [pl.BlockSpec((tm,), lambda i: (jnp.minimum(i + 1, n_tiles - 1),), memory_space=pltpu.SMEM)] * TOP_K
        + [
                  pl.BlockSpec((tm, d), lambda i: (i, 0)),
                  pl.BlockSpec((tm, 8), lambda i: (i, 0)),
                  pl.BlockSpec((1, d), lambda i: (0, 0)),
                  pl.BlockSpec((1, d), lambda i: (0, 0)),
                  pl.BlockSpec(memory_space=pl.ANY)],
        out_specs=pl.BlockSpec((tm, d), lambda i: (i, 0)),
        scratch_shapes=[pltpu.VMEM((2, TOP_K, tm, d), F32), pltpu.SemaphoreType.DMA((2,))],
        compiler_params=pltpu.CompilerParams(dimension_semantics=("arbitrary",),
                                             vmem_limit_bytes=VMEM_LIMIT),
        name="moe_combine_ln",
    )(*dest_rows, *dest_rows, h, tw, lng, lnb, rows)


def _layer(x2, pos, batch, seq, w_in, b_gate, w_branch_sb, w_branch_moba, w_out, ln_mix_g, ln_mix_b,
           w_router, b_router, w_gate_up, b_gate_up, w_down, b_down, ln_ffn_g, ln_ffn_b):
    t_tok, d = x2.shape
    nb = seq // KV_BLOCK
    qkv_w = 6 * WIDTH
    tm_a = 512 if t_tok % 512 == 0 else KV_BLOCK
    tm_mix = 1024 if t_tok % 1024 == 0 else tm_a

    w_qkv = w_in[:, :qkv_w].astype(BF16)
    w_g = w_in[:, qkv_w:].astype(BF16)
    pos3 = pos.astype(F32).reshape(t_tok // tm_a, 1, tm_a)
    invf = (ROPE_THETA ** (-jnp.arange(ROPE_HALF, dtype=F32) / ROPE_HALF)).reshape(ROPE_HALF, 1)
    ki = jnp.arange(KV_BLOCK)
    tri = jnp.where(ki[None, :] > ki[:, None], -1.0, 0.0).astype(BF16)
    avg = jnp.where(jnp.arange(seq)[None, :] // KV_BLOCK == jnp.arange(nb)[:, None],
                    1.0 / KV_BLOCK, 0.0).astype(BF16)

    qt_sb, k_sb, vt_sb, qt_mb, k_mb, vt_mb = _qkv_call(x2, pos3, invf, w_qkv, tm_a)
    a_sb = _sb_call(qt_sb, k_sb, vt_sb, tri, batch, seq)
    a_mb = _moba_call(qt_mb, k_mb, vt_mb, avg, batch, seq)

    w_r = w_router.T.astype(BF16)
    b_r = jnp.broadcast_to(b_router[:, None], (N_EXPERTS, LANES))
    h, idx_t, cnt_t, tw8, counts = _mixer_call(
        x2, a_sb, a_mb, w_g, b_gate.reshape(1, -1), w_branch_sb.astype(BF16), w_branch_moba.astype(BF16),
        w_out.astype(BF16), ln_mix_g.reshape(1, -1), ln_mix_b.reshape(1, -1), w_r, b_r, tm_mix)

    n_assign = t_tok * TOP_K
    n_blocks = -(-(n_assign + N_EXPERTS * (EXPERT_ROWS - 1)) // EXPERT_ROWS)
    counts = counts[:, 0]
    dest_t, start, block_expert, next_expert, ctl, n_used = _plan_call(counts, idx_t, cnt_t, n_blocks)
    dest_rows = [dest_t[k] for k in range(TOP_K)]
    xg = _dispatch_call(dest_rows, start, counts, h, n_blocks * EXPERT_ROWS, tm_a)

    f = w_down.shape[1]
    rows = _expert_call(block_expert, next_expert, ctl, n_used, xg, w_gate_up, w_down,
                        b_gate_up[:, 0::2].reshape(N_EXPERTS, 1, f), b_gate_up[:, 1::2].reshape(N_EXPERTS, 1, f),
                        b_down.reshape(N_EXPERTS, 1, d))
    return _combine_call(dest_rows, h, tw8, ln_ffn_g.reshape(1, -1), ln_ffn_b.reshape(1, -1), rows, KV_BLOCK)


def kernel(x, positions, w_in, b_gate, w_branch_sb, w_branch_moba, w_out, ln_mix_g, ln_mix_b, w_router,
           b_router, w_gate_up, b_gate_up, w_down, b_down, ln_ffn_g, ln_ffn_b):
    batch, seq, d = x.shape
    h = x.reshape(batch * seq, d)
    pos = positions.reshape(batch * seq)
    for layer in range(w_in.shape[0]):
        h = _layer(h, pos, batch, seq, w_in[layer], b_gate[layer], w_branch_sb[layer], w_branch_moba[layer],
                   w_out[layer], ln_mix_g[layer], ln_mix_b[layer], w_router[layer], b_router[layer],
                   w_gate_up[layer], b_gate_up[layer], w_down[layer], b_down[layer],
                   ln_ffn_g[layer], ln_ffn_b[layer])
    return h.reshape(batch, seq, d)
```

```python
import functools

import jax
import jax.numpy as jnp
from jax import lax
from jax.experimental import pallas as pl
from jax.experimental.pallas import tpu as pltpu

F32 = jnp.float32
BF16 = jnp.bfloat16
I32 = jnp.int32

HEAD_DIM = 64
N_HEADS = 8
WIDTH = N_HEADS * HEAD_DIM
LANES = 128
SUBLANES = 8
N_PAIRS = WIDTH // LANES
KV_BLOCK = 256
MOBA_TOPK = 3
ROPE_THETA = 500000.0
ROPE_HALF = 8
N_EXPERTS = 32
TOP_K = 4
SWIGLU_LIMIT = 7.0
SWIGLU_ALPHA = 1.702
EXPERT_ROWS = 512
MXU_ROWS = 256
LN_EPS = 1e-5
DEPTH = 1
DEEPNORM_ALPHA = (2 * DEPTH) ** 0.25
QK_SCALE = HEAD_DIM ** -0.5
NEG_INF = float("-inf")
VMEM_LIMIT = 56 * 1024 * 1024


def _dot(a, b):
    return jnp.dot(a, b, preferred_element_type=F32)


def _qkv_kernel(x_ref, pos_ref, invf_ref, w_ref,
                qt_sb_ref, k_sb_ref, vt_sb_ref, qt_mb_ref, k_mb_ref, vt_mb_ref, *, tm):
    xb = x_ref[...].astype(BF16)
    ang = invf_ref[...] * pos_ref[0]
    cos, sin = jnp.cos(ang), jnp.sin(ang)

    def rope_t(t):
        parts = []
        for base in (0, HEAD_DIM):
            x1 = t[base:base + ROPE_HALF]
            x2 = t[base + ROPE_HALF:base + 2 * ROPE_HALF]
            parts += [x1 * cos - x2 * sin, x2 * cos + x1 * sin, t[base + 2 * ROPE_HALF:base + HEAD_DIM]]
        return jnp.concatenate(parts, axis=0)

    def store_t(ref, p, t):
        tb = t.astype(BF16)
        for blk in range(tm // KV_BLOCK):
            ref[blk, p] = tb[:, blk * KV_BLOCK:(blk + 1) * KV_BLOCK]

    for sec in range(6):
        for half in range(2):
            c0 = sec * WIDTH + half * 2 * LANES
            r = _dot(xb, w_ref[:, c0:c0 + 2 * LANES])
            for q in range(2):
                p = half * 2 + q
                t = r[:, q * LANES:(q + 1) * LANES]
                if sec == 0:
                    store_t(qt_sb_ref, p, (t * QK_SCALE).T)
                elif sec == 1:
                    k_sb_ref[:, p * LANES:(p + 1) * LANES] = t.astype(BF16)
                elif sec == 2:
                    store_t(vt_sb_ref, p, t.T)
                elif sec == 3:
                    store_t(qt_mb_ref, p, rope_t(t.T) * QK_SCALE)
                elif sec == 4:
                    k_mb_ref[:, p * LANES:(p + 1) * LANES] = rope_t(t.T).T.astype(BF16)
                else:
                    store_t(vt_mb_ref, p, t.T)


def _qkv_call(x2, pos3, invf, w_qkv, tm):
    t_tok, d = x2.shape
    nblk = t_tok // KV_BLOCK
    bpt = tm // KV_BLOCK
    t_shape = jax.ShapeDtypeStruct((nblk, N_PAIRS, LANES, KV_BLOCK), BF16)
    n_shape = jax.ShapeDtypeStruct((t_tok, WIDTH), BF16)
    t_spec = pl.BlockSpec((bpt, N_PAIRS, LANES, KV_BLOCK), lambda i: (i, 0, 0, 0))
    n_spec = pl.BlockSpec((tm, WIDTH), lambda i: (i, 0))
    return pl.pallas_call(
        functools.partial(_qkv_kernel, tm=tm),
        out_shape=(t_shape, n_shape, t_shape, t_shape, n_shape, t_shape),
        grid=(t_tok // tm,),
        in_specs=[pl.BlockSpec((tm, d), lambda i: (i, 0)),
                  pl.BlockSpec((1, 1, tm), lambda i: (i, 0, 0)),
                  pl.BlockSpec((ROPE_HALF, 1), lambda i: (0, 0)),
                  pl.BlockSpec((d, 6 * WIDTH), lambda i: (0, 0))],
        out_specs=(t_spec, n_spec, t_spec, t_spec, n_spec, t_spec),
        compiler_params=pltpu.CompilerParams(dimension_semantics=("arbitrary",),
                                             vmem_limit_bytes=VMEM_LIMIT),
        name="qkv_proj",
    )(x2, pos3, invf, w_qkv)


def _head_rows(h):
    row = lax.broadcasted_iota(I32, (LANES, KV_BLOCK), 0)
    return (row >= HEAD_DIM * h) & (row < HEAD_DIM * (h + 1))


def _tile_specs(seq):
    nb = seq // KV_BLOCK
    tiles_spec = pl.BlockSpec((nb, None, LANES, KV_BLOCK), lambda b, p, i: (b, p, 0, 0))
    col_spec = pl.BlockSpec((seq, LANES), lambda b, p, i: (b, p))
    return tiles_spec, col_spec


def _chain_list(i, nb, past_block):
    tiles = (i, nb - 1 - i)
    chains = [dict(j=tiles[a], a=a, h=h, diagonal=True, first=None) for a in range(2) for h in range(2)]
    for t in range(nb - 1):
        first = t < i
        for h in range(2):
            chains.append(dict(j=past_block(first, t), a=jnp.where(first, 0, 1), h=h, diagonal=False, first=first))
    return tiles, chains


def _skewed(chains, stages):
    n = len(chains)
    for slot in range(n + max(lag for _, lag in stages)):
        for stage, lag in stages:
            if 0 <= slot - lag < n:
                stage(chains[slot - lag])


def _sb_kernel(qt_ref, k_ref, vt_ref, tri_ref, o_ref, q_scr, acc_scr):
    i = pl.program_id(2)
    nb = qt_ref.shape[0]
    tiles, chains = _chain_list(i, nb, lambda first, t: jnp.where(first, i - 1 - t, nb - 2 - t))
    for a in range(2):
        qt = qt_ref[tiles[a]]
        for h in range(2):
            q_scr[a, h] = jnp.where(_head_rows(h), qt, jnp.zeros_like(qt))
    acc_scr[...] = jnp.zeros_like(acc_scr)
    key = lax.broadcasted_iota(I32, (KV_BLOCK, KV_BLOCK), 0)
    qry = lax.broadcasted_iota(I32, (KV_BLOCK, KV_BLOCK), 1)
    past = key < qry
    tri = tri_ref[...]

    zero = jnp.zeros((1, KV_BLOCK), F32)
    carries = [[zero, zero], [zero, zero]]

    def scores(c):
        kb = k_ref[pl.ds(pl.multiple_of(c["j"] * KV_BLOCK, KV_BLOCK), KV_BLOCK), :]
        c["z"] = _dot(kb, q_scr[c["a"], c["h"]])

    def softplus(c):
        z = c.pop("z")
        sp = jnp.maximum(z, 0.0) + jnp.log(1.0 + jnp.exp(-jnp.abs(z)))
        spm = jnp.where(past, sp, 0.0) if c["diagonal"] else sp
        c["log_beta"] = z - sp
        c["spm"] = spm.astype(BF16)
        c["sp0"] = spm[0:1, :]

    def suffix(c):
        c["after"] = _dot(tri, c.pop("spm"))

    def weights(c):
        after = c.pop("after")
        w = jnp.exp(c.pop("log_beta") + after)
        if c["diagonal"]:
            w = jnp.where(past, w, 0.0)
        c["w"] = w.astype(BF16)
        total = after[0:1, :] - c.pop("sp0")
        h, first = c["h"], c["first"]
        if first is None:
            c["carry"] = zero
            carries[c["a"]][h] = total
        else:
            c["carry"] = jnp.where(first, carries[0][h], carries[1][h])
            cout = c["carry"] + total
            carries[0][h] = jnp.where(first, cout, carries[0][h])
            carries[1][h] = jnp.where(first, carries[1][h], cout)

    def values(c):
        rows = slice(c["h"] * HEAD_DIM, (c["h"] + 1) * HEAD_DIM)
        pv = _dot(vt_ref[c["j"], rows, :], c.pop("w")) * jnp.exp(c.pop("carry"))
        acc_scr[c["a"], rows, :] += pv

    _skewed(chains, ((scores, 0), (suffix, 2), (values, 4), (softplus, 1), (weights, 3)))
    for a in range(2):
        o_ref[pl.ds(pl.multiple_of(tiles[a] * KV_BLOCK, KV_BLOCK), KV_BLOCK), :] = acc_scr[a].T.astype(BF16)


def _sb_call(qt, k, vt, tri, batch, seq):
    nb = seq // KV_BLOCK
    tiles_spec, col_spec = _tile_specs(seq)
    return pl.pallas_call(
        _sb_kernel,
        out_shape=jax.ShapeDtypeStruct((batch * seq, WIDTH), BF16),
        grid=(batch, N_PAIRS, nb // 2),
        in_specs=[tiles_spec, col_spec, tiles_spec,
                  pl.BlockSpec((KV_BLOCK, KV_BLOCK), lambda b, p, i: (0, 0))],
        out_specs=col_spec,
        scratch_shapes=[pltpu.VMEM((2, 2, LANES, KV_BLOCK), BF16), pltpu.VMEM((2, LANES, KV_BLOCK), F32)],
        compiler_params=pltpu.CompilerParams(
            dimension_semantics=("arbitrary", "arbitrary", "arbitrary"), vmem_limit_bytes=VMEM_LIMIT),
        name="stickbreak_attn",
    )(qt, k, vt, tri)


def _moba_kernel(qt_ref, k_ref, vt_ref, avg_ref, o_ref, km_ref, bias_scr, q_scr, acc_scr, s_scr, den_scr):
    i = pl.program_id(2)
    nb = qt_ref.shape[0]

    @pl.when(i == 0)
    def _():
        km_ref[...] = _dot(avg_ref[...], k_ref[...])

    tiles, chains = _chain_list(i, nb, lambda first, t: jnp.where(first, t, t - i))
    key = lax.broadcasted_iota(I32, (KV_BLOCK, KV_BLOCK), 0)
    qry = lax.broadcasted_iota(I32, (KV_BLOCK, KV_BLOCK), 1)
    causal = key <= qry
    blk = lax.broadcasted_iota(I32, (nb, KV_BLOCK), 0)
    km = km_ref[...].astype(BF16)
    for a in range(2):
        qt = qt_ref[tiles[a]]
        valid = blk < tiles[a]
        for h in range(2):
            qth = jnp.where(_head_rows(h), qt, jnp.zeros_like(qt))
            q_scr[a, h] = qth
            g = jnp.where(valid, _dot(km, qth), NEG_INF)
            rank = jnp.zeros((nb, KV_BLOCK), F32)
            for jp in range(nb):
                gj = g[jp:jp + 1, :]
                better = jnp.where(gj > g, 1.0, jnp.where(gj == g, jnp.where(blk > jp, 1.0, 0.0), 0.0))
                rank = rank + better
            bias_scr[a, h] = jnp.where(valid, jnp.where(rank < MOBA_TOPK, 0.0, NEG_INF), NEG_INF)

    for n, c in enumerate(chains):
        c["n"] = n
    low = jnp.full((SUBLANES, KV_BLOCK), NEG_INF, F32)
    maxima = [[low, low], [low, low]]
    acc_scr[...] = jnp.zeros_like(acc_scr)
    den_scr[...] = jnp.zeros_like(den_scr)
    ones = jnp.ones((2 * SUBLANES, KV_BLOCK), BF16)

    def scores(c):
        kb = k_ref[pl.ds(pl.multiple_of(c["j"] * KV_BLOCK, KV_BLOCK), KV_BLOCK), :]
        c["s"] = _dot(kb, q_scr[c["a"], c["h"]])

    def mask(c):
        h, first = c["h"], c["first"]
        if first is None:
            s = jnp.where(causal, c.pop("s"), NEG_INF)
        else:
            s = c.pop("s") + bias_scr[c["a"], h, pl.ds(c["j"], 1), :]
        s_scr[c["n"]] = s
        top = jnp.max(s.reshape(KV_BLOCK // SUBLANES, SUBLANES, KV_BLOCK), axis=0)
        if first is None:
            maxima[c["a"]][h] = top
        else:
            maxima[0][h] = jnp.where(first, jnp.maximum(maxima[0][h], top), maxima[0][h])
            maxima[1][h] = jnp.where(first, maxima[1][h], jnp.maximum(maxima[1][h], top))

    _skewed(chains, ((scores, 0), (mask, 1)))
    maxima = [[jnp.max(maxima[a][h], axis=0, keepdims=True) for h in range(2)] for a in range(2)]

    def weights(c):
        h, first = c["h"], c["first"]
        m = maxima[c["a"]][h] if first is None else jnp.where(first, maxima[0][h], maxima[1][h])
        c["p"] = jnp.exp(s_scr[c["n"]] - m).astype(BF16)

    def values(c):
        rows = slice(c["h"] * HEAD_DIM, (c["h"] + 1) * HEAD_DIM)
        pv = _dot(jnp.concatenate([vt_ref[c["j"], rows, :], ones], axis=0), c.pop("p"))
        acc_scr[c["a"], rows, :] += pv[:HEAD_DIM]
        den_scr[c["a"], c["h"]] += pv[HEAD_DIM:]

    _skewed(chains, ((values, 1), (weights, 0)))
    for a in range(2):
        for h in range(2):
            rows = slice(h * HEAD_DIM, (h + 1) * HEAD_DIM)
            acc_scr[a, rows, :] = acc_scr[a, rows, :] / den_scr[a, h, 0:1, :]
        o_ref[pl.ds(pl.multiple_of(tiles[a] * KV_BLOCK, KV_BLOCK), KV_BLOCK), :] = acc_scr[a].T.astype(BF16)


def _moba_call(qt, k, vt, avg, batch, seq):
    nb = seq // KV_BLOCK
    tiles_spec, col_spec = _tile_specs(seq)
    return pl.pallas_call(
        _moba_kernel,
        out_shape=jax.ShapeDtypeStruct((batch * seq, WIDTH), BF16),
        grid=(batch, N_PAIRS, nb // 2),
        in_specs=[tiles_spec, col_spec, tiles_spec, pl.BlockSpec((nb, seq), lambda b, p, i: (0, 0))],
        out_specs=col_spec,
        scratch_shapes=[pltpu.VMEM((nb, LANES), F32), pltpu.VMEM((2, 2, nb, KV_BLOCK), F32),
                        pltpu.VMEM((2, 2, LANES, KV_BLOCK), BF16), pltpu.VMEM((2, LANES, KV_BLOCK), F32),
                        pltpu.VMEM((2 * (nb + 1), KV_BLOCK, KV_BLOCK), F32),
                        pltpu.VMEM((2, 2, 2 * SUBLANES, KV_BLOCK), F32)],
        compiler_params=pltpu.CompilerParams(
            dimension_semantics=("arbitrary", "arbitrary", "arbitrary"), vmem_limit_bytes=VMEM_LIMIT),
        name="moba_attn",
    )(qt, k, vt, avg)


def _layer_norm(r, g, b):
    mu = jnp.mean(r, axis=-1, keepdims=True)
    d = r - mu
    var = jnp.mean(d * d, axis=-1, keepdims=True)
    return d * lax.rsqrt(var + LN_EPS) * g + b


def _mixer_kernel(x_ref, asb_ref, amb_ref, wg_ref, bg_ref, wbs_ref, wbm_ref, wo_ref, lng_ref, lnb_ref,
                  wr_ref, br_ref, h_ref, idx_ref, cnt_ref, tw_ref, counts_ref, carry_ref, *, tm, d):
    @pl.when(pl.program_id(0) == 0)
    def _():
        carry_ref[...] = jnp.zeros_like(carry_ref)

    rows_per = MXU_ROWS
    chains = [dict(rows=slice(c * rows_per, (c + 1) * rows_per)) for c in range(tm // rows_per)]
    expert = lax.broadcasted_iota(I32, (N_EXPERTS, rows_per), 0)
    slot8 = lax.broadcasted_iota(I32, (8, rows_per), 0)
    slot128 = lax.broadcasted_iota(I32, (LANES, rows_per), 0)
    rt = lax.broadcasted_iota(I32, (rows_per, rows_per), 0)
    ct = lax.broadcasted_iota(I32, (rows_per, rows_per), 1)
    earlier = jnp.where(rt < ct, 1.0, 0.0).astype(BF16)

    def project(c):
        c["gp"] = _dot(x_ref[c["rows"], :].astype(BF16), wg_ref[...])
        c["ysb"] = _dot(asb_ref[c["rows"], :], wbs_ref[...])
        c["ymb"] = _dot(amb_ref[c["rows"], :], wbm_ref[...])

    def gate(c):
        g = jax.nn.sigmoid(c.pop("gp") + bg_ref[...])
        c["mixed"] = (g[:, :d] * c.pop("ysb") + g[:, d:] * c.pop("ymb")).astype(BF16)

    def out_proj(c):
        c["mix"] = _dot(c.pop("mixed"), wo_ref[...])

    def norm(c):
        h = _layer_norm(DEEPNORM_ALPHA * x_ref[c["rows"], :] + c.pop("mix"), lng_ref[...], lnb_ref[...])
        h_ref[c["rows"], :] = h
        c["hb"] = h.astype(BF16)

    def route(c):
        nt = (((1,), (1,)), ((), ()))
        c["logits"] = lax.dot_general(wr_ref[...], c.pop("hb"), nt, preferred_element_type=F32)

    def top_k(c):
        bias = br_ref[...]
        logits = c.pop("logits") + jnp.concatenate([bias] * (rows_per // LANES), axis=1)
        vals, idxs = [], []
        member = jnp.zeros((N_EXPERTS, rows_per), F32)
        for _ in range(TOP_K):
            mx = jnp.max(logits, axis=0, keepdims=True)
            ik = jnp.min(jnp.where(logits == mx, expert, N_EXPERTS), axis=0, keepdims=True)
            hit = expert == ik
            vals.append(mx)
            idxs.append(ik)
            member = jnp.where(hit, 1.0, member)
            logits = jnp.where(hit, NEG_INF, logits)
        es = [jnp.exp(v - vals[0]) for v in vals]
        den = es[0] + es[1] + es[2] + es[3]
        idx_out = jnp.zeros((8, rows_per), I32)
        tw_rows = jnp.zeros((LANES, rows_per), F32)
        for k in range(TOP_K):
            idx_out = jnp.where(slot8 == k, idxs[k], idx_out)
            tw_rows = jnp.where(slot128 == k, es[k] / den, tw_rows)
        idx_ref[:, c["rows"]] = idx_out
        tw_ref[c["rows"], :] = tw_rows.T[:, :8]
        c["idxs"] = idxs
        c["member"] = member

    def count(c):
        member = c.pop("member")
        carry = carry_ref[...]
        cnt = _dot(member.astype(BF16), earlier) + jnp.concatenate([carry] * (rows_per // LANES), axis=1)
        carry_ref[...] = carry + jnp.sum(member, axis=1, keepdims=True)
        cnt_out = jnp.zeros((8, rows_per), I32)
        for k, ik in enumerate(c.pop("idxs")):
            ck = jnp.sum(jnp.where(expert == ik, cnt, 0.0), axis=0, keepdims=True)
            cnt_out = jnp.where(slot8 == k, ck.astype(I32), cnt_out)
        cnt_ref[:, c["rows"]] = cnt_out

    _skewed(chains, ((project, 0), (out_proj, 2), (route, 4), (count, 6), (gate, 1), (norm, 3), (top_k, 5)))
    counts_ref[...] = carry_ref[...].astype(I32)


def _mixer_call(x2, a_sb, a_mb, wg, bg, wbs, wbm, wo, lng, lnb, wr, br, tm):
    t_tok, d = x2.shape
    const = lambda shape: pl.BlockSpec(shape, lambda i: (0,) * len(shape))
    col8 = pl.BlockSpec((tm, 8), lambda i: (i, 0))
    row8 = pl.BlockSpec((8, tm), lambda i: (0, i))
    return pl.pallas_call(
        functools.partial(_mixer_kernel, tm=tm, d=d),
        out_shape=(jax.ShapeDtypeStruct((t_tok, d), F32),
                   jax.ShapeDtypeStruct((8, t_tok), I32),
                   jax.ShapeDtypeStruct((8, t_tok), I32),
                   jax.ShapeDtypeStruct((t_tok, 8), F32),
                   jax.ShapeDtypeStruct((N_EXPERTS, LANES), I32)),
        grid=(t_tok // tm,),
        in_specs=[pl.BlockSpec((tm, d), lambda i: (i, 0)),
                  pl.BlockSpec((tm, WIDTH), lambda i: (i, 0)),
                  pl.BlockSpec((tm, WIDTH), lambda i: (i, 0)),
                  const((d, 2 * d)), const((1, 2 * d)),
                  const((WIDTH, d)), const((WIDTH, d)), const((d, d)),
                  const((1, d)), const((1, d)),
                  const((N_EXPERTS, d)), const((N_EXPERTS, LANES))],
        out_specs=(pl.BlockSpec((tm, d), lambda i: (i, 0)), row8, row8, col8, const((N_EXPERTS, LANES))),
        scratch_shapes=[pltpu.VMEM((N_EXPERTS, LANES), F32)],
        compiler_params=pltpu.CompilerParams(dimension_semantics=("arbitrary",),
                                             vmem_limit_bytes=VMEM_LIMIT),
        name="mixer_ln_router",
    )(x2, a_sb, a_mb, wg, bg, wbs, wbm, wo, lng, lnb, wr, br)


CTL_FIRST = 1
CTL_LAST = 2
CTL_ODD = 4


def _plan_kernel(counts_ref, idx_ref, cnt_ref, dest_ref, start_ref, be_ref, nx_ref, ctl_ref, nu_ref, follow_ref):
    def region(e, off):
        start_ref[e] = off
        return off + (counts_ref[e] + EXPERT_ROWS - 1) // EXPERT_ROWS * EXPERT_ROWS

    total = lax.fori_loop(0, N_EXPERTS, region, 0)
    nu_ref[0] = total // EXPERT_ROWS

    def nonempty_after(t, nxt):
        e = N_EXPERTS - 1 - t
        follow_ref[e] = nxt
        return jnp.where(counts_ref[e] > 0, e, nxt)

    lax.fori_loop(0, N_EXPERTS, nonempty_after, -1)

    def clear(b, _):
        be_ref[b] = 0
        nx_ref[b] = 0
        ctl_ref[b] = 0
        return 0

    lax.fori_loop(0, be_ref.shape[0], clear, 0)

    def blocks_of(e, state):
        b0, odd = state
        n = (counts_ref[e] + EXPERT_ROWS - 1) // EXPERT_ROWS
        follows = follow_ref[e] >= 0

        def block(j, _):
            be_ref[b0 + j] = e
            nx_ref[b0 + j] = jnp.maximum(follow_ref[e], 0)
            ctl_ref[b0 + j] = (jnp.where(follows & (j == 0), CTL_FIRST, 0)
                               + jnp.where(follows & (j == n - 1), CTL_LAST, 0) + odd * CTL_ODD)
            return 0

        lax.fori_loop(0, n, block, 0)
        return b0 + n, jnp.where(n > 0, 1 - odd, odd)

    lax.fori_loop(0, N_EXPERTS, blocks_of, (0, 0))

    idx = idx_ref[...]
    dest = cnt_ref[...]
    for e in range(N_EXPERTS):
        dest = dest + jnp.where(idx == e, start_ref[e], 0)
    dest_ref[...] = dest


def _plan_call(counts, idx_t, cnt_t, n_blocks):
    smem = pl.BlockSpec(memory_space=pltpu.SMEM)
    vmem = pl.BlockSpec(memory_space=pltpu.VMEM)
    blocks = jax.ShapeDtypeStruct((n_blocks,), I32)
    return pl.pallas_call(
        _plan_kernel,
        out_shape=(jax.ShapeDtypeStruct(idx_t.shape, I32), jax.ShapeDtypeStruct((N_EXPERTS,), I32),
                   blocks, blocks, blocks, jax.ShapeDtypeStruct((1,), I32)),
        in_specs=[smem, vmem, vmem],
        out_specs=(vmem, smem, smem, smem, smem, smem),
        scratch_shapes=[pltpu.SMEM((N_EXPERTS,), I32)],
        name="moe_plan",
    )(counts, idx_t, cnt_t)


def _row_copy(src_ref, s, dst_ref, t, sem):
    return pltpu.make_async_copy(src_ref.at[pl.ds(s, 1)], dst_ref.at[pl.ds(t, 1)], sem)


def _dispatch_kernel(*refs, tm, n_pad):
    dest_refs = refs[:TOP_K]
    start_ref, counts_ref, h_ref, xg_ref, zero_ref, sem, zsem = refs[TOP_K:]
    @pl.when(pl.program_id(0) == 0)
    def _():
        zero_ref[...] = jnp.zeros_like(zero_ref)

        def pad_expert(e, _):
            n = (EXPERT_ROWS - counts_ref[e] % EXPERT_ROWS) % EXPERT_ROWS
            off = start_ref[e] + counts_ref[e]
            for s in range(SUBLANES - 1):
                @pl.when(s < (n & (SUBLANES - 1)))
                def _(row=off + s):
                    _row_copy(zero_ref, 0, xg_ref, row, zsem).start()
            off = off + (n & (SUBLANES - 1))
            size = SUBLANES
            while size < EXPERT_ROWS:
                @pl.when((n & size) != 0)
                def _(off=off, size=size):
                    dst = xg_ref.at[pl.ds(pl.multiple_of(off, SUBLANES), size)]
                    pltpu.make_async_copy(zero_ref.at[pl.ds(0, size)], dst, zsem).start()
                off = off + (n & size)
                size *= 2
            return off

        used = lax.fori_loop(0, N_EXPERTS, pad_expert, 0)

        def pad_block(b, _):
            first = pl.multiple_of(used + b * EXPERT_ROWS, EXPERT_ROWS)
            pltpu.make_async_copy(zero_ref, xg_ref.at[pl.ds(first, EXPERT_ROWS)], zsem).start()
            return 0

        lax.fori_loop(0, (xg_ref.shape[0] - used) // EXPERT_ROWS, pad_block, 0)
        pltpu.make_async_copy(xg_ref.at[pl.ds(0, n_pad)], xg_ref.at[pl.ds(0, n_pad)], zsem).wait()

    def issue(g, _):
        base = pl.multiple_of(g * SUBLANES, SUBLANES)
        for s in range(SUBLANES):
            for k in range(TOP_K):
                _row_copy(h_ref, base + s, xg_ref, dest_refs[k][base + s], sem).start()
        return 0

    lax.fori_loop(0, tm // SUBLANES, issue, 0)
    pltpu.make_async_copy(xg_ref.at[pl.ds(0, tm * TOP_K)], xg_ref.at[pl.ds(0, tm * TOP_K)], sem).wait()


def _dispatch_call(dest_rows, start, counts, h, n_rows, tm):
    t_tok, d = h.shape
    smem_blk = pl.BlockSpec((tm,), lambda i: (i,), memory_space=pltpu.SMEM)
    smem = pl.BlockSpec(memory_space=pltpu.SMEM)
    return pl.pallas_call(
        functools.partial(_dispatch_kernel, tm=tm, n_pad=n_rows - t_tok * TOP_K),
        out_shape=jax.ShapeDtypeStruct((n_rows, d), F32),
        grid=(t_tok // tm,),
        in_specs=[smem_blk] * TOP_K + [smem, smem, pl.BlockSpec((tm, d), lambda i: (i, 0))],
        out_specs=pl.BlockSpec(memory_space=pl.ANY),
        scratch_shapes=[pltpu.VMEM((EXPERT_ROWS, d), F32), pltpu.SemaphoreType.DMA(()),
                        pltpu.SemaphoreType.DMA(())],
        compiler_params=pltpu.CompilerParams(dimension_semantics=("arbitrary",),
                                             vmem_limit_bytes=VMEM_LIMIT),
        name="moe_dispatch",
    )(*dest_rows, start, counts, h)


def _expert_kernel(be_ref, nx_ref, ctl_ref, nu_ref, x_ref, wgu_hbm, wd_hbm, bg_ref, bu_ref, bd_ref, o_ref,
                   wgu_stage, wd_stage, wgt0, wut0, wdb0, wgt1, wut1, wdb1, tmp_ref, sems):
    i = pl.program_id(0)
    live = i < nu_ref[0]
    ctl = ctl_ref[i]
    sets = ((wgt0, wut0, wdb0), (wgt1, wut1, wdb1))
    chunk = 2 * LANES

    def fetch(e):
        return (pltpu.make_async_copy(wgu_hbm.at[e], wgu_stage, sems.at[0]),
                pltpu.make_async_copy(wd_hbm.at[e], wd_stage, sems.at[1]))

    def prepare(wgt_ref, wut_ref, wdb_ref):
        for c in range(wgu_stage.shape[1] // chunk):
            rows = slice(c * LANES, (c + 1) * LANES)
            for k in range(wgu_stage.shape[0] // LANES):
                cols = slice(k * LANES, (k + 1) * LANES)
                slot = (c * (wgu_stage.shape[0] // LANES) + k) % tmp_ref.shape[0]
                tmp_ref[slot] = wgu_stage[cols, c * chunk:(c + 1) * chunk].T
                wgt_ref[rows, cols] = tmp_ref[slot, pl.ds(0, LANES, stride=2), :].astype(BF16)
                wut_ref[rows, cols] = tmp_ref[slot, pl.ds(1, LANES, stride=2), :].astype(BF16)
        wdb_ref[...] = wd_stage[...].astype(BF16)

    def ffn(wgt_ref, wut_ref, wdb_ref):
        nt = (((1,), (1,)), ((), ()))
        chains = [dict(rows=slice(c * MXU_ROWS, (c + 1) * MXU_ROWS)) for c in range(EXPERT_ROWS // MXU_ROWS)]

        def gate_up(c):
            x = x_ref[c["rows"], :].astype(BF16)
            c["gate"] = lax.dot_general(x, wgt_ref[...], nt, preferred_element_type=F32)
            c["up"] = lax.dot_general(x, wut_ref[...], nt, preferred_element_type=F32)

        def activation(c):
            gate = jnp.minimum(c.pop("gate") + bg_ref[...], SWIGLU_LIMIT)
            up = jnp.clip(c.pop("up") + bu_ref[...], -SWIGLU_LIMIT, SWIGLU_LIMIT)
            c["act"] = ((up + 1.0) * gate * jax.nn.sigmoid(SWIGLU_ALPHA * gate)).astype(BF16)

        def down(c):
            o_ref[c["rows"], :] = _dot(c.pop("act"), wdb_ref[...]) + bd_ref[...]

        _skewed(chains, ((gate_up, 0), (down, 2), (activation, 1)))

    @pl.when(i == 0)
    def _():
        for cp in fetch(be_ref[0]):
            cp.start()
        for cp in fetch(be_ref[0]):
            cp.wait()
        prepare(*sets[0])

    @pl.when(live & ((ctl & CTL_FIRST) != 0))
    def _():
        for cp in fetch(nx_ref[i]):
            cp.start()

    for odd in range(2):
        mine = live & (((ctl & CTL_ODD) != 0) == bool(odd))

        @pl.when(mine & ((ctl & CTL_LAST) == 0))
        def _(odd=odd):
            ffn(*sets[odd])

        @pl.when(mine & ((ctl & CTL_LAST) != 0))
        def _(odd=odd):
            for cp in fetch(nx_ref[i]):
                cp.wait()
            ffn(*sets[odd])
            prepare(*sets[1 - odd])

    @pl.when(jnp.logical_not(live))
    def _():
        o_ref[...] = jnp.zeros_like(o_ref)


def _expert_call(block_expert, next_expert, ctl, n_used, xg, w_gate_up, w_down, bg, bu, bd):
    n_rows, d = xg.shape
    f = w_down.shape[1]
    n_blocks = n_rows // EXPERT_ROWS
    blk = lambda i, be, nx, ctl, nu: (jnp.minimum(i, nu[0] - 1), 0)
    bsel = lambda i, be, nx, ctl, nu: (be[i], 0, 0)
    prepared = [pltpu.VMEM((f, d), BF16) for _ in range(6)]
    return pl.pallas_call(
        _expert_kernel,
        out_shape=jax.ShapeDtypeStruct((n_rows, d), F32),
        grid_spec=pltpu.PrefetchScalarGridSpec(
            num_scalar_prefetch=4,
            grid=(n_blocks,),
            in_specs=[pl.BlockSpec((EXPERT_ROWS, d), blk),
                      pl.BlockSpec(memory_space=pl.ANY),
                      pl.BlockSpec(memory_space=pl.ANY),
                      pl.BlockSpec((None, 1, f), bsel),
                      pl.BlockSpec((None, 1, f), bsel),
                      pl.BlockSpec((None, 1, d), bsel)],
            out_specs=pl.BlockSpec((EXPERT_ROWS, d), lambda i, be, nx, ctl, nu: (i, 0)),
            scratch_shapes=[pltpu.VMEM((d, 2 * f), F32), pltpu.VMEM((f, d), F32), *prepared,
                            pltpu.VMEM((4, 2 * LANES, LANES), F32), pltpu.SemaphoreType.DMA((2,))]),
        compiler_params=pltpu.CompilerParams(dimension_semantics=("arbitrary",),
                                             vmem_limit_bytes=VMEM_LIMIT),
        name="expert_ffn",
    )(block_expert, next_expert, ctl, n_used, xg, w_gate_up, w_down, bg, bu, bd)


def _combine_kernel(*refs, tm):
    dest_refs, next_refs = refs[:TOP_K], refs[TOP_K:2 * TOP_K]
    h_ref, tw_ref, lng_ref, lnb_ref, rows_ref, o_ref, buf_ref, sems = refs[2 * TOP_K:]
    i = pl.program_id(0)
    slot = i % 2

    def gather(rows_of, into):
        def issue(g, _):
            base = pl.multiple_of(g * SUBLANES, SUBLANES)
            for s in range(SUBLANES):
                for k in range(TOP_K):
                    _row_copy(rows_ref, rows_of[k][base + s], buf_ref.at[into, k], base + s, sems.at[into]).start()
            return 0
        lax.fori_loop(0, tm // SUBLANES, issue, 0)

    @pl.when(i == 0)
    def _():
        gather(dest_refs, 0)

    @pl.when(i + 1 < pl.num_programs(0))
    def _():
        gather(next_refs, 1 - slot)

    pltpu.make_async_copy(buf_ref.at[slot], buf_ref.at[slot], sems.at[slot]).wait()
    tw = tw_ref[...]
    y = tw[:, 0:1] * buf_ref[slot, 0]
    for k in range(1, TOP_K):
        y = y + tw[:, k:k + 1] * buf_ref[slot, k]
    o_ref[...] = _layer_norm(DEEPNORM_ALPHA * h_ref[...] + y, lng_ref[...], lnb_ref[...])


def _combine_call(dest_rows, h, tw, lng, lnb, rows, tm):
    t_tok, d = h.shape
    n_tiles = t_tok // tm
    return pl.pallas_call(
        functools.partial(_combine_kernel, tm=tm),
        out_shape=jax.ShapeDtypeStruct((t_tok, d), F32),
        grid=(n_tiles,),
        in_specs=[pl.BlockSpec((tm,), lambda i: (i,), memory_space=pltpu.SMEM)] * TOP_K
        + [pl.BlockSpec((tm,), lambda i: (jnp.minimum(i + 1, n_tiles - 1),), memory_space=pltpu.SMEM)] * TOP_K
        + [
                  pl.BlockSpec((tm, d), lambda i: (i, 0)),
                  pl.BlockSpec((tm, 8), lambda i: (i, 0)),
                  pl.BlockSpec((1, d), lambda i: (0, 0)),
                  pl.BlockSpec((1, d), lambda i: (0, 0)),
                  pl.BlockSpec(memory_space=pl.ANY)],
        out_specs=pl.BlockSpec((tm, d), lambda i: (i, 0)),
        scratch_shapes=[pltpu.VMEM((2, TOP_K, tm, d), F32), pltpu.SemaphoreType.DMA((2,))],
        compiler_params=pltpu.CompilerParams(dimension_semantics=("arbitrary",),
                                             vmem_limit_bytes=VMEM_LIMIT),
        name="moe_combine_ln",
    )(*dest_rows, *dest_rows, h, tw, lng, lnb, rows)


def _layer(x2, pos, batch, seq, w_in, b_gate, w_branch_sb, w_branch_moba, w_out, ln_mix_g, ln_mix_b,
           w_router, b_router, w_gate_up, b_gate_up, w_down, b_down, ln_ffn_g, ln_ffn_b):
    t_tok, d = x2.shape
    nb = seq // KV_BLOCK
    qkv_w = 6 * WIDTH
    tm_a = 512 if t_tok % 512 == 0 else KV_BLOCK
    tm_mix = 1024 if t_tok % 1024 == 0 else tm_a

    w_qkv = w_in[:, :qkv_w].astype(BF16)
    w_g = w_in[:, qkv_w:].astype(BF16)
    pos3 = pos.astype(F32).reshape(t_tok // tm_a, 1, tm_a)
    invf = (ROPE_THETA ** (-jnp.arange(ROPE_HALF, dtype=F32) / ROPE_HALF)).reshape(ROPE_HALF, 1)
    ki = jnp.arange(KV_BLOCK)
    tri = jnp.where(ki[None, :] > ki[:, None], -1.0, 0.0).astype(BF16)
    avg = jnp.where(jnp.arange(seq)[None, :] // KV_BLOCK == jnp.arange(nb)[:, None],
                    1.0 / KV_BLOCK, 0.0).astype(BF16)

    qt_sb, k_sb, vt_sb, qt_mb, k_mb, vt_mb = _qkv_call(x2, pos3, invf, w_qkv, tm_a)
    a_sb = _sb_call(qt_sb, k_sb, vt_sb, tri, batch, seq)
    a_mb = _moba_call(qt_mb, k_mb, vt_mb, avg, batch, seq)

    w_r = w_router.T.astype(BF16)
    b_r = jnp.broadcast_to(b_router[:, None], (N_EXPERTS, LANES))
    h, idx_t, cnt_t, tw8, counts = _mixer_call(
        x2, a_sb, a_mb, w_g, b_gate.reshape(1, -1), w_branch_sb.astype(BF16), w_branch_moba.astype(BF16),
        w_out.astype(BF16), ln_mix_g.reshape(1, -1), ln_mix_b.reshape(1, -1), w_r, b_r, tm_mix)

    n_assign = t_tok * TOP_K
    n_blocks = -(-(n_assign + N_EXPERTS * (EXPERT_ROWS - 1)) // EXPERT_ROWS)
    counts = counts[:, 0]
    dest_t, start, block_expert, next_expert, ctl, n_used = _plan_call(counts, idx_t, cnt_t, n_blocks)
    dest_rows = [dest_t[k] for k in range(TOP_K)]
    xg = _dispatch_call(dest_rows, start, counts, h, n_blocks * EXPERT_ROWS, tm_a)

    f = w_down.shape[1]
    rows = _expert_call(block_expert, next_expert, ctl, n_used, xg, w_gate_up, w_down,
                        b_gate_up[:, 0::2].reshape(N_EXPERTS, 1, f), b_gate_up[:, 1::2].reshape(N_EXPERTS, 1, f),
                        b_down.reshape(N_EXPERTS, 1, d))
    return _combine_call(dest_rows, h, tw8, ln_ffn_g.reshape(1, -1), ln_ffn_b.reshape(1, -1), rows, KV_BLOCK)


def kernel(x, positions, w_in, b_gate, w_branch_sb, w_branch_moba, w_out, ln_mix_g, ln_mix_b, w_router,
           b_router, w_gate_up, b_gate_up, w_down, b_down, ln_ffn_g, ln_ffn_b):
    batch, seq, d = x.shape
    h = x.reshape(batch * seq, d)
    pos = positions.reshape(batch * seq)
    for layer in range(w_in.shape[0]):
        h = _layer(h, pos, batch, seq, w_in[layer], b_gate[layer], w_branch_sb[layer], w_branch_moba[layer],
                   w_out[layer], ln_mix_g[layer], ln_mix_b[layer], w_router[layer], b_router[layer],
                   w_gate_up[layer], b_gate_up[layer], w_down[layer], b_down[layer],
                   ln_ffn_g[layer], ln_ffn_b[layer])
    return h.reshape(batch, seq, d)
```

```python
import functools

import jax
import jax.numpy as jnp
from jax import lax
from jax.experimental import pallas as pl
from jax.experimental.pallas import tpu as pltpu

F32 = jnp.float32
BF16 = jnp.bfloat16
I32 = jnp.int32

HEAD_DIM = 64
N_HEADS = 8
WIDTH = N_HEADS * HEAD_DIM
LANES = 128
SUBLANES = 8
N_PAIRS = WIDTH // LANES
KV_BLOCK = 256
MOBA_TOPK = 3
ROPE_THETA = 500000.0
ROPE_HALF = 8
N_EXPERTS = 32
TOP_K = 4
SWIGLU_LIMIT = 7.0
SWIGLU_ALPHA = 1.702
EXPERT_ROWS = 512
MXU_ROWS = 256
TILE = 512
CHUNK = 8
LOCAL_ROWS = TILE * TOP_K + N_EXPERTS * CHUNK
CHUNKS_PER_BLOCK = EXPERT_ROWS // CHUNK
ZERO_ROW = LOCAL_ROWS - CHUNK
LN_EPS = 1e-5
DEPTH = 1
DEEPNORM_ALPHA = (2 * DEPTH) ** 0.25
QK_SCALE = HEAD_DIM ** -0.5
NEG_INF = float("-inf")
VMEM_LIMIT = 56 * 1024 * 1024


def _dot(a, b):
    return jnp.dot(a, b, preferred_element_type=F32)


def _qkv_kernel(x_ref, pos_ref, invf_ref, w_ref,
                qt_sb_ref, k_sb_ref, vt_sb_ref, qt_mb_ref, k_mb_ref, vt_mb_ref, *, tm):
    xb = x_ref[...].astype(BF16)
    ang = invf_ref[...] * pos_ref[0]
    cos, sin = jnp.cos(ang), jnp.sin(ang)

    def rope_t(t):
        parts = []
        for base in (0, HEAD_DIM):
            x1 = t[base:base + ROPE_HALF]
            x2 = t[base + ROPE_HALF:base + 2 * ROPE_HALF]
            parts += [x1 * cos - x2 * sin, x2 * cos + x1 * sin, t[base + 2 * ROPE_HALF:base + HEAD_DIM]]
        return jnp.concatenate(parts, axis=0)

    def store_t(ref, p, t):
        tb = t.astype(BF16)
        for blk in range(tm // KV_BLOCK):
            ref[blk, p] = tb[:, blk * KV_BLOCK:(blk + 1) * KV_BLOCK]

    for sec in range(6):
        for half in range(2):
            c0 = sec * WIDTH + half * 2 * LANES
            r = _dot(xb, w_ref[:, c0:c0 + 2 * LANES])
            for q in range(2):
                p = half * 2 + q
                t = r[:, q * LANES:(q + 1) * LANES]
                if sec == 0:
                    store_t(qt_sb_ref, p, (t * QK_SCALE).T)
                elif sec == 1:
                    k_sb_ref[:, p * LANES:(p + 1) * LANES] = t.astype(BF16)
                elif sec == 2:
                    store_t(vt_sb_ref, p, t.T)
                elif sec == 3:
                    store_t(qt_mb_ref, p, rope_t(t.T) * QK_SCALE)
                elif sec == 4:
                    k_mb_ref[:, p * LANES:(p + 1) * LANES] = rope_t(t.T).T.astype(BF16)
                else:
                    store_t(vt_mb_ref, p, t.T)


def _qkv_call(x2, pos3, invf, w_qkv, tm):
    t_tok, d = x2.shape
    nblk = t_tok // KV_BLOCK
    bpt = tm // KV_BLOCK
    t_shape = jax.ShapeDtypeStruct((nblk, N_PAIRS, LANES, KV_BLOCK), BF16)
    n_shape = jax.ShapeDtypeStruct((t_tok, WIDTH), BF16)
    t_spec = pl.BlockSpec((bpt, N_PAIRS, LANES, KV_BLOCK), lambda i: (i, 0, 0, 0))
    n_spec = pl.BlockSpec((tm, WIDTH), lambda i: (i, 0))
    return pl.pallas_call(
        functools.partial(_qkv_kernel, tm=tm),
        out_shape=(t_shape, n_shape, t_shape, t_shape, n_shape, t_shape),
        grid=(t_tok // tm,),
        in_specs=[pl.BlockSpec((tm, d), lambda i: (i, 0)),
                  pl.BlockSpec((1, 1, tm), lambda i: (i, 0, 0)),
                  pl.BlockSpec((ROPE_HALF, 1), lambda i: (0, 0)),
                  pl.BlockSpec((d, 6 * WIDTH), lambda i: (0, 0))],
        out_specs=(t_spec, n_spec, t_spec, t_spec, n_spec, t_spec),
        compiler_params=pltpu.CompilerParams(dimension_semantics=("arbitrary",),
                                             vmem_limit_bytes=VMEM_LIMIT),
        name="qkv_proj",
    )(x2, pos3, invf, w_qkv)


def _head_rows(h):
    row = lax.broadcasted_iota(I32, (LANES, KV_BLOCK), 0)
    return (row >= HEAD_DIM * h) & (row < HEAD_DIM * (h + 1))


def _tile_specs(seq):
    nb = seq // KV_BLOCK
    tiles_spec = pl.BlockSpec((nb, None, LANES, KV_BLOCK), lambda b, p, i: (b, p, 0, 0))
    col_spec = pl.BlockSpec((seq, LANES), lambda b, p, i: (b, p))
    return tiles_spec, col_spec


def _chain_list(i, nb, past_block):
    tiles = (i, nb - 1 - i)
    chains = [dict(j=tiles[a], a=a, h=h, diagonal=True, first=None) for a in range(2) for h in range(2)]
    for t in range(nb - 1):
        first = t < i
        for h in range(2):
            chains.append(dict(j=past_block(first, t), a=jnp.where(first, 0, 1), h=h, diagonal=False, first=first))
    return tiles, chains


def _skewed(chains, stages):
    n = len(chains)
    for slot in range(n + max(lag for _, lag in stages)):
        for stage, lag in stages:
            if 0 <= slot - lag < n:
                stage(chains[slot - lag])


def _sb_kernel(qt_ref, k_ref, vt_ref, tri_ref, o_ref, q_scr, acc_scr):
    i = pl.program_id(2)
    nb = qt_ref.shape[0]
    tiles, chains = _chain_list(i, nb, lambda first, t: jnp.where(first, i - 1 - t, nb - 2 - t))
    for a in range(2):
        qt = qt_ref[tiles[a]]
        for h in range(2):
            q_scr[a, h] = jnp.where(_head_rows(h), qt, jnp.zeros_like(qt))
    acc_scr[...] = jnp.zeros_like(acc_scr)
    key = lax.broadcasted_iota(I32, (KV_BLOCK, KV_BLOCK), 0)
    qry = lax.broadcasted_iota(I32, (KV_BLOCK, KV_BLOCK), 1)
    past = key < qry
    tri = tri_ref[...]

    zero = jnp.zeros((1, KV_BLOCK), F32)
    carries = [[zero, zero], [zero, zero]]

    def scores(c):
        kb = k_ref[pl.ds(pl.multiple_of(c["j"] * KV_BLOCK, KV_BLOCK), KV_BLOCK), :]
        c["z"] = _dot(kb, q_scr[c["a"], c["h"]])

    def softplus(c):
        z = c.pop("z")
        sp = jnp.maximum(z, 0.0) + jnp.log(1.0 + jnp.exp(-jnp.abs(z)))
        spm = jnp.where(past, sp, 0.0) if c["diagonal"] else sp
        c["log_beta"] = z - sp
        c["spm"] = spm.astype(BF16)
        c["sp0"] = spm[0:1, :]

    def suffix(c):
        c["after"] = _dot(tri, c.pop("spm"))

    def weights(c):
        after = c.pop("after")
        w = jnp.exp(c.pop("log_beta") + after)
        if c["diagonal"]:
            w = jnp.where(past, w, 0.0)
        c["w"] = w.astype(BF16)
        total = after[0:1, :] - c.pop("sp0")
        h, first = c["h"], c["first"]
        if first is None:
            c["carry"] = zero
            carries[c["a"]][h] = total
        else:
            c["carry"] = jnp.where(first, carries[0][h], carries[1][h])
            cout = c["carry"] + total
            carries[0][h] = jnp.where(first, cout, carries[0][h])
            carries[1][h] = jnp.where(first, carries[1][h], cout)

    def values(c):
        rows = slice(c["h"] * HEAD_DIM, (c["h"] + 1) * HEAD_DIM)
        pv = _dot(vt_ref[c["j"], rows, :], c.pop("w")) * jnp.exp(c.pop("carry"))
        acc_scr[c["a"], rows, :] += pv

    _skewed(chains, ((scores, 0), (suffix, 2), (values, 4), (softplus, 1), (weights, 3)))
    for a in range(2):
        o_ref[pl.ds(pl.multiple_of(tiles[a] * KV_BLOCK, KV_BLOCK), KV_BLOCK), :] = acc_scr[a].T.astype(BF16)


def _sb_call(qt, k, vt, tri, batch, seq):
    nb = seq // KV_BLOCK
    tiles_spec, col_spec = _tile_specs(seq)
    return pl.pallas_call(
        _sb_kernel,
        out_shape=jax.ShapeDtypeStruct((batch * seq, WIDTH), BF16),
        grid=(batch, N_PAIRS, nb // 2),
        in_specs=[tiles_spec, col_spec, tiles_spec,
                  pl.BlockSpec((KV_BLOCK, KV_BLOCK), lambda b, p, i: (0, 0))],
        out_specs=col_spec,
        scratch_shapes=[pltpu.VMEM((2, 2, LANES, KV_BLOCK), BF16), pltpu.VMEM((2, LANES, KV_BLOCK), F32)],
        compiler_params=pltpu.CompilerParams(
            dimension_semantics=("arbitrary", "arbitrary", "arbitrary"), vmem_limit_bytes=VMEM_LIMIT),
        name="stickbreak_attn",
    )(qt, k, vt, tri)


def _moba_kernel(qt_ref, k_ref, vt_ref, avg_ref, o_ref, km_ref, bias_scr, q_scr, acc_scr, s_scr, den_scr):
    i = pl.program_id(2)
    nb = qt_ref.shape[0]

    @pl.when(i == 0)
    def _():
        km_ref[...] = _dot(avg_ref[...], k_ref[...])

    tiles, chains = _chain_list(i, nb, lambda first, t: jnp.where(first, t, t - i))
    key = lax.broadcasted_iota(I32, (KV_BLOCK, KV_BLOCK), 0)
    qry = lax.broadcasted_iota(I32, (KV_BLOCK, KV_BLOCK), 1)
    causal = key <= qry
    blk = lax.broadcasted_iota(I32, (nb, KV_BLOCK), 0)
    km = km_ref[...].astype(BF16)
    for a in range(2):
        qt = qt_ref[tiles[a]]
        valid = blk < tiles[a]
        for h in range(2):
            qth = jnp.where(_head_rows(h), qt, jnp.zeros_like(qt))
            q_scr[a, h] = qth
            g = jnp.where(valid, _dot(km, qth), NEG_INF)
            rank = jnp.zeros((nb, KV_BLOCK), F32)
            for jp in range(nb):
                gj = g[jp:jp + 1, :]
                better = jnp.where(gj > g, 1.0, jnp.where(gj == g, jnp.where(blk > jp, 1.0, 0.0), 0.0))
                rank = rank + better
            bias_scr[a, h] = jnp.where(valid, jnp.where(rank < MOBA_TOPK, 0.0, NEG_INF), NEG_INF)

    for n, c in enumerate(chains):
        c["n"] = n
    low = jnp.full((SUBLANES, KV_BLOCK), NEG_INF, F32)
    maxima = [[low, low], [low, low]]
    acc_scr[...] = jnp.zeros_like(acc_scr)
    den_scr[...] = jnp.zeros_like(den_scr)
    ones = jnp.ones((2 * SUBLANES, KV_BLOCK), BF16)

    def scores(c):
        kb = k_ref[pl.ds(pl.multiple_of(c["j"] * KV_BLOCK, KV_BLOCK), KV_BLOCK), :]
        c["s"] = _dot(kb, q_scr[c["a"], c["h"]])

    def mask(c):
        h, first = c["h"], c["first"]
        if first is None:
            s = jnp.where(causal, c.pop("s"), NEG_INF)
        else:
            s = c.pop("s") + bias_scr[c["a"], h, pl.ds(c["j"], 1), :]
        s_scr[c["n"]] = s
        top = jnp.max(s.reshape(KV_BLOCK // SUBLANES, SUBLANES, KV_BLOCK), axis=0)
        if first is None:
            maxima[c["a"]][h] = top
        else:
            maxima[0][h] = jnp.where(first, jnp.maximum(maxima[0][h], top), maxima[0][h])
            maxima[1][h] = jnp.where(first, maxima[1][h], jnp.maximum(maxima[1][h], top))

    _skewed(chains, ((scores, 0), (mask, 1)))
    maxima = [[jnp.max(maxima[a][h], axis=0, keepdims=True) for h in range(2)] for a in range(2)]

    def weights(c):
        h, first = c["h"], c["first"]
        m = maxima[c["a"]][h] if first is None else jnp.where(first, maxima[0][h], maxima[1][h])
        c["p"] = jnp.exp(s_scr[c["n"]] - m).astype(BF16)

    def values(c):
        rows = slice(c["h"] * HEAD_DIM, (c["h"] + 1) * HEAD_DIM)
        pv = _dot(jnp.concatenate([vt_ref[c["j"], rows, :], ones], axis=0), c.pop("p"))
        acc_scr[c["a"], rows, :] += pv[:HEAD_DIM]
        den_scr[c["a"], c["h"]] += pv[HEAD_DIM:]

    _skewed(chains, ((values, 1), (weights, 0)))
    for a in range(2):
        for h in range(2):
            rows = slice(h * HEAD_DIM, (h + 1) * HEAD_DIM)
            acc_scr[a, rows, :] = acc_scr[a, rows, :] / den_scr[a, h, 0:1, :]
        o_ref[pl.ds(pl.multiple_of(tiles[a] * KV_BLOCK, KV_BLOCK), KV_BLOCK), :] = acc_scr[a].T.astype(BF16)


def _moba_call(qt, k, vt, avg, batch, seq):
    nb = seq // KV_BLOCK
    tiles_spec, col_spec = _tile_specs(seq)
    return pl.pallas_call(
        _moba_kernel,
        out_shape=jax.ShapeDtypeStruct((batch * seq, WIDTH), BF16),
        grid=(batch, N_PAIRS, nb // 2),
        in_specs=[tiles_spec, col_spec, tiles_spec, pl.BlockSpec((nb, seq), lambda b, p, i: (0, 0))],
        out_specs=col_spec,
        scratch_shapes=[pltpu.VMEM((nb, LANES), F32), pltpu.VMEM((2, 2, nb, KV_BLOCK), F32),
                        pltpu.VMEM((2, 2, LANES, KV_BLOCK), BF16), pltpu.VMEM((2, LANES, KV_BLOCK), F32),
                        pltpu.VMEM((2 * (nb + 1), KV_BLOCK, KV_BLOCK), F32),
                        pltpu.VMEM((2, 2, 2 * SUBLANES, KV_BLOCK), F32)],
        compiler_params=pltpu.CompilerParams(
            dimension_semantics=("arbitrary", "arbitrary", "arbitrary"), vmem_limit_bytes=VMEM_LIMIT),
        name="moba_attn",
    )(qt, k, vt, avg)


def _layer_norm(r, g, b):
    mu = jnp.mean(r, axis=-1, keepdims=True)
    d = r - mu
    var = jnp.mean(d * d, axis=-1, keepdims=True)
    return d * lax.rsqrt(var + LN_EPS) * g + b


def _mixer_kernel(x_ref, asb_ref, amb_ref, wg_ref, bg_ref, wbs_ref, wbm_ref, wo_ref, lng_ref, lnb_ref,
                  wr_ref, br_ref, h_ref, ls_ref, lpos_ref, tw_ref, cnts_ref, *, d):
    rows_per = MXU_ROWS
    chains = [dict(rows=slice(c * rows_per, (c + 1) * rows_per)) for c in range(TILE // rows_per)]
    expert = lax.broadcasted_iota(I32, (N_EXPERTS, rows_per), 0)
    slot8 = lax.broadcasted_iota(I32, (8, rows_per), 0)
    rt = lax.broadcasted_iota(I32, (rows_per, rows_per), 0)
    ct = lax.broadcasted_iota(I32, (rows_per, rows_per), 1)
    earlier = jnp.where(rt < ct, 1.0, 0.0).astype(BF16)
    seen = [jnp.zeros((N_EXPERTS, LANES), F32)]

    def project(c):
        c["gp"] = _dot(x_ref[c["rows"], :].astype(BF16), wg_ref[...])
        c["ysb"] = _dot(asb_ref[c["rows"], :], wbs_ref[...])
        c["ymb"] = _dot(amb_ref[c["rows"], :], wbm_ref[...])

    def gate(c):
        g = jax.nn.sigmoid(c.pop("gp") + bg_ref[...])
        c["mixed"] = (g[:, :d] * c.pop("ysb") + g[:, d:] * c.pop("ymb")).astype(BF16)

    def out_proj(c):
        c["mix"] = _dot(c.pop("mixed"), wo_ref[...])

    def norm(c):
        h = _layer_norm(DEEPNORM_ALPHA * x_ref[c["rows"], :] + c.pop("mix"), lng_ref[...], lnb_ref[...])
        h_ref[c["rows"], :] = h
        c["hb"] = h.astype(BF16)

    def route(c):
        nt = (((1,), (1,)), ((), ()))
        c["logits"] = lax.dot_general(wr_ref[...], c["hb"], nt, preferred_element_type=F32)

    def top_k(c):
        bias = br_ref[...]
        logits = c.pop("logits") + jnp.concatenate([bias] * (rows_per // LANES), axis=1)
        vals, idxs = [], []
        member = jnp.zeros((N_EXPERTS, rows_per), F32)
        for _ in range(TOP_K):
            mx = jnp.max(logits, axis=0, keepdims=True)
            ik = jnp.min(jnp.where(logits == mx, expert, N_EXPERTS), axis=0, keepdims=True)
            hit = expert == ik
            vals.append(mx)
            idxs.append(ik)
            member = jnp.where(hit, 1.0, member)
            logits = jnp.where(hit, NEG_INF, logits)
        es = [jnp.exp(v - vals[0]) for v in vals]
        den = es[0] + es[1] + es[2] + es[3]
        c["tw"] = [e / den for e in es]
        c["idxs"] = idxs
        c["member"] = member

    def count(c):
        member = c.pop("member")
        c["rank"] = _dot(member.astype(BF16), earlier) + jnp.concatenate([seen[0]] * (rows_per // LANES), axis=1)
        seen[0] = seen[0] + jnp.sum(member, axis=1, keepdims=True)

    _skewed(chains, ((project, 0), (out_proj, 2), (route, 4), (count, 6), (gate, 1), (norm, 3), (top_k, 5)))

    counts = seen[0]
    cnts_ref[...] = counts.astype(I32)
    seg = jnp.ceil(counts * (1.0 / CHUNK)) * CHUNK
    e_r = lax.broadcasted_iota(I32, (N_EXPERTS, N_EXPERTS), 0)
    e_c = lax.broadcasted_iota(I32, (N_EXPERTS, N_EXPERTS), 1)
    loff = _dot(jnp.where(e_c < e_r, 1.0, 0.0).astype(BF16), seg.astype(BF16))
    loff = jnp.concatenate([loff] * (rows_per // LANES), axis=1)
    lpos, tws = [], []
    for c in chains:
        where_to = c.pop("rank") + loff
        lp = jnp.full((8, rows_per), -1.0, F32)
        tw = jnp.zeros((8, rows_per), F32)
        for k, ik in enumerate(c.pop("idxs")):
            pk = jnp.sum(jnp.where(expert == ik, where_to, 0.0), axis=0, keepdims=True)
            lp = jnp.where(slot8 == k, pk, lp)
            tw = jnp.where(slot8 == k, c["tw"][k], tw)
        lpos.append(lp)
        tws.append(tw)
    lpos = jnp.concatenate(lpos, axis=1)
    lpos_ref[...] = lpos.astype(I32)
    tw_ref[...] = jnp.concatenate(tws, axis=1)

    hb = jnp.concatenate([c["hb"] for c in chains], axis=0)
    pieces = [dict(j=j) for j in range(LOCAL_ROWS // rows_per)]
    row = lax.broadcasted_iota(I32, (rows_per, TILE), 0)

    def select(q):
        r = (row + q["j"] * rows_per).astype(F32)
        sel = jnp.zeros((rows_per, TILE), F32)
        for k in range(TOP_K):
            sel = sel + jnp.where(lpos[k:k + 1, :] == r, 1.0, 0.0)
        q["sel"] = sel.astype(BF16)

    def place(q):
        ls_ref[q["j"] * rows_per:(q["j"] + 1) * rows_per, :] = _dot(q.pop("sel"), hb)

    _skewed(pieces, ((place, 1), (select, 0)))


def _mixer_call(x2, a_sb, a_mb, wg, bg, wbs, wbm, wo, lng, lnb, wr, br):
    t_tok, d = x2.shape
    n_tiles = t_tok // TILE
    const = lambda shape: pl.BlockSpec(shape, lambda i: (0,) * len(shape))
    row8 = pl.BlockSpec((8, TILE), lambda i: (0, i))
    return pl.pallas_call(
        functools.partial(_mixer_kernel, d=d),
        out_shape=(jax.ShapeDtypeStruct((t_tok, d), F32),
                   jax.ShapeDtypeStruct((n_tiles * LOCAL_ROWS, d), F32),
                   jax.ShapeDtypeStruct((8, t_tok), I32),
                   jax.ShapeDtypeStruct((8, t_tok), F32),
                   jax.ShapeDtypeStruct((n_tiles, N_EXPERTS, LANES), I32)),
        grid=(n_tiles,),
        in_specs=[pl.BlockSpec((TILE, d), lambda i: (i, 0)),
                  pl.BlockSpec((TILE, WIDTH), lambda i: (i, 0)),
                  pl.BlockSpec((TILE, WIDTH), lambda i: (i, 0)),
                  const((d, 2 * d)), const((1, 2 * d)),
                  const((WIDTH, d)), const((WIDTH, d)), const((d, d)),
                  const((1, d)), const((1, d)),
                  const((N_EXPERTS, d)), const((N_EXPERTS, LANES))],
        out_specs=(pl.BlockSpec((TILE, d), lambda i: (i, 0)),
                   pl.BlockSpec((LOCAL_ROWS, d), lambda i: (i, 0)),
                   row8, row8,
                   pl.BlockSpec((None, N_EXPERTS, LANES), lambda i: (i, 0, 0))),
        compiler_params=pltpu.CompilerParams(dimension_semantics=("arbitrary",),
                                             vmem_limit_bytes=VMEM_LIMIT),
        name="mixer_ln_router",
    )(x2, a_sb, a_mb, wg, bg, wbs, wbm, wo, lng, lnb, wr, br)


CTL_FIRST = 1
CTL_LAST = 2
CTL_ODD = 4


def _plan_kernel(cnt_ref, be_ref, nx_ref, ctl_ref, nu_ref, table_ref, tail_ref, size_ref, follow_ref, *, n_tiles):
    def seg(t, e):
        return (cnt_ref[t * N_EXPERTS + e] + CHUNK - 1) // CHUNK * CHUNK

    def size_of(e, _):
        size_ref[e] = lax.fori_loop(0, n_tiles, lambda t, a: a + seg(t, e), 0)
        return 0

    lax.fori_loop(0, N_EXPERTS, size_of, 0)

    def nonempty_after(t, nxt):
        e = N_EXPERTS - 1 - t
        follow_ref[e] = nxt
        return jnp.where(size_ref[e] > 0, e, nxt)

    lax.fori_loop(0, N_EXPERTS, nonempty_after, -1)

    def clear_block(b, _):
        be_ref[b] = 0
        nx_ref[b] = 0
        ctl_ref[b] = 0
        return 0

    lax.fori_loop(0, be_ref.shape[0], clear_block, 0)

    def clear_chunk(q, _):
        table_ref[q] = -1
        return 0

    lax.fori_loop(0, table_ref.shape[0], clear_chunk, 0)

    def clear_tile(t, _):
        tail_ref[t] = 0
        return 0

    lax.fori_loop(0, n_tiles, clear_tile, 0)

    def blocks_of(e, state):
        b0, odd = state
        n = (size_ref[e] + EXPERT_ROWS - 1) // EXPERT_ROWS
        follows = follow_ref[e] >= 0

        def block(j, _):
            be_ref[b0 + j] = e
            nx_ref[b0 + j] = jnp.maximum(follow_ref[e], 0)
            ctl_ref[b0 + j] = (jnp.where(follows & (j == 0), CTL_FIRST, 0)
                               + jnp.where(follows & (j == n - 1), CTL_LAST, 0) + odd * CTL_ODD)
            return 0

        lax.fori_loop(0, n, block, 0)

        def tile_chunks(t, pos):
            lo = tail_ref[t]
            chunks = seg(t, e) // CHUNK

            def chunk(q, _):
                table_ref[pos + q] = t * LOCAL_ROWS + lo + q * CHUNK
                return 0

            lax.fori_loop(0, chunks, chunk, 0)
            tail_ref[t] = lo + chunks * CHUNK
            return pos + chunks

        lax.fori_loop(0, n_tiles, tile_chunks, b0 * CHUNKS_PER_BLOCK)
        return b0 + n, jnp.where(n > 0, 1 - odd, odd)

    n_used, _ = lax.fori_loop(0, N_EXPERTS, blocks_of, (0, 0))
    nu_ref[0] = n_used


def _plan_call(tile_counts, n_tiles, n_blocks):
    smem = pl.BlockSpec(memory_space=pltpu.SMEM)
    blocks = jax.ShapeDtypeStruct((n_blocks,), I32)
    return pl.pallas_call(
        functools.partial(_plan_kernel, n_tiles=n_tiles),
        out_shape=(blocks, blocks, blocks, jax.ShapeDtypeStruct((1,), I32),
                   jax.ShapeDtypeStruct((n_blocks * CHUNKS_PER_BLOCK,), I32),
                   jax.ShapeDtypeStruct((n_tiles,), I32)),
        in_specs=[smem],
        out_specs=(smem,) * 6,
        scratch_shapes=[pltpu.SMEM((N_EXPERTS,), I32), pltpu.SMEM((N_EXPERTS,), I32)],
        name="moe_plan",
    )(tile_counts)


def _sink_pieces():
    return [(off, min(EXPERT_ROWS, LOCAL_ROWS - off)) for off in range(0, LOCAL_ROWS, EXPERT_ROWS)]


def _expert_kernel(be_ref, nx_ref, ctl_ref, nu_ref, table_ref, tail_ref,
                   ls_hbm, wgu_hbm, wd_hbm, bg_ref, bu_ref, bd_ref, lo_hbm,
                   x_buf, o_buf, wgu_stage, wd_stage, wgt0, wut0, wdb0, wgt1, wut1, wdb1, tmp_ref,
                   wsems, xsems, osems, zsem, *, n_tiles):
    i = pl.program_id(0)
    slot = i % 2
    n_used = nu_ref[0]
    live = i < n_used
    ctl = ctl_ref[i]
    sets = ((wgt0, wut0, wdb0), (wgt1, wut1, wdb1))
    chunk = 2 * LANES
    spare = n_tiles * LOCAL_ROWS

    def fetch(e):
        return (pltpu.make_async_copy(wgu_hbm.at[e], wgu_stage, wsems.at[0]),
                pltpu.make_async_copy(wd_hbm.at[e], wd_stage, wsems.at[1]))

    def prepare(wgt_ref, wut_ref, wdb_ref):
        for c in range(wgu_stage.shape[1] // chunk):
            rows = slice(c * LANES, (c + 1) * LANES)
            for k in range(wgu_stage.shape[0] // LANES):
                cols = slice(k * LANES, (k + 1) * LANES)
                tslot = (c * (wgu_stage.shape[0] // LANES) + k) % tmp_ref.shape[0]
                tmp_ref[tslot] = wgu_stage[cols, c * chunk:(c + 1) * chunk].T
                wgt_ref[rows, cols] = tmp_ref[tslot, pl.ds(0, LANES, stride=2), :].astype(BF16)
                wut_ref[rows, cols] = tmp_ref[tslot, pl.ds(1, LANES, stride=2), :].astype(BF16)
        wdb_ref[...] = wd_stage[...].astype(BF16)

    def rows_in(b, s):
        def one(j, _):
            a = table_ref[b * CHUNKS_PER_BLOCK + j]
            src = pl.multiple_of(jnp.where(a >= 0, a, ZERO_ROW), CHUNK)
            dst = x_buf.at[s, pl.ds(pl.multiple_of(j * CHUNK, CHUNK), CHUNK)]
            pltpu.make_async_copy(ls_hbm.at[pl.ds(src, CHUNK)], dst, xsems.at[s]).start()
            return 0
        lax.fori_loop(0, CHUNKS_PER_BLOCK, one, 0)

    def rows_out(b, s):
        def one(j, _):
            a = table_ref[b * CHUNKS_PER_BLOCK + j]
            pad = spare + s * EXPERT_ROWS + j * CHUNK
            dst = pl.multiple_of(jnp.where(a >= 0, a, pad), CHUNK)
            src = o_buf.at[s, pl.ds(pl.multiple_of(j * CHUNK, CHUNK), CHUNK)]
            pltpu.make_async_copy(src, lo_hbm.at[pl.ds(dst, CHUNK)], osems.at[s]).start()
            return 0
        lax.fori_loop(0, CHUNKS_PER_BLOCK, one, 0)

    def block_done(buf, sems, s):
        pltpu.make_async_copy(buf.at[s], buf.at[s], sems.at[s]).wait()

    def ffn(wgt_ref, wut_ref, wdb_ref):
        nt = (((1,), (1,)), ((), ()))
        chains = [dict(rows=slice(c * MXU_ROWS, (c + 1) * MXU_ROWS)) for c in range(EXPERT_ROWS // MXU_ROWS)]

        def gate_up(c):
            x = x_buf[slot, c["rows"], :].astype(BF16)
            c["gate"] = lax.dot_general(x, wgt_ref[...], nt, preferred_element_type=F32)
            c["up"] = lax.dot_general(x, wut_ref[...], nt, preferred_element_type=F32)

        def activation(c):
            gate = jnp.minimum(c.pop("gate") + bg_ref[...], SWIGLU_LIMIT)
            up = jnp.clip(c.pop("up") + bu_ref[...], -SWIGLU_LIMIT, SWIGLU_LIMIT)
            c["act"] = ((up + 1.0) * gate * jax.nn.sigmoid(SWIGLU_ALPHA * gate)).astype(BF16)

        def down(c):
            o_buf[slot, c["rows"], :] = _dot(c.pop("act"), wdb_ref[...]) + bd_ref[...]

        _skewed(chains, ((gate_up, 0), (down, 2), (activation, 1)))

    @pl.when(i == 0)
    def _():
        o_buf[0] = jnp.zeros_like(o_buf[0])

        def tails(start):
            def tile(t, _):
                def one(q, _):
                    dst = pl.multiple_of(t * LOCAL_ROWS + tail_ref[t] + q * CHUNK, CHUNK)
                    cp = pltpu.make_async_copy(o_buf.at[0, pl.ds(0, CHUNK)], lo_hbm.at[pl.ds(dst, CHUNK)], zsem)
                    if start:
                        cp.start()
                    else:
                        cp.wait()
                    return 0
                lax.fori_loop(0, (LOCAL_ROWS - tail_ref[t]) // CHUNK, one, 0)
                return 0
            lax.fori_loop(0, n_tiles, tile, 0)

        sink = [pltpu.make_async_copy(o_buf.at[0, pl.ds(0, n)], lo_hbm.at[pl.ds(spare + off, n)], zsem)
                for off, n in _sink_pieces()]
        tails(True)
        for cp in sink:
            cp.start()
        tails(False)
        for cp in sink:
            cp.wait()
        rows_in(0, 0)
        for cp in fetch(be_ref[0]):
            cp.start()
        for cp in fetch(be_ref[0]):
            cp.wait()
        prepare(*sets[0])

    @pl.when(i + 1 < n_used)
    def _():
        rows_in(i + 1, 1 - slot)

    @pl.when(live & ((ctl & CTL_FIRST) != 0))
    def _():
        for cp in fetch(nx_ref[i]):
            cp.start()

    @pl.when(live)
    def _():
        block_done(x_buf, xsems, slot)

    @pl.when(live & (i >= 2))
    def _():
        block_done(o_buf, osems, slot)

    for odd in range(2):
        mine = live & (((ctl & CTL_ODD) != 0) == bool(odd))

        @pl.when(mine & ((ctl & CTL_LAST) == 0))
        def _(odd=odd):
            ffn(*sets[odd])

        @pl.when(mine & ((ctl & CTL_LAST) != 0))
        def _(odd=odd):
            for cp in fetch(nx_ref[i]):
                cp.wait()
            ffn(*sets[odd])
            prepare(*sets[1 - odd])

    @pl.when(live)
    def _():
        rows_out(i, slot)

    @pl.when(i == n_used - 1)
    def _():
        block_done(o_buf, osems, slot)

        @pl.when(i >= 1)
        def _():
            block_done(o_buf, osems, 1 - slot)


def _expert_call(block_expert, next_expert, ctl, n_used, table, tail, ls, w_gate_up, w_down, bg, bu, bd, n_tiles):
    d = ls.shape[1]
    f = w_down.shape[1]
    n_blocks = block_expert.shape[0]
    bsel = lambda i, be, nx, ctl, nu, tb, tl: (be[i], 0, 0)
    anywhere = pl.BlockSpec(memory_space=pl.ANY)
    prepared = [pltpu.VMEM((f, d), BF16) for _ in range(6)]
    return pl.pallas_call(
        functools.partial(_expert_kernel, n_tiles=n_tiles),
        out_shape=jax.ShapeDtypeStruct(((n_tiles + 1) * LOCAL_ROWS, d), F32),
        grid_spec=pltpu.PrefetchScalarGridSpec(
            num_scalar_prefetch=6,
            grid=(n_blocks,),
            in_specs=[anywhere, anywhere, anywhere,
                      pl.BlockSpec((None, 1, f), bsel),
                      pl.BlockSpec((None, 1, f), bsel),
                      pl.BlockSpec((None, 1, d), bsel)],
            out_specs=anywhere,
            scratch_shapes=[pltpu.VMEM((2, EXPERT_ROWS, d), F32), pltpu.VMEM((2, EXPERT_ROWS, d), F32),
                            pltpu.VMEM((d, 2 * f), F32), pltpu.VMEM((f, d), F32), *prepared,
                            pltpu.VMEM((4, 2 * LANES, LANES), F32),
                            pltpu.SemaphoreType.DMA((2,)), pltpu.SemaphoreType.DMA((2,)),
                            pltpu.SemaphoreType.DMA((2,)), pltpu.SemaphoreType.DMA(())]),
        compiler_params=pltpu.CompilerParams(dimension_semantics=("arbitrary",),
                                             vmem_limit_bytes=VMEM_LIMIT),
        name="expert_ffn",
    )(block_expert, next_expert, ctl, n_used, table, tail, ls, w_gate_up, w_down, bg, bu, bd)


def _combine_kernel(lo_ref, lpos_ref, tw_ref, h_ref, lng_ref, lnb_ref, o_ref):
    rows_per = MXU_ROWS
    fill = jnp.zeros((LANES - 8, TILE), F32)
    lp_col = jnp.concatenate([lpos_ref[...].astype(F32), fill], axis=0).T
    tw_col = jnp.concatenate([tw_ref[...], fill], axis=0).T
    lane = lax.broadcasted_iota(I32, (TILE, rows_per), 1)
    pieces = [dict(j=j) for j in range(LOCAL_ROWS // rows_per)]
    acc = [jnp.zeros(o_ref.shape, F32)]

    def weigh(q):
        r = (lane + q["j"] * rows_per).astype(F32)
        w = jnp.zeros((TILE, rows_per), F32)
        for k in range(TOP_K):
            w = w + jnp.where(lp_col[:, k:k + 1] == r, tw_col[:, k:k + 1], 0.0)
        q["w"] = w.astype(BF16)

    def mix(q):
        rows = lo_ref[q["j"] * rows_per:(q["j"] + 1) * rows_per, :].astype(BF16)
        acc[0] = acc[0] + _dot(q.pop("w"), rows)

    _skewed(pieces, ((mix, 1), (weigh, 0)))
    o_ref[...] = _layer_norm(DEEPNORM_ALPHA * h_ref[...] + acc[0], lng_ref[...], lnb_ref[...])


def _combine_call(lo, lpos_t, tw_t, h, lng, lnb):
    t_tok, d = h.shape
    row8 = pl.BlockSpec((8, TILE), lambda i: (0, i))
    return pl.pallas_call(
        _combine_kernel,
        out_shape=jax.ShapeDtypeStruct((t_tok, d), F32),
        grid=(t_tok // TILE,),
        in_specs=[pl.BlockSpec((LOCAL_ROWS, d), lambda i: (i, 0)), row8, row8,
                  pl.BlockSpec((TILE, d), lambda i: (i, 0)),
                  pl.BlockSpec((1, d), lambda i: (0, 0)),
                  pl.BlockSpec((1, d), lambda i: (0, 0))],
        out_specs=pl.BlockSpec((TILE, d), lambda i: (i, 0)),
        compiler_params=pltpu.CompilerParams(dimension_semantics=("arbitrary",),
                                             vmem_limit_bytes=VMEM_LIMIT),
        name="moe_combine_ln",
    )(lo, lpos_t, tw_t, h, lng, lnb)


def _layer(x2, pos, batch, seq, w_in, b_gate, w_branch_sb, w_branch_moba, w_out, ln_mix_g, ln_mix_b,
           w_router, b_router, w_gate_up, b_gate_up, w_down, b_down, ln_ffn_g, ln_ffn_b):
    t_tok, d = x2.shape
    nb = seq // KV_BLOCK
    qkv_w = 6 * WIDTH
    n_tiles = t_tok // TILE

    w_qkv = w_in[:, :qkv_w].astype(BF16)
    w_g = w_in[:, qkv_w:].astype(BF16)
    pos3 = pos.astype(F32).reshape(n_tiles, 1, TILE)
    invf = (ROPE_THETA ** (-jnp.arange(ROPE_HALF, dtype=F32) / ROPE_HALF)).reshape(ROPE_HALF, 1)
    ki = jnp.arange(KV_BLOCK)
    tri = jnp.where(ki[None, :] > ki[:, None], -1.0, 0.0).astype(BF16)
    avg = jnp.where(jnp.arange(seq)[None, :] // KV_BLOCK == jnp.arange(nb)[:, None],
                    1.0 / KV_BLOCK, 0.0).astype(BF16)

    qt_sb, k_sb, vt_sb, qt_mb, k_mb, vt_mb = _qkv_call(x2, pos3, invf, w_qkv, TILE)
    a_sb = _sb_call(qt_sb, k_sb, vt_sb, tri, batch, seq)
    a_mb = _moba_call(qt_mb, k_mb, vt_mb, avg, batch, seq)

    w_r = w_router.T.astype(BF16)
    b_r = jnp.broadcast_to(b_router[:, None], (N_EXPERTS, LANES))
    h, ls, lpos_t, tw_t, cnts = _mixer_call(
        x2, a_sb, a_mb, w_g, b_gate.reshape(1, -1), w_branch_sb.astype(BF16), w_branch_moba.astype(BF16),
        w_out.astype(BF16), ln_mix_g.reshape(1, -1), ln_mix_b.reshape(1, -1), w_r, b_r)

    max_rows = t_tok * TOP_K + n_tiles * N_EXPERTS * (CHUNK - 1)
    n_blocks = -(-max_rows // EXPERT_ROWS) + N_EXPERTS
    block_expert, next_expert, ctl, n_used, table, tail = _plan_call(cnts[:, :, 0].reshape(-1), n_tiles, n_blocks)

    f = w_down.shape[1]
    lo = _expert_call(block_expert, next_expert, ctl, n_used, table, tail, ls, w_gate_up, w_down,
                      b_gate_up[:, 0::2].reshape(N_EXPERTS, 1, f), b_gate_up[:, 1::2].reshape(N_EXPERTS, 1, f),
                      b_down.reshape(N_EXPERTS, 1, d), n_tiles)
    return _combine_call(lo, lpos_t, tw_t, h, ln_ffn_g.reshape(1, -1), ln_ffn_b.reshape(1, -1))


def kernel(x, positions, w_in, b_gate, w_branch_sb, w_branch_moba, w_out, ln_mix_g, ln_mix_b, w_router,
           b_router, w_gate_up, b_gate_up, w_down, b_down, ln_ffn_g, ln_ffn_b):
    batch, seq, d = x.shape
    h = x.reshape(batch * seq, d)
    pos = positions.reshape(batch * seq)
    for layer in range(w_in.shape[0]):
        h = _layer(h, pos, batch, seq, w_in[layer], b_gate[layer], w_branch_sb[layer], w_branch_moba[layer],
                   w_out[layer], ln_mix_g[layer], ln_mix_b[layer], w_router[layer], b_router[layer],
                   w_gate_up[layer], b_gate_up[layer], w_down[layer], b_down[layer],
                   ln_ffn_g[layer], ln_ffn_b[layer])
    return h.reshape(batch, seq, d)
```

```python
import functools

import jax
import jax.numpy as jnp
from jax import lax
from jax.experimental import pallas as pl
from jax.experimental.pallas import tpu as pltpu

F32 = jnp.float32
BF16 = jnp.bfloat16
I32 = jnp.int32

HEAD_DIM = 64
N_HEADS = 8
WIDTH = N_HEADS * HEAD_DIM
LANES = 128
SUBLANES = 8
N_PAIRS = WIDTH // LANES
KV_BLOCK = 256
MOBA_TOPK = 3
ROPE_THETA = 500000.0
ROPE_HALF = 8
N_EXPERTS = 32
TOP_K = 4
SWIGLU_LIMIT = 7.0
SWIGLU_ALPHA = 1.702
EXPERT_ROWS = 512
MXU_ROWS = 256
TILE = 512
CHUNK = 8
LOCAL_ROWS = TILE * TOP_K + N_EXPERTS * CHUNK
CHUNKS_PER_BLOCK = EXPERT_ROWS // CHUNK
ZERO_ROW = LOCAL_ROWS - CHUNK
LN_EPS = 1e-5
DEPTH = 1
DEEPNORM_ALPHA = (2 * DEPTH) ** 0.25
QK_SCALE = HEAD_DIM ** -0.5
NEG_INF = float("-inf")
VMEM_LIMIT = 56 * 1024 * 1024


def _dot(a, b):
    return jnp.dot(a, b, preferred_element_type=F32)


def _qkv_kernel(x_ref, pos_ref, invf_ref, w_ref,
                qt_sb_ref, k_sb_ref, vt_sb_ref, qt_mb_ref, k_mb_ref, vt_mb_ref, *, tm):
    xb = x_ref[...].astype(BF16)
    ang = invf_ref[...] * pos_ref[0]
    cos, sin = jnp.cos(ang), jnp.sin(ang)

    def rope_t(t):
        parts = []
        for base in (0, HEAD_DIM):
            x1 = t[base:base + ROPE_HALF]
            x2 = t[base + ROPE_HALF:base + 2 * ROPE_HALF]
            parts += [x1 * cos - x2 * sin, x2 * cos + x1 * sin, t[base + 2 * ROPE_HALF:base + HEAD_DIM]]
        return jnp.concatenate(parts, axis=0)

    def store_t(ref, p, t):
        tb = t.astype(BF16)
        for blk in range(tm // KV_BLOCK):
            ref[blk, p] = tb[:, blk * KV_BLOCK:(blk + 1) * KV_BLOCK]

    for sec in range(6):
        for half in range(2):
            c0 = sec * WIDTH + half * 2 * LANES
            r = _dot(xb, w_ref[:, c0:c0 + 2 * LANES])
            for q in range(2):
                p = half * 2 + q
                t = r[:, q * LANES:(q + 1) * LANES]
                if sec == 0:
                    store_t(qt_sb_ref, p, (t * QK_SCALE).T)
                elif sec == 1:
                    k_sb_ref[:, p * LANES:(p + 1) * LANES] = t.astype(BF16)
                elif sec == 2:
                    store_t(vt_sb_ref, p, t.T)
                elif sec == 3:
                    store_t(qt_mb_ref, p, rope_t(t.T) * QK_SCALE)
                elif sec == 4:
                    k_mb_ref[:, p * LANES:(p + 1) * LANES] = rope_t(t.T).T.astype(BF16)
                else:
                    store_t(vt_mb_ref, p, t.T)


def _qkv_call(x2, pos3, invf, w_qkv, tm):
    t_tok, d = x2.shape
    nblk = t_tok // KV_BLOCK
    bpt = tm // KV_BLOCK
    t_shape = jax.ShapeDtypeStruct((nblk, N_PAIRS, LANES, KV_BLOCK), BF16)
    n_shape = jax.ShapeDtypeStruct((t_tok, WIDTH), BF16)
    t_spec = pl.BlockSpec((bpt, N_PAIRS, LANES, KV_BLOCK), lambda i: (i, 0, 0, 0))
    n_spec = pl.BlockSpec((tm, WIDTH), lambda i: (i, 0))
    return pl.pallas_call(
        functools.partial(_qkv_kernel, tm=tm),
        out_shape=(t_shape, n_shape, t_shape, t_shape, n_shape, t_shape),
        grid=(t_tok // tm,),
        in_specs=[pl.BlockSpec((tm, d), lambda i: (i, 0)),
                  pl.BlockSpec((1, 1, tm), lambda i: (i, 0, 0)),
                  pl.BlockSpec((ROPE_HALF, 1), lambda i: (0, 0)),
                  pl.BlockSpec((d, 6 * WIDTH), lambda i: (0, 0))],
        out_specs=(t_spec, n_spec, t_spec, t_spec, n_spec, t_spec),
        compiler_params=pltpu.CompilerParams(dimension_semantics=("arbitrary",),
                                             vmem_limit_bytes=VMEM_LIMIT),
        name="qkv_proj",
    )(x2, pos3, invf, w_qkv)


def _head_rows(h):
    row = lax.broadcasted_iota(I32, (LANES, KV_BLOCK), 0)
    return (row >= HEAD_DIM * h) & (row < HEAD_DIM * (h + 1))


def _tile_specs(seq):
    nb = seq // KV_BLOCK
    tiles_spec = pl.BlockSpec((nb, None, LANES, KV_BLOCK), lambda b, p, i: (b, p, 0, 0))
    col_spec = pl.BlockSpec((seq, LANES), lambda b, p, i: (b, p))
    return tiles_spec, col_spec


def _chain_list(i, nb, past_block):
    tiles = (i, nb - 1 - i)
    chains = [dict(j=tiles[a], a=a, h=h, diagonal=True, first=None) for a in range(2) for h in range(2)]
    for t in range(nb - 1):
        first = t < i
        for h in range(2):
            chains.append(dict(j=past_block(first, t), a=jnp.where(first, 0, 1), h=h, diagonal=False, first=first))
    return tiles, chains


def _skewed(chains, stages):
    n = len(chains)
    for slot in range(n + max(lag for _, lag in stages)):
        for stage, lag in stages:
            if 0 <= slot - lag < n:
                stage(chains[slot - lag])


def _sb_kernel(qt_ref, k_ref, vt_ref, tri_ref, o_ref, q_scr, acc_scr):
    i = pl.program_id(2)
    nb = qt_ref.shape[0]
    tiles, chains = _chain_list(i, nb, lambda first, t: jnp.where(first, i - 1 - t, nb - 2 - t))
    for a in range(2):
        qt = qt_ref[tiles[a]]
        for h in range(2):
            q_scr[a, h] = jnp.where(_head_rows(h), qt, jnp.zeros_like(qt))
    acc_scr[...] = jnp.zeros_like(acc_scr)
    key = lax.broadcasted_iota(I32, (KV_BLOCK, KV_BLOCK), 0)
    qry = lax.broadcasted_iota(I32, (KV_BLOCK, KV_BLOCK), 1)
    past = key < qry
    tri = tri_ref[...]

    zero = jnp.zeros((1, KV_BLOCK), F32)
    carries = [[zero, zero], [zero, zero]]

    def scores(c):
        kb = k_ref[pl.ds(pl.multiple_of(c["j"] * KV_BLOCK, KV_BLOCK), KV_BLOCK), :]
        c["z"] = _dot(kb, q_scr[c["a"], c["h"]])

    def softplus(c):
        z = c.pop("z")
        sp = jnp.maximum(z, 0.0) + jnp.log(1.0 + jnp.exp(-jnp.abs(z)))
        spm = jnp.where(past, sp, 0.0) if c["diagonal"] else sp
        c["log_beta"] = z - sp
        c["spm"] = spm.astype(BF16)
        c["sp0"] = spm[0:1, :]

    def suffix(c):
        c["after"] = _dot(tri, c.pop("spm"))

    def weights(c):
        after = c.pop("after")
        w = jnp.exp(c.pop("log_beta") + after)
        if c["diagonal"]:
            w = jnp.where(past, w, 0.0)
        c["w"] = w.astype(BF16)
        total = after[0:1, :] - c.pop("sp0")
        h, first = c["h"], c["first"]
        if first is None:
            c["carry"] = zero
            carries[c["a"]][h] = total
        else:
            c["carry"] = jnp.where(first, carries[0][h], carries[1][h])
            cout = c["carry"] + total
            carries[0][h] = jnp.where(first, cout, carries[0][h])
            carries[1][h] = jnp.where(first, carries[1][h], cout)

    def values(c):
        rows = slice(c["h"] * HEAD_DIM, (c["h"] + 1) * HEAD_DIM)
        pv = _dot(vt_ref[c["j"], rows, :], c.pop("w")) * jnp.exp(c.pop("carry"))
        acc_scr[c["a"], rows, :] += pv

    _skewed(chains, ((scores, 0), (suffix, 2), (values, 4), (softplus, 1), (weights, 3)))
    for a in range(2):
        o_ref[pl.ds(pl.multiple_of(tiles[a] * KV_BLOCK, KV_BLOCK), KV_BLOCK), :] = acc_scr[a].T.astype(BF16)


def _sb_call(qt, k, vt, tri, batch, seq):
    nb = seq // KV_BLOCK
    tiles_spec, col_spec = _tile_specs(seq)
    return pl.pallas_call(
        _sb_kernel,
        out_shape=jax.ShapeDtypeStruct((batch * seq, WIDTH), BF16),
        grid=(batch, N_PAIRS, nb // 2),
        in_specs=[tiles_spec, col_spec, tiles_spec,
                  pl.BlockSpec((KV_BLOCK, KV_BLOCK), lambda b, p, i: (0, 0))],
        out_specs=col_spec,
        scratch_shapes=[pltpu.VMEM((2, 2, LANES, KV_BLOCK), BF16), pltpu.VMEM((2, LANES, KV_BLOCK), F32)],
        compiler_params=pltpu.CompilerParams(
            dimension_semantics=("arbitrary", "arbitrary", "arbitrary"), vmem_limit_bytes=VMEM_LIMIT),
        name="stickbreak_attn",
    )(qt, k, vt, tri)


def _moba_kernel(qt_ref, k_ref, vt_ref, avg_ref, o_ref, km_ref, bias_scr, q_scr, acc_scr, s_scr, den_scr):
    i = pl.program_id(2)
    nb = qt_ref.shape[0]

    @pl.when(i == 0)
    def _():
        km_ref[...] = _dot(avg_ref[...], k_ref[...])

    tiles, chains = _chain_list(i, nb, lambda first, t: jnp.where(first, t, t - i))
    key = lax.broadcasted_iota(I32, (KV_BLOCK, KV_BLOCK), 0)
    qry = lax.broadcasted_iota(I32, (KV_BLOCK, KV_BLOCK), 1)
    causal = key <= qry
    blk = lax.broadcasted_iota(I32, (nb, KV_BLOCK), 0)
    km = km_ref[...].astype(BF16)
    for a in range(2):
        qt = qt_ref[tiles[a]]
        valid = blk < tiles[a]
        for h in range(2):
            qth = jnp.where(_head_rows(h), qt, jnp.zeros_like(qt))
            q_scr[a, h] = qth
            g = jnp.where(valid, _dot(km, qth), NEG_INF)
            rank = jnp.zeros((nb, KV_BLOCK), F32)
            for jp in range(nb):
                gj = g[jp:jp + 1, :]
                better = jnp.where(gj > g, 1.0, jnp.where(gj == g, jnp.where(blk > jp, 1.0, 0.0), 0.0))
                rank = rank + better
            bias_scr[a, h] = jnp.where(valid, jnp.where(rank < MOBA_TOPK, 0.0, NEG_INF), NEG_INF)

    for n, c in enumerate(chains):
        c["n"] = n
    low = jnp.full((SUBLANES, KV_BLOCK), NEG_INF, F32)
    maxima = [[low, low], [low, low]]
    acc_scr[...] = jnp.zeros_like(acc_scr)
    den_scr[...] = jnp.zeros_like(den_scr)
    ones = jnp.ones((2 * SUBLANES, KV_BLOCK), BF16)

    def scores(c):
        kb = k_ref[pl.ds(pl.multiple_of(c["j"] * KV_BLOCK, KV_BLOCK), KV_BLOCK), :]
        c["s"] = _dot(kb, q_scr[c["a"], c["h"]])

    def mask(c):
        h, first = c["h"], c["first"]
        if first is None:
            s = jnp.where(causal, c.pop("s"), NEG_INF)
        else:
            s = c.pop("s") + bias_scr[c["a"], h, pl.ds(c["j"], 1), :]
        s_scr[c["n"]] = s
        top = jnp.max(s.reshape(KV_BLOCK // SUBLANES, SUBLANES, KV_BLOCK), axis=0)
        if first is None:
            maxima[c["a"]][h] = top
        else:
            maxima[0][h] = jnp.where(first, jnp.maximum(maxima[0][h], top), maxima[0][h])
            maxima[1][h] = jnp.where(first, maxima[1][h], jnp.maximum(maxima[1][h], top))

    _skewed(chains, ((scores, 0), (mask, 1)))
    maxima = [[jnp.max(maxima[a][h], axis=0, keepdims=True) for h in range(2)] for a in range(2)]

    def weights(c):
        h, first = c["h"], c["first"]
        m = maxima[c["a"]][h] if first is None else jnp.where(first, maxima[0][h], maxima[1][h])
        c["p"] = jnp.exp(s_scr[c["n"]] - m).astype(BF16)

    def values(c):
        rows = slice(c["h"] * HEAD_DIM, (c["h"] + 1) * HEAD_DIM)
        pv = _dot(jnp.concatenate([vt_ref[c["j"], rows, :], ones], axis=0), c.pop("p"))
        acc_scr[c["a"], rows, :] += pv[:HEAD_DIM]
        den_scr[c["a"], c["h"]] += pv[HEAD_DIM:]

    _skewed(chains, ((values, 1), (weights, 0)))
    for a in range(2):
        for h in range(2):
            rows = slice(h * HEAD_DIM, (h + 1) * HEAD_DIM)
            acc_scr[a, rows, :] = acc_scr[a, rows, :] / den_scr[a, h, 0:1, :]
        o_ref[pl.ds(pl.multiple_of(tiles[a] * KV_BLOCK, KV_BLOCK), KV_BLOCK), :] = acc_scr[a].T.astype(BF16)


def _moba_call(qt, k, vt, avg, batch, seq):
    nb = seq // KV_BLOCK
    tiles_spec, col_spec = _tile_specs(seq)
    return pl.pallas_call(
        _moba_kernel,
        out_shape=jax.ShapeDtypeStruct((batch * seq, WIDTH), BF16),
        grid=(batch, N_PAIRS, nb // 2),
        in_specs=[tiles_spec, col_spec, tiles_spec, pl.BlockSpec((nb, seq), lambda b, p, i: (0, 0))],
        out_specs=col_spec,
        scratch_shapes=[pltpu.VMEM((nb, LANES), F32), pltpu.VMEM((2, 2, nb, KV_BLOCK), F32),
                        pltpu.VMEM((2, 2, LANES, KV_BLOCK), BF16), pltpu.VMEM((2, LANES, KV_BLOCK), F32),
                        pltpu.VMEM((2 * (nb + 1), KV_BLOCK, KV_BLOCK), F32),
                        pltpu.VMEM((2, 2, 2 * SUBLANES, KV_BLOCK), F32)],
        compiler_params=pltpu.CompilerParams(
            dimension_semantics=("arbitrary", "arbitrary", "arbitrary"), vmem_limit_bytes=VMEM_LIMIT),
        name="moba_attn",
    )(qt, k, vt, avg)


def _layer_norm(r, g, b):
    mu = jnp.mean(r, axis=-1, keepdims=True)
    d = r - mu
    var = jnp.mean(d * d, axis=-1, keepdims=True)
    return d * lax.rsqrt(var + LN_EPS) * g + b


def _mixer_kernel(x_ref, asb_ref, amb_ref, wg_ref, bg_ref, wbs_ref, wbm_ref, wo_ref, lng_ref, lnb_ref,
                  wr_ref, br_ref, h_ref, ls_ref, lpos_ref, tw_ref, cnts_ref, *, d):
    rows_per = MXU_ROWS
    chains = [dict(rows=slice(c * rows_per, (c + 1) * rows_per)) for c in range(TILE // rows_per)]
    expert = lax.broadcasted_iota(I32, (N_EXPERTS, rows_per), 0)
    slot8 = lax.broadcasted_iota(I32, (8, rows_per), 0)
    rt = lax.broadcasted_iota(I32, (rows_per, rows_per), 0)
    ct = lax.broadcasted_iota(I32, (rows_per, rows_per), 1)
    earlier = jnp.where(rt < ct, 1.0, 0.0).astype(BF16)
    seen = [jnp.zeros((N_EXPERTS, LANES), F32)]

    def project(c):
        c["gp"] = _dot(x_ref[c["rows"], :].astype(BF16), wg_ref[...])
        c["ysb"] = _dot(asb_ref[c["rows"], :], wbs_ref[...])
        c["ymb"] = _dot(amb_ref[c["rows"], :], wbm_ref[...])

    def gate(c):
        g = jax.nn.sigmoid(c.pop("gp") + bg_ref[...])
        c["mixed"] = (g[:, :d] * c.pop("ysb") + g[:, d:] * c.pop("ymb")).astype(BF16)

    def out_proj(c):
        c["mix"] = _dot(c.pop("mixed"), wo_ref[...])

    def norm(c):
        h = _layer_norm(DEEPNORM_ALPHA * x_ref[c["rows"], :] + c.pop("mix"), lng_ref[...], lnb_ref[...])
        h_ref[c["rows"], :] = h
        c["hb"] = h.astype(BF16)

    def route(c):
        nt = (((1,), (1,)), ((), ()))
        c["logits"] = lax.dot_general(wr_ref[...], c["hb"], nt, preferred_element_type=F32)

    def top_k(c):
        bias = br_ref[...]
        logits = c.pop("logits") + jnp.concatenate([bias] * (rows_per // LANES), axis=1)
        vals, idxs = [], []
        member = jnp.zeros((N_EXPERTS, rows_per), F32)
        for _ in range(TOP_K):
            mx = jnp.max(logits, axis=0, keepdims=True)
            ik = jnp.min(jnp.where(logits == mx, expert, N_EXPERTS), axis=0, keepdims=True)
            hit = expert == ik
            vals.append(mx)
            idxs.append(ik)
            member = jnp.where(hit, 1.0, member)
            logits = jnp.where(hit, NEG_INF, logits)
        es = [jnp.exp(v - vals[0]) for v in vals]
        den = es[0] + es[1] + es[2] + es[3]
        c["tw"] = [e / den for e in es]
        c["idxs"] = idxs
        c["member"] = member

    def count(c):
        member = c.pop("member")
        c["rank"] = _dot(member.astype(BF16), earlier) + jnp.concatenate([seen[0]] * (rows_per // LANES), axis=1)
        seen[0] = seen[0] + jnp.sum(member, axis=1, keepdims=True)

    _skewed(chains, ((project, 0), (out_proj, 2), (route, 4), (count, 6), (gate, 1), (norm, 3), (top_k, 5)))

    counts = seen[0]
    cnts_ref[...] = counts.astype(I32)
    seg = jnp.ceil(counts * (1.0 / CHUNK)) * CHUNK
    e_r = lax.broadcasted_iota(I32, (N_EXPERTS, N_EXPERTS), 0)
    e_c = lax.broadcasted_iota(I32, (N_EXPERTS, N_EXPERTS), 1)
    loff = _dot(jnp.where(e_c < e_r, 1.0, 0.0).astype(BF16), seg.astype(BF16))
    loff = jnp.concatenate([loff] * (rows_per // LANES), axis=1)
    lpos, tws = [], []
    for c in chains:
        where_to = c.pop("rank") + loff
        lp = jnp.full((8, rows_per), -1.0, F32)
        tw = jnp.zeros((8, rows_per), F32)
        for k, ik in enumerate(c.pop("idxs")):
            pk = jnp.sum(jnp.where(expert == ik, where_to, 0.0), axis=0, keepdims=True)
            lp = jnp.where(slot8 == k, pk, lp)
            tw = jnp.where(slot8 == k, c["tw"][k], tw)
        lpos.append(lp)
        tws.append(tw)
    lpos = jnp.concatenate(lpos, axis=1)
    lpos_ref[...] = lpos.astype(I32)
    tw_ref[...] = jnp.concatenate(tws, axis=1)

    hb = jnp.concatenate([c["hb"] for c in chains], axis=0)
    pieces = [dict(j=j) for j in range(LOCAL_ROWS // rows_per)]
    row = lax.broadcasted_iota(I32, (rows_per, TILE), 0)

    def select(q):
        r = (row + q["j"] * rows_per).astype(F32)
        sel = jnp.zeros((rows_per, TILE), F32)
        for k in range(TOP_K):
            sel = sel + jnp.where(lpos[k:k + 1, :] == r, 1.0, 0.0)
        q["sel"] = sel.astype(BF16)

    def place(q):
        ls_ref[q["j"] * rows_per:(q["j"] + 1) * rows_per, :] = _dot(q.pop("sel"), hb)

    _skewed(pieces, ((place, 1), (select, 0)))


def _mixer_call(x2, a_sb, a_mb, wg, bg, wbs, wbm, wo, lng, lnb, wr, br):
    t_tok, d = x2.shape
    n_tiles = t_tok // TILE
    const = lambda shape: pl.BlockSpec(shape, lambda i: (0,) * len(shape))
    row8 = pl.BlockSpec((8, TILE), lambda i: (0, i))
    return pl.pallas_call(
        functools.partial(_mixer_kernel, d=d),
        out_shape=(jax.ShapeDtypeStruct((t_tok, d), F32),
                   jax.ShapeDtypeStruct((n_tiles * LOCAL_ROWS, d), F32),
                   jax.ShapeDtypeStruct((8, t_tok), I32),
                   jax.ShapeDtypeStruct((8, t_tok), F32),
                   jax.ShapeDtypeStruct((n_tiles, N_EXPERTS, LANES), I32)),
        grid=(n_tiles,),
        in_specs=[pl.BlockSpec((TILE, d), lambda i: (i, 0)),
                  pl.BlockSpec((TILE, WIDTH), lambda i: (i, 0)),
                  pl.BlockSpec((TILE, WIDTH), lambda i: (i, 0)),
                  const((d, 2 * d)), const((1, 2 * d)),
                  const((WIDTH, d)), const((WIDTH, d)), const((d, d)),
                  const((1, d)), const((1, d)),
                  const((N_EXPERTS, d)), const((N_EXPERTS, LANES))],
        out_specs=(pl.BlockSpec((TILE, d), lambda i: (i, 0)),
                   pl.BlockSpec((LOCAL_ROWS, d), lambda i: (i, 0)),
                   row8, row8,
                   pl.BlockSpec((None, N_EXPERTS, LANES), lambda i: (i, 0, 0))),
        compiler_params=pltpu.CompilerParams(dimension_semantics=("arbitrary",),
                                             vmem_limit_bytes=VMEM_LIMIT),
        name="mixer_ln_router",
    )(x2, a_sb, a_mb, wg, bg, wbs, wbm, wo, lng, lnb, wr, br)


CTL_FIRST = 1
CTL_LAST = 2
CTL_ODD = 4


def _plan_kernel(cnt_ref, be_ref, nx_ref, ctl_ref, nu_ref, tail_ref, table_ref,
                 size_ref, follow_ref, first_ref, count_ref, base_ref, *, n_tiles):
    def seg(t, e):
        return (cnt_ref[t * N_EXPERTS + e] + CHUNK - 1) // CHUNK * CHUNK

    def size_of(e, _):
        size_ref[e] = lax.fori_loop(0, n_tiles, lambda t, a: a + seg(t, e), 0)
        return 0

    lax.fori_loop(0, N_EXPERTS, size_of, 0)

    def nonempty_after(t, nxt):
        e = N_EXPERTS - 1 - t
        follow_ref[e] = nxt
        return jnp.where(size_ref[e] > 0, e, nxt)

    lax.fori_loop(0, N_EXPERTS, nonempty_after, -1)

    def clear_block(b, _):
        be_ref[b] = 0
        nx_ref[b] = 0
        ctl_ref[b] = 0
        return 0

    lax.fori_loop(0, be_ref.shape[0], clear_block, 0)

    def clear_tile(t, _):
        tail_ref[t] = 0
        return 0

    lax.fori_loop(0, n_tiles, clear_tile, 0)

    def blocks_of(e, state):
        b0, odd = state
        n = (size_ref[e] + EXPERT_ROWS - 1) // EXPERT_ROWS
        follows = follow_ref[e] >= 0

        def block(j, _):
            be_ref[b0 + j] = e
            nx_ref[b0 + j] = jnp.maximum(follow_ref[e], 0)
            ctl_ref[b0 + j] = (jnp.where(follows & (j == 0), CTL_FIRST, 0)
                               + jnp.where(follows & (j == n - 1), CTL_LAST, 0) + odd * CTL_ODD)
            return 0

        lax.fori_loop(0, n, block, 0)

        def segment(t, pos):
            lo = tail_ref[t]
            chunks = seg(t, e) // CHUNK
            first_ref[t * N_EXPERTS + e] = pos
            count_ref[t * N_EXPERTS + e] = chunks
            base_ref[t * N_EXPERTS + e] = t * LOCAL_ROWS + lo
            tail_ref[t] = lo + chunks * CHUNK
            return pos + chunks

        lax.fori_loop(0, n_tiles, segment, b0 * CHUNKS_PER_BLOCK)
        return b0 + n, jnp.where(n > 0, 1 - odd, odd)

    n_used, _ = lax.fori_loop(0, N_EXPERTS, blocks_of, (0, 0))
    nu_ref[0] = n_used

    slot = (lax.broadcasted_iota(I32, table_ref.shape, 0) * LANES
            + lax.broadcasted_iota(I32, table_ref.shape, 1))

    def fill(s, table):
        first = first_ref[s]
        inside = (slot >= first) & (slot < first + count_ref[s])
        return jnp.where(inside, base_ref[s] + (slot - first) * CHUNK, table)

    table_ref[...] = lax.fori_loop(0, n_tiles * N_EXPERTS, fill, jnp.full(table_ref.shape, -1, I32))


def _plan_call(tile_counts, n_tiles, n_blocks):
    smem = pl.BlockSpec(memory_space=pltpu.SMEM)
    blocks = jax.ShapeDtypeStruct((n_blocks,), I32)
    segments = pltpu.SMEM((n_tiles * N_EXPERTS,), I32)
    return pl.pallas_call(
        functools.partial(_plan_kernel, n_tiles=n_tiles),
        out_shape=(blocks, blocks, blocks, jax.ShapeDtypeStruct((1,), I32),
                   jax.ShapeDtypeStruct((n_tiles,), I32),
                   jax.ShapeDtypeStruct((n_blocks * CHUNKS_PER_BLOCK // LANES, LANES), I32)),
        in_specs=[smem],
        out_specs=(smem,) * 5 + (pl.BlockSpec(memory_space=pltpu.VMEM),),
        scratch_shapes=[pltpu.SMEM((N_EXPERTS,), I32), pltpu.SMEM((N_EXPERTS,), I32), segments, segments, segments],
        name="moe_plan",
    )(tile_counts)


def _sink_pieces():
    return [(off, min(EXPERT_ROWS, LOCAL_ROWS - off)) for off in range(0, LOCAL_ROWS, EXPERT_ROWS)]


def _expert_kernel(be_ref, nx_ref, ctl_ref, nu_ref, table_ref, tail_ref,
                   ls_hbm, wgu_hbm, wd_hbm, bg_ref, bu_ref, bd_ref, lo_hbm,
                   x_buf, o_buf, wgu_stage, wd_stage, wgt0, wut0, wdb0, wgt1, wut1, wdb1, tmp_ref,
                   wsems, xsems, osems, zsem, *, n_tiles):
    i = pl.program_id(0)
    slot = i % 2
    n_used = nu_ref[0]
    live = i < n_used
    ctl = ctl_ref[i]
    sets = ((wgt0, wut0, wdb0), (wgt1, wut1, wdb1))
    chunk = 2 * LANES
    spare = n_tiles * LOCAL_ROWS

    def fetch(e):
        return (pltpu.make_async_copy(wgu_hbm.at[e], wgu_stage, wsems.at[0]),
                pltpu.make_async_copy(wd_hbm.at[e], wd_stage, wsems.at[1]))

    def prepare(wgt_ref, wut_ref, wdb_ref):
        for c in range(wgu_stage.shape[1] // chunk):
            rows = slice(c * LANES, (c + 1) * LANES)
            for k in range(wgu_stage.shape[0] // LANES):
                cols = slice(k * LANES, (k + 1) * LANES)
                tslot = (c * (wgu_stage.shape[0] // LANES) + k) % tmp_ref.shape[0]
                tmp_ref[tslot] = wgu_stage[cols, c * chunk:(c + 1) * chunk].T
                wgt_ref[rows, cols] = tmp_ref[tslot, pl.ds(0, LANES, stride=2), :].astype(BF16)
                wut_ref[rows, cols] = tmp_ref[tslot, pl.ds(1, LANES, stride=2), :].astype(BF16)
        wdb_ref[...] = wd_stage[...].astype(BF16)

    def rows_in(b, s):
        def one(j, _):
            a = table_ref[b * CHUNKS_PER_BLOCK + j]
            src = pl.multiple_of(jnp.where(a >= 0, a, ZERO_ROW), CHUNK)
            dst = x_buf.at[s, pl.ds(pl.multiple_of(j * CHUNK, CHUNK), CHUNK)]
            pltpu.make_async_copy(ls_hbm.at[pl.ds(src, CHUNK)], dst, xsems.at[s]).start()
            return 0
        lax.fori_loop(0, CHUNKS_PER_BLOCK, one, 0)

    def rows_out(b, s):
        def one(j, _):
            a = table_ref[b * CHUNKS_PER_BLOCK + j]
            pad = spare + s * EXPERT_ROWS + j * CHUNK
            dst = pl.multiple_of(jnp.where(a >= 0, a, pad), CHUNK)
            src = o_buf.at[s, pl.ds(pl.multiple_of(j * CHUNK, CHUNK), CHUNK)]
            pltpu.make_async_copy(src, lo_hbm.at[pl.ds(dst, CHUNK)], osems.at[s]).start()
            return 0
        lax.fori_loop(0, CHUNKS_PER_BLOCK, one, 0)

    def block_done(buf, sems, s):
        pltpu.make_async_copy(buf.at[s], buf.at[s], sems.at[s]).wait()

    def ffn(wgt_ref, wut_ref, wdb_ref):
        nxt = jnp.minimum(i + 1, pl.num_programs(0) - 1)
        prv = jnp.maximum(i - 1, 0)
        other = 1 - slot
        for j in range(CHUNKS_PER_BLOCK):
            rows = pl.ds(j * CHUNK, CHUNK)
            a = table_ref[nxt * CHUNKS_PER_BLOCK + j]
            src = pl.multiple_of(jnp.where(a >= 0, a, ZERO_ROW), CHUNK)
            pltpu.make_async_copy(ls_hbm.at[pl.ds(src, CHUNK)], x_buf.at[other, rows], xsems.at[other]).start()
            p = jnp.where(i >= 1, table_ref[prv * CHUNKS_PER_BLOCK + j], -1)
            pad = spare + other * EXPERT_ROWS + j * CHUNK
            dst = pl.multiple_of(jnp.where(p >= 0, p, pad), CHUNK)
            pltpu.make_async_copy(o_buf.at[other, rows], lo_hbm.at[pl.ds(dst, CHUNK)], osems.at[other]).start()

        nt = (((1,), (1,)), ((), ()))
        chains = [dict(rows=slice(c * MXU_ROWS, (c + 1) * MXU_ROWS)) for c in range(EXPERT_ROWS // MXU_ROWS)]

        def gate_up(c):
            x = x_buf[slot, c["rows"], :].astype(BF16)
            c["gate"] = lax.dot_general(x, wgt_ref[...], nt, preferred_element_type=F32)
            c["up"] = lax.dot_general(x, wut_ref[...], nt, preferred_element_type=F32)

        def activation(c):
            gate = jnp.minimum(c.pop("gate") + bg_ref[...], SWIGLU_LIMIT)
            up = jnp.clip(c.pop("up") + bu_ref[...], -SWIGLU_LIMIT, SWIGLU_LIMIT)
            c["act"] = ((up + 1.0) * gate * jax.nn.sigmoid(SWIGLU_ALPHA * gate)).astype(BF16)

        def down(c):
            o_buf[slot, c["rows"], :] = _dot(c.pop("act"), wdb_ref[...]) + bd_ref[...]

        _skewed(chains, ((gate_up, 0), (down, 2), (activation, 1)))

    @pl.when(i == 0)
    def _():
        o_buf[...] = jnp.zeros_like(o_buf)

        def tails(start):
            def tile(t, _):
                def one(q, _):
                    dst = pl.multiple_of(t * LOCAL_ROWS + tail_ref[t] + q * CHUNK, CHUNK)
                    cp = pltpu.make_async_copy(o_buf.at[0, pl.ds(0, CHUNK)], lo_hbm.at[pl.ds(dst, CHUNK)], zsem)
                    if start:
                        cp.start()
                    else:
                        cp.wait()
                    return 0
                lax.fori_loop(0, (LOCAL_ROWS - tail_ref[t]) // CHUNK, one, 0)
                return 0
            lax.fori_loop(0, n_tiles, tile, 0)

        sink = [pltpu.make_async_copy(o_buf.at[0, pl.ds(0, n)], lo_hbm.at[pl.ds(spare + off, n)], zsem)
                for off, n in _sink_pieces()]
        tails(True)
        for cp in sink:
            cp.start()
        tails(False)
        for cp in sink:
            cp.wait()
        rows_in(0, 0)
        for cp in fetch(be_ref[0]):
            cp.start()
        for cp in fetch(be_ref[0]):
            cp.wait()
        prepare(*sets[0])

    @pl.when(live & ((ctl & CTL_FIRST) != 0))
    def _():
        for cp in fetch(nx_ref[i]):
            cp.start()

    @pl.when(live)
    def _():
        block_done(x_buf, xsems, slot)

    @pl.when(live & (i >= 1))
    def _():
        block_done(o_buf, osems, slot)

    for odd in range(2):
        mine = live & (((ctl & CTL_ODD) != 0) == bool(odd))

        @pl.when(mine & ((ctl & CTL_LAST) == 0))
        def _(odd=odd):
            ffn(*sets[odd])

        @pl.when(mine & ((ctl & CTL_LAST) != 0))
        def _(odd=odd):
            for cp in fetch(nx_ref[i]):
                cp.wait()
            ffn(*sets[odd])
            prepare(*sets[1 - odd])

    @pl.when(i == n_used - 1)
    def _():
        rows_out(i, slot)
        block_done(o_buf, osems, slot)
        block_done(o_buf, osems, 1 - slot)
        block_done(x_buf, xsems, 1 - slot)


def _expert_call(block_expert, next_expert, ctl, n_used, table, tail, ls, w_gate_up, w_down, bg, bu, bd, n_tiles):
    d = ls.shape[1]
    f = w_down.shape[1]
    n_blocks = block_expert.shape[0]
    bsel = lambda i, be, nx, ctl, nu, tb, tl: (be[i], 0, 0)
    anywhere = pl.BlockSpec(memory_space=pl.ANY)
    prepared = [pltpu.VMEM((f, d), BF16) for _ in range(6)]
    return pl.pallas_call(
        functools.partial(_expert_kernel, n_tiles=n_tiles),
        out_shape=jax.ShapeDtypeStruct(((n_tiles + 1) * LOCAL_ROWS, d), F32),
        grid_spec=pltpu.PrefetchScalarGridSpec(
            num_scalar_prefetch=6,
            grid=(n_blocks,),
            in_specs=[anywhere, anywhere, anywhere,
                      pl.BlockSpec((None, 1, f), bsel),
                      pl.BlockSpec((None, 1, f), bsel),
                      pl.BlockSpec((None, 1, d), bsel)],
            out_specs=anywhere,
            scratch_shapes=[pltpu.VMEM((2, EXPERT_ROWS, d), F32), pltpu.VMEM((2, EXPERT_ROWS, d), F32),
                            pltpu.VMEM((d, 2 * f), F32), pltpu.VMEM((f, d), F32), *prepared,
                            pltpu.VMEM((4, 2 * LANES, LANES), F32),
                            pltpu.SemaphoreType.DMA((2,)), pltpu.SemaphoreType.DMA((2,)),
                            pltpu.SemaphoreType.DMA((2,)), pltpu.SemaphoreType.DMA(())]),
        compiler_params=pltpu.CompilerParams(dimension_semantics=("arbitrary",),
                                             vmem_limit_bytes=VMEM_LIMIT),
        name="expert_ffn",
    )(block_expert, next_expert, ctl, n_used, table, tail, ls, w_gate_up, w_down, bg, bu, bd)


def _combine_kernel(lo_ref, lpos_ref, tw_ref, h_ref, lng_ref, lnb_ref, o_ref):
    rows_per = MXU_ROWS
    fill = jnp.zeros((LANES - 8, TILE), F32)
    lp_col = jnp.concatenate([lpos_ref[...].astype(F32), fill], axis=0).T
    tw_col = jnp.concatenate([tw_ref[...], fill], axis=0).T
    lane = lax.broadcasted_iota(I32, (TILE, rows_per), 1)
    pieces = [dict(j=j) for j in range(LOCAL_ROWS // rows_per)]
    acc = [jnp.zeros(o_ref.shape, F32)]

    def weigh(q):
        r = (lane + q["j"] * rows_per).astype(F32)
        w = jnp.zeros((TILE, rows_per), F32)
        for k in range(TOP_K):
            w = w + jnp.where(lp_col[:, k:k + 1] == r, tw_col[:, k:k + 1], 0.0)
        q["w"] = w.astype(BF16)

    def mix(q):
        rows = lo_ref[q["j"] * rows_per:(q["j"] + 1) * rows_per, :].astype(BF16)
        acc[0] = acc[0] + _dot(q.pop("w"), rows)

    _skewed(pieces, ((mix, 1), (weigh, 0)))
    o_ref[...] = _layer_norm(DEEPNORM_ALPHA * h_ref[...] + acc[0], lng_ref[...], lnb_ref[...])


def _combine_call(lo, lpos_t, tw_t, h, lng, lnb):
    t_tok, d = h.shape
    row8 = pl.BlockSpec((8, TILE), lambda i: (0, i))
    return pl.pallas_call(
        _combine_kernel,
        out_shape=jax.ShapeDtypeStruct((t_tok, d), F32),
        grid=(t_tok // TILE,),
        in_specs=[pl.BlockSpec((LOCAL_ROWS, d), lambda i: (i, 0)), row8, row8,
                  pl.BlockSpec((TILE, d), lambda i: (i, 0)),
                  pl.BlockSpec((1, d), lambda i: (0, 0)),
                  pl.BlockSpec((1, d), lambda i: (0, 0))],
        out_specs=pl.BlockSpec((TILE, d), lambda i: (i, 0)),
        compiler_params=pltpu.CompilerParams(dimension_semantics=("arbitrary",),
                                             vmem_limit_bytes=VMEM_LIMIT),
        name="moe_combine_ln",
    )(lo, lpos_t, tw_t, h, lng, lnb)


def _layer(x2, pos, batch, seq, w_in, b_gate, w_branch_sb, w_branch_moba, w_out, ln_mix_g, ln_mix_b,
           w_router, b_router, w_gate_up, b_gate_up, w_down, b_down, ln_ffn_g, ln_ffn_b):
    t_tok, d = x2.shape
    nb = seq // KV_BLOCK
    qkv_w = 6 * WIDTH
    n_tiles = t_tok // TILE

    w_qkv = w_in[:, :qkv_w].astype(BF16)
    w_g = w_in[:, qkv_w:].astype(BF16)
    pos3 = pos.astype(F32).reshape(n_tiles, 1, TILE)
    invf = (ROPE_THETA ** (-jnp.arange(ROPE_HALF, dtype=F32) / ROPE_HALF)).reshape(ROPE_HALF, 1)
    ki = jnp.arange(KV_BLOCK)
    tri = jnp.where(ki[None, :] > ki[:, None], -1.0, 0.0).astype(BF16)
    avg = jnp.where(jnp.arange(seq)[None, :] // KV_BLOCK == jnp.arange(nb)[:, None],
                    1.0 / KV_BLOCK, 0.0).astype(BF16)

    qt_sb, k_sb, vt_sb, qt_mb, k_mb, vt_mb = _qkv_call(x2, pos3, invf, w_qkv, TILE)
    a_sb = _sb_call(qt_sb, k_sb, vt_sb, tri, batch, seq)
    a_mb = _moba_call(qt_mb, k_mb, vt_mb, avg, batch, seq)

    w_r = w_router.T.astype(BF16)
    b_r = jnp.broadcast_to(b_router[:, None], (N_EXPERTS, LANES))
    h, ls, lpos_t, tw_t, cnts = _mixer_call(
        x2, a_sb, a_mb, w_g, b_gate.reshape(1, -1), w_branch_sb.astype(BF16), w_branch_moba.astype(BF16),
        w_out.astype(BF16), ln_mix_g.reshape(1, -1), ln_mix_b.reshape(1, -1), w_r, b_r)

    max_rows = t_tok * TOP_K + n_tiles * N_EXPERTS * (CHUNK - 1)
    n_blocks = -(-max_rows // EXPERT_ROWS) + N_EXPERTS
    n_blocks = -(-n_blocks // 16) * 16
    block_expert, next_expert, ctl, n_used, tail, table = _plan_call(cnts[:, :, 0].reshape(-1), n_tiles, n_blocks)
    table = table.reshape(-1)

    f = w_down.shape[1]
    lo = _expert_call(block_expert, next_expert, ctl, n_used, table, tail, ls, w_gate_up, w_down,
                      b_gate_up[:, 0::2].reshape(N_EXPERTS, 1, f), b_gate_up[:, 1::2].reshape(N_EXPERTS, 1, f),
                      b_down.reshape(N_EXPERTS, 1, d), n_tiles)
    return _combine_call(lo, lpos_t, tw_t, h, ln_ffn_g.reshape(1, -1), ln_ffn_b.reshape(1, -1))


def kernel(x, positions, w_in, b_gate, w_branch_sb, w_branch_moba, w_out, ln_mix_g, ln_mix_b, w_router,
           b_router, w_gate_up, b_gate_up, w_down, b_down, ln_ffn_g, ln_ffn_b):
    batch, seq, d = x.shape
    h = x.reshape(batch * seq, d)
    pos = positions.reshape(batch * seq)
    for layer in range(w_in.shape[0]):
        h = _layer(h, pos, batch, seq, w_in[layer], b_gate[layer], w_branch_sb[layer], w_branch_moba[layer],
                   w_out[layer], ln_mix_g[layer], ln_mix_b[layer], w_router[layer], b_router[layer],
                   w_gate_up[layer], b_gate_up[layer], w_down[layer], b_down[layer],
                   ln_ffn_g[layer], ln_ffn_b[layer])
    return h.reshape(batch, seq, d)
```

```python
import functools

import jax
import jax.numpy as jnp
from jax import lax
from jax.experimental import pallas as pl
from jax.experimental.pallas import tpu as pltpu

F32 = jnp.float32
BF16 = jnp.bfloat16
I32 = jnp.int32

HEAD_DIM = 64
N_HEADS = 8
WIDTH = N_HEADS * HEAD_DIM
LANES = 128
SUBLANES = 8
N_PAIRS = WIDTH // LANES
KV_BLOCK = 256
MOBA_TOPK = 3
ROPE_THETA = 500000.0
ROPE_HALF = 8
N_EXPERTS = 32
TOP_K = 4
SWIGLU_LIMIT = 7.0
SWIGLU_ALPHA = 1.702
EXPERT_ROWS = 512
MXU_ROWS = 256
TILE = 512
CHUNK = 8
LOCAL_ROWS = TILE * TOP_K + N_EXPERTS * CHUNK
CHUNKS_PER_BLOCK = EXPERT_ROWS // CHUNK
ZERO_ROW = LOCAL_ROWS - CHUNK
LN_EPS = 1e-5
DEPTH = 1
DEEPNORM_ALPHA = (2 * DEPTH) ** 0.25
QK_SCALE = HEAD_DIM ** -0.5
NEG_INF = float("-inf")
VMEM_LIMIT = 56 * 1024 * 1024


def _dot(a, b):
    return jnp.dot(a, b, preferred_element_type=F32)


def _qkv_kernel(x_ref, pos_ref, invf_ref, w_ref,
                qt_sb_ref, k_sb_ref, vt_sb_ref, qt_mb_ref, k_mb_ref, vt_mb_ref, *, tm):
    xb = x_ref[...].astype(BF16)
    ang = invf_ref[...] * pos_ref[0]
    cos, sin = jnp.cos(ang), jnp.sin(ang)

    def rope_t(t):
        parts = []
        for base in (0, HEAD_DIM):
            x1 = t[base:base + ROPE_HALF]
            x2 = t[base + ROPE_HALF:base + 2 * ROPE_HALF]
            parts += [x1 * cos - x2 * sin, x2 * cos + x1 * sin, t[base + 2 * ROPE_HALF:base + HEAD_DIM]]
        return jnp.concatenate(parts, axis=0)

    def store_t(ref, p, t):
        tb = t.astype(BF16)
        for blk in range(tm // KV_BLOCK):
            ref[blk, p] = tb[:, blk * KV_BLOCK:(blk + 1) * KV_BLOCK]

    for sec in range(6):
        for half in range(2):
            c0 = sec * WIDTH + half * 2 * LANES
            r = _dot(xb, w_ref[:, c0:c0 + 2 * LANES])
            for q in range(2):
                p = half * 2 + q
                t = r[:, q * LANES:(q + 1) * LANES]
                if sec == 0:
                    store_t(qt_sb_ref, p, (t * QK_SCALE).T)
                elif sec == 1:
                    k_sb_ref[:, p * LANES:(p + 1) * LANES] = t.astype(BF16)
                elif sec == 2:
                    store_t(vt_sb_ref, p, t.T)
                elif sec == 3:
                    store_t(qt_mb_ref, p, rope_t(t.T) * QK_SCALE)
                elif sec == 4:
                    k_mb_ref[:, p * LANES:(p + 1) * LANES] = rope_t(t.T).T.astype(BF16)
                else:
                    store_t(vt_mb_ref, p, t.T)


def _qkv_call(x2, pos3, invf, w_qkv, tm):
    t_tok, d = x2.shape
    nblk = t_tok // KV_BLOCK
    bpt = tm // KV_BLOCK
    t_shape = jax.ShapeDtypeStruct((nblk, N_PAIRS, LANES, KV_BLOCK), BF16)
    n_shape = jax.ShapeDtypeStruct((t_tok, WIDTH), BF16)
    t_spec = pl.BlockSpec((bpt, N_PAIRS, LANES, KV_BLOCK), lambda i: (i, 0, 0, 0))
    n_spec = pl.BlockSpec((tm, WIDTH), lambda i: (i, 0))
    return pl.pallas_call(
        functools.partial(_qkv_kernel, tm=tm),
        out_shape=(t_shape, n_shape, t_shape, t_shape, n_shape, t_shape),
        grid=(t_tok // tm,),
        in_specs=[pl.BlockSpec((tm, d), lambda i: (i, 0)),
                  pl.BlockSpec((1, 1, tm), lambda i: (i, 0, 0)),
                  pl.BlockSpec((ROPE_HALF, 1), lambda i: (0, 0)),
                  pl.BlockSpec((d, 6 * WIDTH), lambda i: (0, 0))],
        out_specs=(t_spec, n_spec, t_spec, t_spec, n_spec, t_spec),
        compiler_params=pltpu.CompilerParams(dimension_semantics=("arbitrary",),
                                             vmem_limit_bytes=VMEM_LIMIT),
        name="qkv_proj",
    )(x2, pos3, invf, w_qkv)


def _head_rows(h):
    row = lax.broadcasted_iota(I32, (LANES, KV_BLOCK), 0)
    return (row >= HEAD_DIM * h) & (row < HEAD_DIM * (h + 1))


def _tile_specs(seq):
    nb = seq // KV_BLOCK
    tiles_spec = pl.BlockSpec((nb, None, LANES, KV_BLOCK), lambda b, p, i: (b, p, 0, 0))
    col_spec = pl.BlockSpec((seq, LANES), lambda b, p, i: (b, p))
    return tiles_spec, col_spec


def _chain_list(i, nb, past_block):
    tiles = (i, nb - 1 - i)
    chains = [dict(j=tiles[a], a=a, h=h, diagonal=True, first=None) for a in range(2) for h in range(2)]
    for t in range(nb - 1):
        first = t < i
        for h in range(2):
            chains.append(dict(j=past_block(first, t), a=jnp.where(first, 0, 1), h=h, diagonal=False, first=first))
    return tiles, chains


def _skewed(chains, stages):
    n = len(chains)
    for slot in range(n + max(lag for _, lag in stages)):
        for stage, lag in stages:
            if 0 <= slot - lag < n:
                stage(chains[slot - lag])


def _sb_kernel(qt_ref, k_ref, vt_ref, tri_ref, o_ref, q_scr, acc_scr):
    i = pl.program_id(2)
    nb = qt_ref.shape[0]
    tiles, chains = _chain_list(i, nb, lambda first, t: jnp.where(first, i - 1 - t, nb - 2 - t))
    for a in range(2):
        qt = qt_ref[tiles[a]]
        for h in range(2):
            q_scr[a, h] = jnp.where(_head_rows(h), qt, jnp.zeros_like(qt))
    acc_scr[...] = jnp.zeros_like(acc_scr)
    key = lax.broadcasted_iota(I32, (KV_BLOCK, KV_BLOCK), 0)
    qry = lax.broadcasted_iota(I32, (KV_BLOCK, KV_BLOCK), 1)
    past = key < qry
    tri = tri_ref[...]

    zero = jnp.zeros((1, KV_BLOCK), F32)
    carries = [[zero, zero], [zero, zero]]

    def scores(c):
        kb = k_ref[pl.ds(pl.multiple_of(c["j"] * KV_BLOCK, KV_BLOCK), KV_BLOCK), :]
        c["z"] = _dot(kb, q_scr[c["a"], c["h"]])

    def softplus(c):
        z = c.pop("z")
        sp = jnp.maximum(z, 0.0) + jnp.log(1.0 + jnp.exp(-jnp.abs(z)))
        spm = jnp.where(past, sp, 0.0) if c["diagonal"] else sp
        c["log_beta"] = z - sp
        c["spm"] = spm.astype(BF16)
        c["sp0"] = spm[0:1, :]

    def suffix(c):
        c["after"] = _dot(tri, c.pop("spm"))

    def weights(c):
        after = c.pop("after")
        w = jnp.exp(c.pop("log_beta") + after)
        if c["diagonal"]:
            w = jnp.where(past, w, 0.0)
        c["w"] = w.astype(BF16)
        total = after[0:1, :] - c.pop("sp0")
        h, first = c["h"], c["first"]
        if first is None:
            c["carry"] = zero
            carries[c["a"]][h] = total
        else:
            c["carry"] = jnp.where(first, carries[0][h], carries[1][h])
            cout = c["carry"] + total
            carries[0][h] = jnp.where(first, cout, carries[0][h])
            carries[1][h] = jnp.where(first, carries[1][h], cout)

    def values(c):
        rows = slice(c["h"] * HEAD_DIM, (c["h"] + 1) * HEAD_DIM)
        pv = _dot(vt_ref[c["j"], rows, :], c.pop("w")) * jnp.exp(c.pop("carry"))
        acc_scr[c["a"], rows, :] += pv

    _skewed(chains, ((scores, 0), (suffix, 2), (values, 4), (softplus, 1), (weights, 3)))
    for a in range(2):
        o_ref[pl.ds(pl.multiple_of(tiles[a] * KV_BLOCK, KV_BLOCK), KV_BLOCK), :] = acc_scr[a].T.astype(BF16)


def _sb_call(qt, k, vt, tri, batch, seq):
    nb = seq // KV_BLOCK
    tiles_spec, col_spec = _tile_specs(seq)
    return pl.pallas_call(
        _sb_kernel,
        out_shape=jax.ShapeDtypeStruct((batch * seq, WIDTH), BF16),
        grid=(batch, N_PAIRS, nb // 2),
        in_specs=[tiles_spec, col_spec, tiles_spec,
                  pl.BlockSpec((KV_BLOCK, KV_BLOCK), lambda b, p, i: (0, 0))],
        out_specs=col_spec,
        scratch_shapes=[pltpu.VMEM((2, 2, LANES, KV_BLOCK), BF16), pltpu.VMEM((2, LANES, KV_BLOCK), F32)],
        compiler_params=pltpu.CompilerParams(
            dimension_semantics=("arbitrary", "arbitrary", "arbitrary"), vmem_limit_bytes=VMEM_LIMIT),
        name="stickbreak_attn",
    )(qt, k, vt, tri)


def _moba_kernel(qt_ref, k_ref, vt_ref, avg_ref, o_ref, km_ref, bias_scr, q_scr, acc_scr, s_scr, den_scr):
    i = pl.program_id(2)
    nb = qt_ref.shape[0]

    @pl.when(i == 0)
    def _():
        km_ref[...] = _dot(avg_ref[...], k_ref[...])

    tiles, chains = _chain_list(i, nb, lambda first, t: jnp.where(first, t, t - i))
    key = lax.broadcasted_iota(I32, (KV_BLOCK, KV_BLOCK), 0)
    qry = lax.broadcasted_iota(I32, (KV_BLOCK, KV_BLOCK), 1)
    causal = key <= qry
    blk = lax.broadcasted_iota(I32, (nb, KV_BLOCK), 0)
    km = km_ref[...].astype(BF16)
    for a in range(2):
        qt = qt_ref[tiles[a]]
        valid = blk < tiles[a]
        for h in range(2):
            qth = jnp.where(_head_rows(h), qt, jnp.zeros_like(qt))
            q_scr[a, h] = qth
            g = jnp.where(valid, _dot(km, qth), NEG_INF)
            rank = jnp.zeros((nb, KV_BLOCK), F32)
            for jp in range(nb):
                gj = g[jp:jp + 1, :]
                better = jnp.where(gj > g, 1.0, jnp.where(gj == g, jnp.where(blk > jp, 1.0, 0.0), 0.0))
                rank = rank + better
            bias_scr[a, h] = jnp.where(valid, jnp.where(rank < MOBA_TOPK, 0.0, NEG_INF), NEG_INF)

    for n, c in enumerate(chains):
        c["n"] = n
    low = jnp.full((SUBLANES, KV_BLOCK), NEG_INF, F32)
    maxima = [[low, low], [low, low]]
    acc_scr[...] = jnp.zeros_like(acc_scr)
    den_scr[...] = jnp.zeros_like(den_scr)
    ones = jnp.ones((2 * SUBLANES, KV_BLOCK), BF16)

    def scores(c):
        kb = k_ref[pl.ds(pl.multiple_of(c["j"] * KV_BLOCK, KV_BLOCK), KV_BLOCK), :]
        c["s"] = _dot(kb, q_scr[c["a"], c["h"]])

    def mask(c):
        h, first = c["h"], c["first"]
        if first is None:
            s = jnp.where(causal, c.pop("s"), NEG_INF)
        else:
            s = c.pop("s") + bias_scr[c["a"], h, pl.ds(c["j"], 1), :]
        s_scr[c["n"]] = s
        top = jnp.max(s.reshape(KV_BLOCK // SUBLANES, SUBLANES, KV_BLOCK), axis=0)
        if first is None:
            maxima[c["a"]][h] = top
        else:
            maxima[0][h] = jnp.where(first, jnp.maximum(maxima[0][h], top), maxima[0][h])
            maxima[1][h] = jnp.where(first, maxima[1][h], jnp.maximum(maxima[1][h], top))

    _skewed(chains, ((scores, 0), (mask, 1)))
    maxima = [[jnp.max(maxima[a][h], axis=0, keepdims=True) for h in range(2)] for a in range(2)]

    def weights(c):
        h, first = c["h"], c["first"]
        m = maxima[c["a"]][h] if first is None else jnp.where(first, maxima[0][h], maxima[1][h])
        c["p"] = jnp.exp(s_scr[c["n"]] - m).astype(BF16)

    def values(c):
        rows = slice(c["h"] * HEAD_DIM, (c["h"] + 1) * HEAD_DIM)
        pv = _dot(jnp.concatenate([vt_ref[c["j"], rows, :], ones], axis=0), c.pop("p"))
        acc_scr[c["a"], rows, :] += pv[:HEAD_DIM]
        den_scr[c["a"], c["h"]] += pv[HEAD_DIM:]

    _skewed(chains, ((values, 1), (weights, 0)))
    for a in range(2):
        for h in range(2):
            rows = slice(h * HEAD_DIM, (h + 1) * HEAD_DIM)
            acc_scr[a, rows, :] = acc_scr[a, rows, :] / den_scr[a, h, 0:1, :]
        o_ref[pl.ds(pl.multiple_of(tiles[a] * KV_BLOCK, KV_BLOCK), KV_BLOCK), :] = acc_scr[a].T.astype(BF16)


def _moba_call(qt, k, vt, avg, batch, seq):
    nb = seq // KV_BLOCK
    tiles_spec, col_spec = _tile_specs(seq)
    return pl.pallas_call(
        _moba_kernel,
        out_shape=jax.ShapeDtypeStruct((batch * seq, WIDTH), BF16),
        grid=(batch, N_PAIRS, nb // 2),
        in_specs=[tiles_spec, col_spec, tiles_spec, pl.BlockSpec((nb, seq), lambda b, p, i: (0, 0))],
        out_specs=col_spec,
        scratch_shapes=[pltpu.VMEM((nb, LANES), F32), pltpu.VMEM((2, 2, nb, KV_BLOCK), F32),
                        pltpu.VMEM((2, 2, LANES, KV_BLOCK), BF16), pltpu.VMEM((2, LANES, KV_BLOCK), F32),
                        pltpu.VMEM((2 * (nb + 1), KV_BLOCK, KV_BLOCK), F32),
                        pltpu.VMEM((2, 2, 2 * SUBLANES, KV_BLOCK), F32)],
        compiler_params=pltpu.CompilerParams(
            dimension_semantics=("arbitrary", "arbitrary", "arbitrary"), vmem_limit_bytes=VMEM_LIMIT),
        name="moba_attn",
    )(qt, k, vt, avg)


def _layer_norm(r, g, b):
    mu = jnp.mean(r, axis=-1, keepdims=True)
    d = r - mu
    var = jnp.mean(d * d, axis=-1, keepdims=True)
    return d * lax.rsqrt(var + LN_EPS) * g + b


def _mixer_kernel(x_ref, asb_ref, amb_ref, wg_ref, bg_ref, wbs_ref, wbm_ref, wo_ref, lng_ref, lnb_ref,
                  wr_ref, br_ref, h_ref, ls_ref, lpos_ref, tw_ref, cnts_ref, *, d):
    rows_per = MXU_ROWS
    chains = [dict(rows=slice(c * rows_per, (c + 1) * rows_per)) for c in range(TILE // rows_per)]
    expert = lax.broadcasted_iota(I32, (N_EXPERTS, rows_per), 0)
    slot8 = lax.broadcasted_iota(I32, (8, rows_per), 0)
    rt = lax.broadcasted_iota(I32, (rows_per, rows_per), 0)
    ct = lax.broadcasted_iota(I32, (rows_per, rows_per), 1)
    earlier = jnp.where(rt < ct, 1.0, 0.0).astype(BF16)
    seen = [jnp.zeros((N_EXPERTS, LANES), F32)]

    def project(c):
        c["gp"] = _dot(x_ref[c["rows"], :].astype(BF16), wg_ref[...])
        c["ysb"] = _dot(asb_ref[c["rows"], :], wbs_ref[...])
        c["ymb"] = _dot(amb_ref[c["rows"], :], wbm_ref[...])

    def gate(c):
        g = jax.nn.sigmoid(c.pop("gp") + bg_ref[...])
        c["mixed"] = (g[:, :d] * c.pop("ysb") + g[:, d:] * c.pop("ymb")).astype(BF16)

    def out_proj(c):
        c["mix"] = _dot(c.pop("mixed"), wo_ref[...])

    def norm(c):
        h = _layer_norm(DEEPNORM_ALPHA * x_ref[c["rows"], :] + c.pop("mix"), lng_ref[...], lnb_ref[...])
        h_ref[c["rows"], :] = h
        c["hb"] = h.astype(BF16)

    def route(c):
        nt = (((1,), (1,)), ((), ()))
        c["logits"] = lax.dot_general(wr_ref[...], c["hb"], nt, preferred_element_type=F32)

    def top_k(c):
        bias = br_ref[...]
        logits = c.pop("logits") + jnp.concatenate([bias] * (rows_per // LANES), axis=1)
        vals, idxs = [], []
        member = jnp.zeros((N_EXPERTS, rows_per), F32)
        for _ in range(TOP_K):
            mx = jnp.max(logits, axis=0, keepdims=True)
            ik = jnp.min(jnp.where(logits == mx, expert, N_EXPERTS), axis=0, keepdims=True)
            hit = expert == ik
            vals.append(mx)
            idxs.append(ik)
            member = jnp.where(hit, 1.0, member)
            logits = jnp.where(hit, NEG_INF, logits)
        es = [jnp.exp(v - vals[0]) for v in vals]
        den = es[0] + es[1] + es[2] + es[3]
        c["tw"] = [e / den for e in es]
        c["idxs"] = idxs
        c["member"] = member

    def count(c):
        member = c.pop("member")
        c["rank"] = _dot(member.astype(BF16), earlier) + jnp.concatenate([seen[0]] * (rows_per // LANES), axis=1)
        seen[0] = seen[0] + jnp.sum(member, axis=1, keepdims=True)

    _skewed(chains, ((project, 0), (out_proj, 2), (route, 4), (count, 6), (gate, 1), (norm, 3), (top_k, 5)))

    counts = seen[0]
    cnts_ref[...] = counts.astype(I32)
    seg = jnp.ceil(counts * (1.0 / CHUNK)) * CHUNK
    e_r = lax.broadcasted_iota(I32, (N_EXPERTS, N_EXPERTS), 0)
    e_c = lax.broadcasted_iota(I32, (N_EXPERTS, N_EXPERTS), 1)
    loff = _dot(jnp.where(e_c < e_r, 1.0, 0.0).astype(BF16), seg.astype(BF16))
    loff = jnp.concatenate([loff] * (rows_per // LANES), axis=1)
    lpos, tws = [], []
    for c in chains:
        where_to = c.pop("rank") + loff
        lp = jnp.full((8, rows_per), -1.0, F32)
        tw = jnp.zeros((8, rows_per), F32)
        for k, ik in enumerate(c.pop("idxs")):
            pk = jnp.sum(jnp.where(expert == ik, where_to, 0.0), axis=0, keepdims=True)
            lp = jnp.where(slot8 == k, pk, lp)
            tw = jnp.where(slot8 == k, c["tw"][k], tw)
        lpos.append(lp)
        tws.append(tw)
    lpos = jnp.concatenate(lpos, axis=1)
    lpos_ref[...] = lpos.astype(I32)
    tw_ref[...] = jnp.concatenate(tws, axis=1)

    hb = jnp.concatenate([c["hb"] for c in chains], axis=0)
    pieces = [dict(j=j) for j in range(LOCAL_ROWS // rows_per)]
    row = lax.broadcasted_iota(I32, (rows_per, TILE), 0)

    def select(q):
        r = (row + q["j"] * rows_per).astype(F32)
        sel = jnp.zeros((rows_per, TILE), F32)
        for k in range(TOP_K):
            sel = sel + jnp.where(lpos[k:k + 1, :] == r, 1.0, 0.0)
        q["sel"] = sel.astype(BF16)

    def place(q):
        ls_ref[q["j"] * rows_per:(q["j"] + 1) * rows_per, :] = _dot(q.pop("sel"), hb)

    _skewed(pieces, ((place, 1), (select, 0)))


def _mixer_call(x2, a_sb, a_mb, wg, bg, wbs, wbm, wo, lng, lnb, wr, br):
    t_tok, d = x2.shape
    n_tiles = t_tok // TILE
    const = lambda shape: pl.BlockSpec(shape, lambda i: (0,) * len(shape))
    row8 = pl.BlockSpec((8, TILE), lambda i: (0, i))
    return pl.pallas_call(
        functools.partial(_mixer_kernel, d=d),
        out_shape=(jax.ShapeDtypeStruct((t_tok, d), F32),
                   jax.ShapeDtypeStruct((n_tiles * LOCAL_ROWS, d), F32),
                   jax.ShapeDtypeStruct((8, t_tok), I32),
                   jax.ShapeDtypeStruct((8, t_tok), F32),
                   jax.ShapeDtypeStruct((n_tiles, N_EXPERTS, LANES), I32)),
        grid=(n_tiles,),
        in_specs=[pl.BlockSpec((TILE, d), lambda i: (i, 0)),
                  pl.BlockSpec((TILE, WIDTH), lambda i: (i, 0)),
                  pl.BlockSpec((TILE, WIDTH), lambda i: (i, 0)),
                  const((d, 2 * d)), const((1, 2 * d)),
                  const((WIDTH, d)), const((WIDTH, d)), const((d, d)),
                  const((1, d)), const((1, d)),
                  const((N_EXPERTS, d)), const((N_EXPERTS, LANES))],
        out_specs=(pl.BlockSpec((TILE, d), lambda i: (i, 0)),
                   pl.BlockSpec((LOCAL_ROWS, d), lambda i: (i, 0)),
                   row8, row8,
                   pl.BlockSpec((None, N_EXPERTS, LANES), lambda i: (i, 0, 0))),
        compiler_params=pltpu.CompilerParams(dimension_semantics=("arbitrary",),
                                             vmem_limit_bytes=VMEM_LIMIT),
        name="mixer_ln_router",
    )(x2, a_sb, a_mb, wg, bg, wbs, wbm, wo, lng, lnb, wr, br)


CTL_FIRST = 1
CTL_LAST = 2
CTL_ODD = 4


def _plan_kernel(cnt_ref, be_ref, nx_ref, ctl_ref, nu_ref, tail_ref, table_ref,
                 size_ref, follow_ref, first_ref, count_ref, base_ref, *, n_tiles):
    def seg(t, e):
        return (cnt_ref[t * N_EXPERTS + e] + CHUNK - 1) // CHUNK * CHUNK

    def size_of(e, _):
        size_ref[e] = lax.fori_loop(0, n_tiles, lambda t, a: a + seg(t, e), 0)
        return 0

    lax.fori_loop(0, N_EXPERTS, size_of, 0)

    def nonempty_after(t, nxt):
        e = N_EXPERTS - 1 - t
        follow_ref[e] = nxt
        return jnp.where(size_ref[e] > 0, e, nxt)

    lax.fori_loop(0, N_EXPERTS, nonempty_after, -1)

    def clear_block(b, _):
        be_ref[b] = 0
        nx_ref[b] = 0
        ctl_ref[b] = 0
        return 0

    lax.fori_loop(0, be_ref.shape[0], clear_block, 0)

    def clear_tile(t, _):
        tail_ref[t] = 0
        return 0

    lax.fori_loop(0, n_tiles, clear_tile, 0)

    def blocks_of(e, state):
        b0, odd = state
        n = (size_ref[e] + EXPERT_ROWS - 1) // EXPERT_ROWS
        follows = follow_ref[e] >= 0

        def block(j, _):
            be_ref[b0 + j] = e
            nx_ref[b0 + j] = jnp.maximum(follow_ref[e], 0)
            ctl_ref[b0 + j] = (jnp.where(follows & (j == 0), CTL_FIRST, 0)
                               + jnp.where(follows & (j == n - 1), CTL_LAST, 0) + odd * CTL_ODD)
            return 0

        lax.fori_loop(0, n, block, 0)

        def segment(t, pos):
            lo = tail_ref[t]
            chunks = seg(t, e) // CHUNK
            first_ref[t * N_EXPERTS + e] = pos
            count_ref[t * N_EXPERTS + e] = chunks
            base_ref[t * N_EXPERTS + e] = t * LOCAL_ROWS + lo
            tail_ref[t] = lo + chunks * CHUNK
            return pos + chunks

        lax.fori_loop(0, n_tiles, segment, b0 * CHUNKS_PER_BLOCK)
        return b0 + n, jnp.where(n > 0, 1 - odd, odd)

    n_used, _ = lax.fori_loop(0, N_EXPERTS, blocks_of, (0, 0))
    nu_ref[0] = n_used

    slot = (lax.broadcasted_iota(I32, table_ref.shape, 0) * LANES
            + lax.broadcasted_iota(I32, table_ref.shape, 1))

    def fill(s, table):
        first = first_ref[s]
        inside = (slot >= first) & (slot < first + count_ref[s])
        return jnp.where(inside, base_ref[s] + (slot - first) * CHUNK, table)

    table_ref[...] = lax.fori_loop(0, n_tiles * N_EXPERTS, fill, jnp.full(table_ref.shape, -1, I32))


def _plan_call(tile_counts, n_tiles, n_blocks):
    smem = pl.BlockSpec(memory_space=pltpu.SMEM)
    blocks = jax.ShapeDtypeStruct((n_blocks,), I32)
    segments = pltpu.SMEM((n_tiles * N_EXPERTS,), I32)
    return pl.pallas_call(
        functools.partial(_plan_kernel, n_tiles=n_tiles),
        out_shape=(blocks, blocks, blocks, jax.ShapeDtypeStruct((1,), I32),
                   jax.ShapeDtypeStruct((n_tiles,), I32),
                   jax.ShapeDtypeStruct((n_blocks * CHUNKS_PER_BLOCK // LANES, LANES), I32)),
        in_specs=[smem],
        out_specs=(smem,) * 5 + (pl.BlockSpec(memory_space=pltpu.VMEM),),
        scratch_shapes=[pltpu.SMEM((N_EXPERTS,), I32), pltpu.SMEM((N_EXPERTS,), I32), segments, segments, segments],
        name="moe_plan",
    )(tile_counts)


def _sink_pieces():
    return [(off, min(EXPERT_ROWS, LOCAL_ROWS - off)) for off in range(0, LOCAL_ROWS, EXPERT_ROWS)]


def _expert_kernel(be_ref, nx_ref, ctl_ref, nu_ref, table_ref, tail_ref,
                   ls_hbm, wgu_hbm, wd_hbm, bg_ref, bu_ref, bd_ref, lo_hbm,
                   x_buf, o_buf, wgu_stage, wd_stage, wgt0, wut0, wdb0, wgt1, wut1, wdb1, tmp_ref,
                   wsems, xsems, osems, zsem, *, n_tiles):
    i = pl.program_id(0)
    slot = i % 2
    n_used = nu_ref[0]
    live = i < n_used
    ctl = ctl_ref[i]
    sets = ((wgt0, wut0, wdb0), (wgt1, wut1, wdb1))
    chunk = 2 * LANES
    spare = n_tiles * LOCAL_ROWS

    def fetch(e):
        return (pltpu.make_async_copy(wgu_hbm.at[e], wgu_stage, wsems.at[0]),
                pltpu.make_async_copy(wd_hbm.at[e], wd_stage, wsems.at[1]))

    def prepare(wgt_ref, wut_ref, wdb_ref):
        for c in range(wgu_stage.shape[1] // chunk):
            rows = slice(c * LANES, (c + 1) * LANES)
            for k in range(wgu_stage.shape[0] // LANES):
                cols = slice(k * LANES, (k + 1) * LANES)
                tslot = (c * (wgu_stage.shape[0] // LANES) + k) % tmp_ref.shape[0]
                tmp_ref[tslot] = wgu_stage[cols, c * chunk:(c + 1) * chunk].T
                wgt_ref[rows, cols] = tmp_ref[tslot, pl.ds(0, LANES, stride=2), :].astype(BF16)
                wut_ref[rows, cols] = tmp_ref[tslot, pl.ds(1, LANES, stride=2), :].astype(BF16)
        wdb_ref[...] = wd_stage[...].astype(BF16)

    def rows_in(b, s):
        def one(j, _):
            a = table_ref[b * CHUNKS_PER_BLOCK + j]
            src = pl.multiple_of(jnp.where(a >= 0, a, ZERO_ROW), CHUNK)
            dst = x_buf.at[s, pl.ds(pl.multiple_of(j * CHUNK, CHUNK), CHUNK)]
            pltpu.make_async_copy(ls_hbm.at[pl.ds(src, CHUNK)], dst, xsems.at[s]).start()
            return 0
        lax.fori_loop(0, CHUNKS_PER_BLOCK, one, 0)

    def block_done(buf, sems, s):
        pltpu.make_async_copy(buf.at[s], buf.at[s], sems.at[s]).wait()

    def ffn(wgt_ref, wut_ref, wdb_ref):
        nxt = jnp.minimum(i + 1, pl.num_programs(0) - 1)
        for j in range(CHUNKS_PER_BLOCK):
            a = table_ref[nxt * CHUNKS_PER_BLOCK + j]
            src = pl.multiple_of(jnp.where(a >= 0, a, ZERO_ROW), CHUNK)
            dst = x_buf.at[1 - slot, pl.ds(j * CHUNK, CHUNK)]
            pltpu.make_async_copy(ls_hbm.at[pl.ds(src, CHUNK)], dst, xsems.at[1 - slot]).start()

        nt = (((1,), (1,)), ((), ()))
        chains = [dict(rows=slice(c * MXU_ROWS, (c + 1) * MXU_ROWS)) for c in range(EXPERT_ROWS // MXU_ROWS)]

        def gate_up(c):
            x = x_buf[slot, c["rows"], :].astype(BF16)
            c["gate"] = lax.dot_general(x, wgt_ref[...], nt, preferred_element_type=F32)
            c["up"] = lax.dot_general(x, wut_ref[...], nt, preferred_element_type=F32)

        def activation(c):
            gate = jnp.minimum(c.pop("gate") + bg_ref[...], SWIGLU_LIMIT)
            up = jnp.clip(c.pop("up") + bu_ref[...], -SWIGLU_LIMIT, SWIGLU_LIMIT)
            c["act"] = ((up + 1.0) * gate * jax.nn.sigmoid(SWIGLU_ALPHA * gate)).astype(BF16)

        def down(c):
            o_buf[slot, c["rows"], :] = _dot(c.pop("act"), wdb_ref[...]) + bd_ref[...]
            for j in range(c["rows"].start // CHUNK, c["rows"].stop // CHUNK):
                a = table_ref[i * CHUNKS_PER_BLOCK + j]
                pad = spare + slot * EXPERT_ROWS + j * CHUNK
                dst = pl.multiple_of(jnp.where(a >= 0, a, pad), CHUNK)
                src = o_buf.at[slot, pl.ds(j * CHUNK, CHUNK)]
                pltpu.make_async_copy(src, lo_hbm.at[pl.ds(dst, CHUNK)], osems.at[slot]).start()

        _skewed(chains, ((gate_up, 0), (down, 2), (activation, 1)))

    @pl.when(i == 0)
    def _():
        o_buf[0] = jnp.zeros_like(o_buf[0])

        def tails(start):
            def tile(t, _):
                def one(q, _):
                    dst = pl.multiple_of(t * LOCAL_ROWS + tail_ref[t] + q * CHUNK, CHUNK)
                    cp = pltpu.make_async_copy(o_buf.at[0, pl.ds(0, CHUNK)], lo_hbm.at[pl.ds(dst, CHUNK)], zsem)
                    if start:
                        cp.start()
                    else:
                        cp.wait()
                    return 0
                lax.fori_loop(0, (LOCAL_ROWS - tail_ref[t]) // CHUNK, one, 0)
                return 0
            lax.fori_loop(0, n_tiles, tile, 0)

        sink = [pltpu.make_async_copy(o_buf.at[0, pl.ds(0, n)], lo_hbm.at[pl.ds(spare + off, n)], zsem)
                for off, n in _sink_pieces()]
        tails(True)
        for cp in sink:
            cp.start()
        tails(False)
        for cp in sink:
            cp.wait()
        rows_in(0, 0)
        for cp in fetch(be_ref[0]):
            cp.start()
        for cp in fetch(be_ref[0]):
            cp.wait()
        prepare(*sets[0])

    @pl.when(live & ((ctl & CTL_FIRST) != 0))
    def _():
        for cp in fetch(nx_ref[i]):
            cp.start()

    @pl.when(live)
    def _():
        block_done(x_buf, xsems, slot)

    @pl.when(live & (i >= 2))
    def _():
        block_done(o_buf, osems, slot)

    for odd in range(2):
        mine = live & (((ctl & CTL_ODD) != 0) == bool(odd))

        @pl.when(mine & ((ctl & CTL_LAST) == 0))
        def _(odd=odd):
            ffn(*sets[odd])

        @pl.when(mine & ((ctl & CTL_LAST) != 0))
        def _(odd=odd):
            for cp in fetch(nx_ref[i]):
                cp.wait()
            ffn(*sets[odd])
            prepare(*sets[1 - odd])

    @pl.when(i == n_used - 1)
    def _():
        block_done(o_buf, osems, slot)
        block_done(x_buf, xsems, 1 - slot)

        @pl.when(i >= 1)
        def _():
            block_done(o_buf, osems, 1 - slot)


def _expert_call(block_expert, next_expert, ctl, n_used, table, tail, ls, w_gate_up, w_down, bg, bu, bd, n_tiles):
    d = ls.shape[1]
    f = w_down.shape[1]
    n_blocks = block_expert.shape[0]
    bsel = lambda i, be, nx, ctl, nu, tb, tl: (be[i], 0, 0)
    anywhere = pl.BlockSpec(memory_space=pl.ANY)
    prepared = [pltpu.VMEM((f, d), BF16) for _ in range(6)]
    return pl.pallas_call(
        functools.partial(_expert_kernel, n_tiles=n_tiles),
        out_shape=jax.ShapeDtypeStruct(((n_tiles + 1) * LOCAL_ROWS, d), F32),
        grid_spec=pltpu.PrefetchScalarGridSpec(
            num_scalar_prefetch=6,
            grid=(n_blocks,),
            in_specs=[anywhere, anywhere, anywhere,
                      pl.BlockSpec((None, 1, f), bsel),
                      pl.BlockSpec((None, 1, f), bsel),
                      pl.BlockSpec((None, 1, d), bsel)],
            out_specs=anywhere,
            scratch_shapes=[pltpu.VMEM((2, EXPERT_ROWS, d), F32), pltpu.VMEM((2, EXPERT_ROWS, d), F32),
                            pltpu.VMEM((d, 2 * f), F32), pltpu.VMEM((f, d), F32), *prepared,
                            pltpu.VMEM((4, 2 * LANES, LANES), F32),
                            pltpu.SemaphoreType.DMA((2,)), pltpu.SemaphoreType.DMA((2,)),
                            pltpu.SemaphoreType.DMA((2,)), pltpu.SemaphoreType.DMA(())]),
        compiler_params=pltpu.CompilerParams(dimension_semantics=("arbitrary",),
                                             vmem_limit_bytes=VMEM_LIMIT),
        name="expert_ffn",
    )(block_expert, next_expert, ctl, n_used, table, tail, ls, w_gate_up, w_down, bg, bu, bd)


def _combine_kernel(lo_ref, lpos_ref, tw_ref, h_ref, lng_ref, lnb_ref, o_ref):
    rows_per = MXU_ROWS
    fill = jnp.zeros((LANES - 8, TILE), F32)
    lp_col = jnp.concatenate([lpos_ref[...].astype(F32), fill], axis=0).T
    tw_col = jnp.concatenate([tw_ref[...], fill], axis=0).T
    lane = lax.broadcasted_iota(I32, (TILE, rows_per), 1)
    pieces = [dict(j=j) for j in range(LOCAL_ROWS // rows_per)]
    acc = [jnp.zeros(o_ref.shape, F32)]

    def weigh(q):
        r = (lane + q["j"] * rows_per).astype(F32)
        w = jnp.zeros((TILE, rows_per), F32)
        for k in range(TOP_K):
            w = w + jnp.where(lp_col[:, k:k + 1] == r, tw_col[:, k:k + 1], 0.0)
        q["w"] = w.astype(BF16)

    def mix(q):
        rows = lo_ref[q["j"] * rows_per:(q["j"] + 1) * rows_per, :].astype(BF16)
        acc[0] = acc[0] + _dot(q.pop("w"), rows)

    _skewed(pieces, ((mix, 1), (weigh, 0)))
    o_ref[...] = _layer_norm(DEEPNORM_ALPHA * h_ref[...] + acc[0], lng_ref[...], lnb_ref[...])


def _combine_call(lo, lpos_t, tw_t, h, lng, lnb):
    t_tok, d = h.shape
    row8 = pl.BlockSpec((8, TILE), lambda i: (0, i))
    return pl.pallas_call(
        _combine_kernel,
        out_shape=jax.ShapeDtypeStruct((t_tok, d), F32),
        grid=(t_tok // TILE,),
        in_specs=[pl.BlockSpec((LOCAL_ROWS, d), lambda i: (i, 0)), row8, row8,
                  pl.BlockSpec((TILE, d), lambda i: (i, 0)),
                  pl.BlockSpec((1, d), lambda i: (0, 0)),
                  pl.BlockSpec((1, d), lambda i: (0, 0))],
        out_specs=pl.BlockSpec((TILE, d), lambda i: (i, 0)),
        compiler_params=pltpu.CompilerParams(dimension_semantics=("arbitrary",),
                                             vmem_limit_bytes=VMEM_LIMIT),
        name="moe_combine_ln",
    )(lo, lpos_t, tw_t, h, lng, lnb)


def _layer(x2, pos, batch, seq, w_in, b_gate, w_branch_sb, w_branch_moba, w_out, ln_mix_g, ln_mix_b,
           w_router, b_router, w_gate_up, b_gate_up, w_down, b_down, ln_ffn_g, ln_ffn_b):
    t_tok, d = x2.shape
    nb = seq // KV_BLOCK
    qkv_w = 6 * WIDTH
    n_tiles = t_tok // TILE

    w_qkv = w_in[:, :qkv_w].astype(BF16)
    w_g = w_in[:, qkv_w:].astype(BF16)
    pos3 = pos.astype(F32).reshape(n_tiles, 1, TILE)
    invf = (ROPE_THETA ** (-jnp.arange(ROPE_HALF, dtype=F32) / ROPE_HALF)).reshape(ROPE_HALF, 1)
    ki = jnp.arange(KV_BLOCK)
    tri = jnp.where(ki[None, :] > ki[:, None], -1.0, 0.0).astype(BF16)
    avg = jnp.where(jnp.arange(seq)[None, :] // KV_BLOCK == jnp.arange(nb)[:, None],
                    1.0 / KV_BLOCK, 0.0).astype(BF16)

    qt_sb, k_sb, vt_sb, qt_mb, k_mb, vt_mb = _qkv_call(x2, pos3, invf, w_qkv, TILE)
    a_sb = _sb_call(qt_sb, k_sb, vt_sb, tri, batch, seq)
    a_mb = _moba_call(qt_mb, k_mb, vt_mb, avg, batch, seq)

    w_r = w_router.T.astype(BF16)
    b_r = jnp.broadcast_to(b_router[:, None], (N_EXPERTS, LANES))
    h, ls, lpos_t, tw_t, cnts = _mixer_call(
        x2, a_sb, a_mb, w_g, b_gate.reshape(1, -1), w_branch_sb.astype(BF16), w_branch_moba.astype(BF16),
        w_out.astype(BF16), ln_mix_g.reshape(1, -1), ln_mix_b.reshape(1, -1), w_r, b_r)

    max_rows = t_tok * TOP_K + n_tiles * N_EXPERTS * (CHUNK - 1)
    n_blocks = -(-max_rows // EXPERT_ROWS) + N_EXPERTS
    n_blocks = -(-n_blocks // 16) * 16
    block_expert, next_expert, ctl, n_used, tail, table = _plan_call(cnts[:, :, 0].reshape(-1), n_tiles, n_blocks)
    table = table.reshape(-1)

    f = w_down.shape[1]
    lo = _expert_call(block_expert, next_expert, ctl, n_used, table, tail, ls, w_gate_up, w_down,
                      b_gate_up[:, 0::2].reshape(N_EXPERTS, 1, f), b_gate_up[:, 1::2].reshape(N_EXPERTS, 1, f),
                      b_down.reshape(N_EXPERTS, 1, d), n_tiles)
    return _combine_call(lo, lpos_t, tw_t, h, ln_ffn_g.reshape(1, -1), ln_ffn_b.reshape(1, -1))


def kernel(x, positions, w_in, b_gate, w_branch_sb, w_branch_moba, w_out, ln_mix_g, ln_mix_b, w_router,
           b_router, w_gate_up, b_gate_up, w_down, b_down, ln_ffn_g, ln_ffn_b):
    batch, seq, d = x.shape
    h = x.reshape(batch * seq, d)
    pos = positions.reshape(batch * seq)
    for layer in range(w_in.shape[0]):
        h = _layer(h, pos, batch, seq, w_in[layer], b_gate[layer], w_branch_sb[layer], w_branch_moba[layer],
                   w_out[layer], ln_mix_g[layer], ln_mix_b[layer], w_router[layer], b_router[layer],
                   w_gate_up[layer], b_gate_up[layer], w_down[layer], b_down[layer],
                   ln_ffn_g[layer], ln_ffn_b[layer])
    return h.reshape(batch, seq, d)
```

```python
import functools

import jax
import jax.numpy as jnp
from jax import lax
from jax.experimental import pallas as pl
from jax.experimental.pallas import tpu as pltpu

F32 = jnp.float32
BF16 = jnp.bfloat16
I32 = jnp.int32

HEAD_DIM = 64
N_HEADS = 8
WIDTH = N_HEADS * HEAD_DIM
LANES = 128
SUBLANES = 8
N_PAIRS = WIDTH // LANES
KV_BLOCK = 256
MOBA_TOPK = 3
ROPE_THETA = 500000.0
ROPE_HALF = 8
N_EXPERTS = 32
TOP_K = 4
SWIGLU_LIMIT = 7.0
SWIGLU_ALPHA = 1.702
EXPERT_ROWS = 512
MXU_ROWS = 256
TILE = 512
CHUNK = 8
LOCAL_ROWS = TILE * TOP_K + N_EXPERTS * CHUNK
CHUNKS_PER_BLOCK = EXPERT_ROWS // CHUNK
ZERO_ROW = LOCAL_ROWS - CHUNK
LN_EPS = 1e-5
DEPTH = 1
DEEPNORM_ALPHA = (2 * DEPTH) ** 0.25
QK_SCALE = HEAD_DIM ** -0.5
NEG_INF = float("-inf")
VMEM_LIMIT = 56 * 1024 * 1024


def _dot(a, b):
    return jnp.dot(a, b, preferred_element_type=F32)


def _qkv_kernel(x_ref, pos_ref, invf_ref, w_ref,
                qt_sb_ref, k_sb_ref, vt_sb_ref, qt_mb_ref, k_mb_ref, vt_mb_ref, *, tm):
    xb = x_ref[...].astype(BF16)
    ang = invf_ref[...] * pos_ref[0]
    cos, sin = jnp.cos(ang), jnp.sin(ang)

    def rope_t(t):
        parts = []
        for base in (0, HEAD_DIM):
            x1 = t[base:base + ROPE_HALF]
            x2 = t[base + ROPE_HALF:base + 2 * ROPE_HALF]
            parts += [x1 * cos - x2 * sin, x2 * cos + x1 * sin, t[base + 2 * ROPE_HALF:base + HEAD_DIM]]
        return jnp.concatenate(parts, axis=0)

    def store_t(ref, p, t):
        tb = t.astype(BF16)
        for blk in range(tm // KV_BLOCK):
            ref[blk, p] = tb[:, blk * KV_BLOCK:(blk + 1) * KV_BLOCK]

    for sec in range(6):
        for half in range(2):
            c0 = sec * WIDTH + half * 2 * LANES
            r = _dot(xb, w_ref[:, c0:c0 + 2 * LANES])
            for q in range(2):
                p = half * 2 + q
                t = r[:, q * LANES:(q + 1) * LANES]
                if sec == 0:
                    store_t(qt_sb_ref, p, (t * QK_SCALE).T)
                elif sec == 1:
                    k_sb_ref[:, p * LANES:(p + 1) * LANES] = t.astype(BF16)
                elif sec == 2:
                    store_t(vt_sb_ref, p, t.T)
                elif sec == 3:
                    store_t(qt_mb_ref, p, rope_t(t.T) * QK_SCALE)
                elif sec == 4:
                    k_mb_ref[:, p * LANES:(p + 1) * LANES] = rope_t(t.T).T.astype(BF16)
                else:
                    store_t(vt_mb_ref, p, t.T)


def _qkv_call(x2, pos3, invf, w_qkv, tm):
    t_tok, d = x2.shape
    nblk = t_tok // KV_BLOCK
    bpt = tm // KV_BLOCK
    t_shape = jax.ShapeDtypeStruct((nblk, N_PAIRS, LANES, KV_BLOCK), BF16)
    n_shape = jax.ShapeDtypeStruct((t_tok, WIDTH), BF16)
    t_spec = pl.BlockSpec((bpt, N_PAIRS, LANES, KV_BLOCK), lambda i: (i, 0, 0, 0))
    n_spec = pl.BlockSpec((tm, WIDTH), lambda i: (i, 0))
    return pl.pallas_call(
        functools.partial(_qkv_kernel, tm=tm),
        out_shape=(t_shape, n_shape, t_shape, t_shape, n_shape, t_shape),
        grid=(t_tok // tm,),
        in_specs=[pl.BlockSpec((tm, d), lambda i: (i, 0)),
                  pl.BlockSpec((1, 1, tm), lambda i: (i, 0, 0)),
                  pl.BlockSpec((ROPE_HALF, 1), lambda i: (0, 0)),
                  pl.BlockSpec((d, 6 * WIDTH), lambda i: (0, 0))],
        out_specs=(t_spec, n_spec, t_spec, t_spec, n_spec, t_spec),
        compiler_params=pltpu.CompilerParams(dimension_semantics=("arbitrary",),
                                             vmem_limit_bytes=VMEM_LIMIT),
        name="qkv_proj",
    )(x2, pos3, invf, w_qkv)


def _head_rows(h):
    row = lax.broadcasted_iota(I32, (LANES, KV_BLOCK), 0)
    return (row >= HEAD_DIM * h) & (row < HEAD_DIM * (h + 1))


def _tile_specs(seq):
    nb = seq // KV_BLOCK
    tiles_spec = pl.BlockSpec((nb, None, LANES, KV_BLOCK), lambda b, p, i: (b, p, 0, 0))
    col_spec = pl.BlockSpec((seq, LANES), lambda b, p, i: (b, p))
    return tiles_spec, col_spec


def _chain_list(i, nb, past_block):
    tiles = (i, nb - 1 - i)
    chains = [dict(j=tiles[a], a=a, h=h, diagonal=True, first=None) for a in range(2) for h in range(2)]
    for t in range(nb - 1):
        first = t < i
        for h in range(2):
            chains.append(dict(j=past_block(first, t), a=jnp.where(first, 0, 1), h=h, diagonal=False, first=first))
    return tiles, chains


def _skewed(chains, stages):
    n = len(chains)
    for slot in range(n + max(lag for _, lag in stages)):
        for stage, lag in stages:
            if 0 <= slot - lag < n:
                stage(chains[slot - lag])


def _sb_kernel(qt_ref, k_ref, vt_ref, tri_ref, o_ref, q_scr, acc_scr):
    i = pl.program_id(2)
    nb = qt_ref.shape[0]
    tiles, chains = _chain_list(i, nb, lambda first, t: jnp.where(first, i - 1 - t, nb - 2 - t))
    for a in range(2):
        qt = qt_ref[tiles[a]]
        for h in range(2):
            q_scr[a, h] = jnp.where(_head_rows(h), qt, jnp.zeros_like(qt))
    acc_scr[...] = jnp.zeros_like(acc_scr)
    key = lax.broadcasted_iota(I32, (KV_BLOCK, KV_BLOCK), 0)
    qry = lax.broadcasted_iota(I32, (KV_BLOCK, KV_BLOCK), 1)
    past = key < qry
    tri = tri_ref[...]

    zero = jnp.zeros((1, KV_BLOCK), F32)
    carries = [[zero, zero], [zero, zero]]

    def scores(c):
        kb = k_ref[pl.ds(pl.multiple_of(c["j"] * KV_BLOCK, KV_BLOCK), KV_BLOCK), :]
        c["z"] = _dot(kb, q_scr[c["a"], c["h"]])

    def softplus(c):
        z = c.pop("z")
        sp = jnp.maximum(z, 0.0) + jnp.log(1.0 + jnp.exp(-jnp.abs(z)))
        spm = jnp.where(past, sp, 0.0) if c["diagonal"] else sp
        c["log_beta"] = z - sp
        c["spm"] = spm.astype(BF16)
        c["sp0"] = spm[0:1, :]

    def suffix(c):
        c["after"] = _dot(tri, c.pop("spm"))

    def weights(c):
        after = c.pop("after")
        w = jnp.exp(c.pop("log_beta") + after)
        if c["diagonal"]:
            w = jnp.where(past, w, 0.0)
        c["w"] = w.astype(BF16)
        total = after[0:1, :] - c.pop("sp0")
        h, first = c["h"], c["first"]
        if first is None:
            c["carry"] = zero
            carries[c["a"]][h] = total
        else:
            c["carry"] = jnp.where(first, carries[0][h], carries[1][h])
            cout = c["carry"] + total
            carries[0][h] = jnp.where(first, cout, carries[0][h])
            carries[1][h] = jnp.where(first, carries[1][h], cout)

    def values(c):
        rows = slice(c["h"] * HEAD_DIM, (c["h"] + 1) * HEAD_DIM)
        pv = _dot(vt_ref[c["j"], rows, :], c.pop("w")) * jnp.exp(c.pop("carry"))
        acc_scr[c["a"], rows, :] += pv

    _skewed(chains, ((scores, 0), (suffix, 2), (values, 4), (softplus, 1), (weights, 3)))
    for a in range(2):
        o_ref[pl.ds(pl.multiple_of(tiles[a] * KV_BLOCK, KV_BLOCK), KV_BLOCK), :] = acc_scr[a].T.astype(BF16)


def _sb_call(qt, k, vt, tri, batch, seq):
    nb = seq // KV_BLOCK
    tiles_spec, col_spec = _tile_specs(seq)
    return pl.pallas_call(
        _sb_kernel,
        out_shape=jax.ShapeDtypeStruct((batch * seq, WIDTH), BF16),
        grid=(batch, N_PAIRS, nb // 2),
        in_specs=[tiles_spec, col_spec, tiles_spec,
                  pl.BlockSpec((KV_BLOCK, KV_BLOCK), lambda b, p, i: (0, 0))],
        out_specs=col_spec,
        scratch_shapes=[pltpu.VMEM((2, 2, LANES, KV_BLOCK), BF16), pltpu.VMEM((2, LANES, KV_BLOCK), F32)],
        compiler_params=pltpu.CompilerParams(
            dimension_semantics=("arbitrary", "arbitrary", "arbitrary"), vmem_limit_bytes=VMEM_LIMIT),
        name="stickbreak_attn",
    )(qt, k, vt, tri)


def _moba_kernel(qt_ref, k_ref, vt_ref, avg_ref, o_ref, km_ref, bias_scr, q_scr, acc_scr, s_scr, den_scr):
    i = pl.program_id(2)
    nb = qt_ref.shape[0]

    @pl.when(i == 0)
    def _():
        km_ref[...] = _dot(avg_ref[...], k_ref[...])

    tiles, chains = _chain_list(i, nb, lambda first, t: jnp.where(first, t, t - i))
    key = lax.broadcasted_iota(I32, (KV_BLOCK, KV_BLOCK), 0)
    qry = lax.broadcasted_iota(I32, (KV_BLOCK, KV_BLOCK), 1)
    causal = key <= qry
    blk = lax.broadcasted_iota(I32, (nb, KV_BLOCK), 0)
    km = km_ref[...].astype(BF16)
    for a in range(2):
        qt = qt_ref[tiles[a]]
        valid = blk < tiles[a]
        for h in range(2):
            qth = jnp.where(_head_rows(h), qt, jnp.zeros_like(qt))
            q_scr[a, h] = qth
            g = jnp.where(valid, _dot(km, qth), NEG_INF)
            rank = jnp.zeros((nb, KV_BLOCK), F32)
            for jp in range(nb):
                gj = g[jp:jp + 1, :]
                better = jnp.where(gj > g, 1.0, jnp.where(gj == g, jnp.where(blk > jp, 1.0, 0.0), 0.0))
                rank = rank + better
            bias_scr[a, h] = jnp.where(valid, jnp.where(rank < MOBA_TOPK, 0.0, NEG_INF), NEG_INF)

    for n, c in enumerate(chains):
        c["n"] = n
    low = jnp.full((SUBLANES, KV_BLOCK), NEG_INF, F32)
    maxima = [[low, low], [low, low]]
    acc_scr[...] = jnp.zeros_like(acc_scr)
    den_scr[...] = jnp.zeros_like(den_scr)
    ones = jnp.ones((2 * SUBLANES, KV_BLOCK), BF16)

    def scores(c):
        kb = k_ref[pl.ds(pl.multiple_of(c["j"] * KV_BLOCK, KV_BLOCK), KV_BLOCK), :]
        c["s"] = _dot(kb, q_scr[c["a"], c["h"]])

    def mask(c):
        h, first = c["h"], c["first"]
        if first is None:
            s = jnp.where(causal, c.pop("s"), NEG_INF)
        else:
            s = c.pop("s") + bias_scr[c["a"], h, pl.ds(c["j"], 1), :]
        s_scr[c["n"]] = s
        top = jnp.max(s.reshape(KV_BLOCK // SUBLANES, SUBLANES, KV_BLOCK), axis=0)
        if first is None:
            maxima[c["a"]][h] = top
        else:
            maxima[0][h] = jnp.where(first, jnp.maximum(maxima[0][h], top), maxima[0][h])
            maxima[1][h] = jnp.where(first, maxima[1][h], jnp.maximum(maxima[1][h], top))

    _skewed(chains, ((scores, 0), (mask, 1)))
    maxima = [[jnp.max(maxima[a][h], axis=0, keepdims=True) for h in range(2)] for a in range(2)]

    def weights(c):
        h, first = c["h"], c["first"]
        m = maxima[c["a"]][h] if first is None else jnp.where(first, maxima[0][h], maxima[1][h])
        c["p"] = jnp.exp(s_scr[c["n"]] - m).astype(BF16)

    def values(c):
        rows = slice(c["h"] * HEAD_DIM, (c["h"] + 1) * HEAD_DIM)
        pv = _dot(jnp.concatenate([vt_ref[c["j"], rows, :], ones], axis=0), c.pop("p"))
        acc_scr[c["a"], rows, :] += pv[:HEAD_DIM]
        den_scr[c["a"], c["h"]] += pv[HEAD_DIM:]

    _skewed(chains, ((values, 1), (weights, 0)))
    for a in range(2):
        for h in range(2):
            rows = slice(h * HEAD_DIM, (h + 1) * HEAD_DIM)
            acc_scr[a, rows, :] = acc_scr[a, rows, :] / den_scr[a, h, 0:1, :]
        o_ref[pl.ds(pl.multiple_of(tiles[a] * KV_BLOCK, KV_BLOCK), KV_BLOCK), :] = acc_scr[a].T.astype(BF16)


def _moba_call(qt, k, vt, avg, batch, seq):
    nb = seq // KV_BLOCK
    tiles_spec, col_spec = _tile_specs(seq)
    return pl.pallas_call(
        _moba_kernel,
        out_shape=jax.ShapeDtypeStruct((batch * seq, WIDTH), BF16),
        grid=(batch, N_PAIRS, nb // 2),
        in_specs=[tiles_spec, col_spec, tiles_spec, pl.BlockSpec((nb, seq), lambda b, p, i: (0, 0))],
        out_specs=col_spec,
        scratch_shapes=[pltpu.VMEM((nb, LANES), F32), pltpu.VMEM((2, 2, nb, KV_BLOCK), F32),
                        pltpu.VMEM((2, 2, LANES, KV_BLOCK), BF16), pltpu.VMEM((2, LANES, KV_BLOCK), F32),
                        pltpu.VMEM((2 * (nb + 1), KV_BLOCK, KV_BLOCK), F32),
                        pltpu.VMEM((2, 2, 2 * SUBLANES, KV_BLOCK), F32)],
        compiler_params=pltpu.CompilerParams(
            dimension_semantics=("arbitrary", "arbitrary", "arbitrary"), vmem_limit_bytes=VMEM_LIMIT),
        name="moba_attn",
    )(qt, k, vt, avg)


def _layer_norm(r, g, b):
    mu = jnp.mean(r, axis=-1, keepdims=True)
    d = r - mu
    var = jnp.mean(d * d, axis=-1, keepdims=True)
    return d * lax.rsqrt(var + LN_EPS) * g + b


def _mixer_kernel(x_ref, asb_ref, amb_ref, wg_ref, bg_ref, wbs_ref, wbm_ref, wo_ref, lng_ref, lnb_ref,
                  wr_ref, br_ref, h_ref, ls_ref, lpos_ref, tw_ref, cnts_ref, *, d):
    rows_per = MXU_ROWS
    chains = [dict(rows=slice(c * rows_per, (c + 1) * rows_per)) for c in range(TILE // rows_per)]
    expert = lax.broadcasted_iota(I32, (N_EXPERTS, rows_per), 0)
    slot8 = lax.broadcasted_iota(I32, (8, rows_per), 0)
    rt = lax.broadcasted_iota(I32, (rows_per, rows_per), 0)
    ct = lax.broadcasted_iota(I32, (rows_per, rows_per), 1)
    earlier = jnp.where(rt < ct, 1.0, 0.0).astype(BF16)
    seen = [jnp.zeros((N_EXPERTS, LANES), F32)]

    def project(c):
        c["gp"] = _dot(x_ref[c["rows"], :].astype(BF16), wg_ref[...])
        c["ysb"] = _dot(asb_ref[c["rows"], :], wbs_ref[...])
        c["ymb"] = _dot(amb_ref[c["rows"], :], wbm_ref[...])

    def gate(c):
        g = jax.nn.sigmoid(c.pop("gp") + bg_ref[...])
        c["mixed"] = (g[:, :d] * c.pop("ysb") + g[:, d:] * c.pop("ymb")).astype(BF16)

    def out_proj(c):
        c["mix"] = _dot(c.pop("mixed"), wo_ref[...])

    def norm(c):
        h = _layer_norm(DEEPNORM_ALPHA * x_ref[c["rows"], :] + c.pop("mix"), lng_ref[...], lnb_ref[...])
        h_ref[c["rows"], :] = h
        c["hb"] = h.astype(BF16)

    def route(c):
        nt = (((1,), (1,)), ((), ()))
        c["logits"] = lax.dot_general(wr_ref[...], c["hb"], nt, preferred_element_type=F32)

    def top_k(c):
        bias = br_ref[...]
        logits = c.pop("logits") + jnp.concatenate([bias] * (rows_per // LANES), axis=1)
        vals, idxs = [], []
        member = jnp.zeros((N_EXPERTS, rows_per), F32)
        for _ in range(TOP_K):
            mx = jnp.max(logits, axis=0, keepdims=True)
            ik = jnp.min(jnp.where(logits == mx, expert, N_EXPERTS), axis=0, keepdims=True)
            hit = expert == ik
            vals.append(mx)
            idxs.append(ik)
            member = jnp.where(hit, 1.0, member)
            logits = jnp.where(hit, NEG_INF, logits)
        es = [jnp.exp(v - vals[0]) for v in vals]
        den = es[0] + es[1] + es[2] + es[3]
        c["tw"] = [e / den for e in es]
        c["idxs"] = idxs
        c["member"] = member

    def count(c):
        member = c.pop("member")
        c["rank"] = _dot(member.astype(BF16), earlier) + jnp.concatenate([seen[0]] * (rows_per // LANES), axis=1)
        seen[0] = seen[0] + jnp.sum(member, axis=1, keepdims=True)

    _skewed(chains, ((project, 0), (out_proj, 2), (route, 4), (count, 6), (gate, 1), (norm, 3), (top_k, 5)))

    counts = seen[0]
    cnts_ref[...] = counts.astype(I32)
    seg = jnp.ceil(counts * (1.0 / CHUNK)) * CHUNK
    e_r = lax.broadcasted_iota(I32, (N_EXPERTS, N_EXPERTS), 0)
    e_c = lax.broadcasted_iota(I32, (N_EXPERTS, N_EXPERTS), 1)
    loff = _dot(jnp.where(e_c < e_r, 1.0, 0.0).astype(BF16), seg.astype(BF16))
    loff = jnp.concatenate([loff] * (rows_per // LANES), axis=1)
    lpos, tws = [], []
    for c in chains:
        where_to = c.pop("rank") + loff
        lp = jnp.full((8, rows_per), -1.0, F32)
        tw = jnp.zeros((8, rows_per), F32)
        for k, ik in enumerate(c.pop("idxs")):
            pk = jnp.sum(jnp.where(expert == ik, where_to, 0.0), axis=0, keepdims=True)
            lp = jnp.where(slot8 == k, pk, lp)
            tw = jnp.where(slot8 == k, c["tw"][k], tw)
        lpos.append(lp)
        tws.append(tw)
    lpos = jnp.concatenate(lpos, axis=1)
    lpos_ref[...] = lpos.astype(I32)
    tw_ref[...] = jnp.concatenate(tws, axis=1)

    hb = jnp.concatenate([c["hb"] for c in chains], axis=0)
    pieces = [dict(j=j) for j in range(LOCAL_ROWS // rows_per)]
    row = lax.broadcasted_iota(I32, (rows_per, TILE), 0)

    def select(q):
        r = (row + q["j"] * rows_per).astype(F32)
        sel = jnp.zeros((rows_per, TILE), F32)
        for k in range(TOP_K):
            sel = sel + jnp.where(lpos[k:k + 1, :] == r, 1.0, 0.0)
        q["sel"] = sel.astype(BF16)

    def place(q):
        ls_ref[q["j"] * rows_per:(q["j"] + 1) * rows_per, :] = _dot(q.pop("sel"), hb)

    _skewed(pieces, ((place, 1), (select, 0)))


def _mixer_call(x2, a_sb, a_mb, wg, bg, wbs, wbm, wo, lng, lnb, wr, br):
    t_tok, d = x2.shape
    n_tiles = t_tok // TILE
    const = lambda shape: pl.BlockSpec(shape, lambda i: (0,) * len(shape))
    row8 = pl.BlockSpec((8, TILE), lambda i: (0, i))
    return pl.pallas_call(
        functools.partial(_mixer_kernel, d=d),
        out_shape=(jax.ShapeDtypeStruct((t_tok, d), F32),
                   jax.ShapeDtypeStruct((n_tiles * LOCAL_ROWS, d), F32),
                   jax.ShapeDtypeStruct((8, t_tok), I32),
                   jax.ShapeDtypeStruct((8, t_tok), F32),
                   jax.ShapeDtypeStruct((n_tiles, N_EXPERTS, LANES), I32)),
        grid=(n_tiles,),
        in_specs=[pl.BlockSpec((TILE, d), lambda i: (i, 0)),
                  pl.BlockSpec((TILE, WIDTH), lambda i: (i, 0)),
                  pl.BlockSpec((TILE, WIDTH), lambda i: (i, 0)),
                  const((d, 2 * d)), const((1, 2 * d)),
                  const((WIDTH, d)), const((WIDTH, d)), const((d, d)),
                  const((1, d)), const((1, d)),
                  const((N_EXPERTS, d)), const((N_EXPERTS, LANES))],
        out_specs=(pl.BlockSpec((TILE, d), lambda i: (i, 0)),
                   pl.BlockSpec((LOCAL_ROWS, d), lambda i: (i, 0)),
                   row8, row8,
                   pl.BlockSpec((None, N_EXPERTS, LANES), lambda i: (i, 0, 0))),
        compiler_params=pltpu.CompilerParams(dimension_semantics=("arbitrary",),
                                             vmem_limit_bytes=VMEM_LIMIT),
        name="mixer_ln_router",
    )(x2, a_sb, a_mb, wg, bg, wbs, wbm, wo, lng, lnb, wr, br)


CTL_FIRST = 1
CTL_LAST = 2
CTL_ODD = 4


def _plan_kernel(cnt_ref, be_ref, nx_ref, ctl_ref, nu_ref, tail_ref, table_ref,
                 size_ref, follow_ref, first_ref, count_ref, base_ref, *, n_tiles):
    def seg(t, e):
        return (cnt_ref[t * N_EXPERTS + e] + CHUNK - 1) // CHUNK * CHUNK

    def size_of(e, _):
        size_ref[e] = lax.fori_loop(0, n_tiles, lambda t, a: a + seg(t, e), 0)
        return 0

    lax.fori_loop(0, N_EXPERTS, size_of, 0)

    def nonempty_after(t, nxt):
        e = N_EXPERTS - 1 - t
        follow_ref[e] = nxt
        return jnp.where(size_ref[e] > 0, e, nxt)

    lax.fori_loop(0, N_EXPERTS, nonempty_after, -1)

    def clear_block(b, _):
        be_ref[b] = 0
        nx_ref[b] = 0
        ctl_ref[b] = 0
        return 0

    lax.fori_loop(0, be_ref.shape[0], clear_block, 0)

    def clear_tile(t, _):
        tail_ref[t] = 0
        return 0

    lax.fori_loop(0, n_tiles, clear_tile, 0)

    def blocks_of(e, state):
        b0, odd = state
        n = (size_ref[e] + EXPERT_ROWS - 1) // EXPERT_ROWS
        follows = follow_ref[e] >= 0

        def block(j, _):
            be_ref[b0 + j] = e
            nx_ref[b0 + j] = jnp.maximum(follow_ref[e], 0)
            ctl_ref[b0 + j] = (jnp.where(follows & (j == 0), CTL_FIRST, 0)
                               + jnp.where(follows & (j == n - 1), CTL_LAST, 0) + odd * CTL_ODD)
            return 0

        lax.fori_loop(0, n, block, 0)

        def segment(t, pos):
            lo = tail_ref[t]
            chunks = seg(t, e) // CHUNK
            first_ref[t * N_EXPERTS + e] = pos
            count_ref[t * N_EXPERTS + e] = chunks
            base_ref[t * N_EXPERTS + e] = t * LOCAL_ROWS + lo
            tail_ref[t] = lo + chunks * CHUNK
            return pos + chunks

        lax.fori_loop(0, n_tiles, segment, b0 * CHUNKS_PER_BLOCK)
        return b0 + n, jnp.where(n > 0, 1 - odd, odd)

    n_used, _ = lax.fori_loop(0, N_EXPERTS, blocks_of, (0, 0))
    nu_ref[0] = n_used

    slot = (lax.broadcasted_iota(I32, table_ref.shape, 0) * LANES
            + lax.broadcasted_iota(I32, table_ref.shape, 1))

    def fill(s, table):
        first = first_ref[s]
        inside = (slot >= first) & (slot < first + count_ref[s])
        return jnp.where(inside, base_ref[s] + (slot - first) * CHUNK, table)

    table_ref[...] = lax.fori_loop(0, n_tiles * N_EXPERTS, fill, jnp.full(table_ref.shape, -1, I32))


def _plan_call(tile_counts, n_tiles, n_blocks):
    smem = pl.BlockSpec(memory_space=pltpu.SMEM)
    blocks = jax.ShapeDtypeStruct((n_blocks,), I32)
    segments = pltpu.SMEM((n_tiles * N_EXPERTS,), I32)
    return pl.pallas_call(
        functools.partial(_plan_kernel, n_tiles=n_tiles),
        out_shape=(blocks, blocks, blocks, jax.ShapeDtypeStruct((1,), I32),
                   jax.ShapeDtypeStruct((n_tiles,), I32),
                   jax.ShapeDtypeStruct((n_blocks * CHUNKS_PER_BLOCK // LANES, LANES), I32)),
        in_specs=[smem],
        out_specs=(smem,) * 5 + (pl.BlockSpec(memory_space=pltpu.VMEM),),
        scratch_shapes=[pltpu.SMEM((N_EXPERTS,), I32), pltpu.SMEM((N_EXPERTS,), I32), segments, segments, segments],
        name="moe_plan",
    )(tile_counts)


def _sink_pieces():
    return [(off, min(EXPERT_ROWS, LOCAL_ROWS - off)) for off in range(0, LOCAL_ROWS, EXPERT_ROWS)]


def _expert_kernel(be_ref, nx_ref, ctl_ref, nu_ref, table_ref, tail_ref,
                   ls_hbm, wgu_hbm, wd_hbm, bg_ref, bu_ref, bd_ref, lo_hbm,
                   x_buf, o_buf, wgu_stage, wd_stage, wgt0, wut0, wdb0, wgt1, wut1, wdb1, tmp_ref,
                   wsems, xsems, osems, zsem, *, n_tiles):
    i = pl.program_id(0)
    slot = i % 2
    n_used = nu_ref[0]
    live = i < n_used
    ctl = ctl_ref[i]
    sets = ((wgt0, wut0, wdb0), (wgt1, wut1, wdb1))
    chunk = 2 * LANES
    spare = n_tiles * LOCAL_ROWS

    def fetch(e):
        return (pltpu.make_async_copy(wgu_hbm.at[e], wgu_stage, wsems.at[0]),
                pltpu.make_async_copy(wd_hbm.at[e], wd_stage, wsems.at[1]))

    def prepare(wgt_ref, wut_ref, wdb_ref):
        for c in range(wgu_stage.shape[1] // chunk):
            rows = slice(c * LANES, (c + 1) * LANES)
            for k in range(wgu_stage.shape[0] // LANES):
                cols = slice(k * LANES, (k + 1) * LANES)
                tslot = (c * (wgu_stage.shape[0] // LANES) + k) % tmp_ref.shape[0]
                tmp_ref[tslot] = wgu_stage[cols, c * chunk:(c + 1) * chunk].T
                wgt_ref[rows, cols] = tmp_ref[tslot, pl.ds(0, LANES, stride=2), :].astype(BF16)
                wut_ref[rows, cols] = tmp_ref[tslot, pl.ds(1, LANES, stride=2), :].astype(BF16)
        wdb_ref[...] = wd_stage[...].astype(BF16)

    def rows_in(b, s):
        def one(j, _):
            a = table_ref[b * CHUNKS_PER_BLOCK + j]
            src = pl.multiple_of(jnp.where(a >= 0, a, ZERO_ROW), CHUNK)
            dst = x_buf.at[s, pl.ds(pl.multiple_of(j * CHUNK, CHUNK), CHUNK)]
            pltpu.make_async_copy(ls_hbm.at[pl.ds(src, CHUNK)], dst, xsems.at[s]).start()
            return 0
        lax.fori_loop(0, CHUNKS_PER_BLOCK, one, 0)

    def block_done(buf, sems, s):
        pltpu.make_async_copy(buf.at[s], buf.at[s], sems.at[s]).wait()

    def ffn(wgt_ref, wut_ref, wdb_ref):
        nxt = jnp.minimum(i + 1, pl.num_programs(0) - 1)
        for j in range(CHUNKS_PER_BLOCK):
            a = table_ref[nxt * CHUNKS_PER_BLOCK + j]
            src = pl.multiple_of(jnp.where(a >= 0, a, ZERO_ROW), CHUNK)
            dst = x_buf.at[1 - slot, pl.ds(j * CHUNK, CHUNK)]
            pltpu.make_async_copy(ls_hbm.at[pl.ds(src, CHUNK)], dst, xsems.at[1 - slot]).start()

        nt = (((1,), (1,)), ((), ()))
        chains = [dict(rows=slice(c * MXU_ROWS, (c + 1) * MXU_ROWS)) for c in range(EXPERT_ROWS // MXU_ROWS)]

        def gate_up(c):
            x = x_buf[slot, c["rows"], :].astype(BF16)
            c["gate"] = lax.dot_general(x, wgt_ref[...], nt, preferred_element_type=F32)
            c["up"] = lax.dot_general(x, wut_ref[...], nt, preferred_element_type=F32)

        def activation(c):
            gate = jnp.minimum(c.pop("gate") + bg_ref[...], SWIGLU_LIMIT)
            up = jnp.clip(c.pop("up") + bu_ref[...], -SWIGLU_LIMIT, SWIGLU_LIMIT)
            c["act"] = ((up + 1.0) * gate * jax.nn.sigmoid(SWIGLU_ALPHA * gate)).astype(BF16)

        def down(c):
            o_buf[slot, c["rows"], :] = _dot(c.pop("act"), wdb_ref[...]) + bd_ref[...]
            for j in range(c["rows"].start // CHUNK, c["rows"].stop // CHUNK):
                a = table_ref[i * CHUNKS_PER_BLOCK + j]
                pad = spare + slot * EXPERT_ROWS + j * CHUNK
                dst = pl.multiple_of(jnp.where(a >= 0, a, pad), CHUNK)
                src = o_buf.at[slot, pl.ds(j * CHUNK, CHUNK)]
                pltpu.make_async_copy(src, lo_hbm.at[pl.ds(dst, CHUNK)], osems.at[slot]).start()

        _skewed(chains, ((gate_up, 0), (down, 2), (activation, 1)))

    @pl.when(i == 0)
    def _():
        o_buf[0] = jnp.zeros_like(o_buf[0])

        def tails(start):
            def tile(t, _):
                def one(q, _):
                    dst = pl.multiple_of(t * LOCAL_ROWS + tail_ref[t] + q * CHUNK, CHUNK)
                    cp = pltpu.make_async_copy(o_buf.at[0, pl.ds(0, CHUNK)], lo_hbm.at[pl.ds(dst, CHUNK)], zsem)
                    if start:
                        cp.start()
                    else:
                        cp.wait()
                    return 0
                lax.fori_loop(0, (LOCAL_ROWS - tail_ref[t]) // CHUNK, one, 0)
                return 0
            lax.fori_loop(0, n_tiles, tile, 0)

        sink = [pltpu.make_async_copy(o_buf.at[0, pl.ds(0, n)], lo_hbm.at[pl.ds(spare + off, n)], zsem)
                for off, n in _sink_pieces()]
        tails(True)
        for cp in sink:
            cp.start()
        tails(False)
        for cp in sink:
            cp.wait()
        rows_in(0, 0)
        for cp in fetch(be_ref[0]):
            cp.start()
        for cp in fetch(be_ref[0]):
            cp.wait()
        prepare(*sets[0])

    @pl.when(live & ((ctl & CTL_FIRST) != 0))
    def _():
        for cp in fetch(nx_ref[i]):
            cp.start(priority=1)

    @pl.when(live)
    def _():
        block_done(x_buf, xsems, slot)

    @pl.when(live & (i >= 2))
    def _():
        block_done(o_buf, osems, slot)

    for odd in range(2):
        mine = live & (((ctl & CTL_ODD) != 0) == bool(odd))

        @pl.when(mine & ((ctl & CTL_LAST) == 0))
        def _(odd=odd):
            ffn(*sets[odd])

        @pl.when(mine & ((ctl & CTL_LAST) != 0))
        def _(odd=odd):
            for cp in fetch(nx_ref[i]):
                cp.wait()
            ffn(*sets[odd])
            prepare(*sets[1 - odd])

    @pl.when(i == n_used - 1)
    def _():
        block_done(o_buf, osems, slot)
        block_done(x_buf, xsems, 1 - slot)

        @pl.when(i >= 1)
        def _():
            block_done(o_buf, osems, 1 - slot)


def _expert_call(block_expert, next_expert, ctl, n_used, table, tail, ls, w_gate_up, w_down, bg, bu, bd, n_tiles):
    d = ls.shape[1]
    f = w_down.shape[1]
    n_blocks = block_expert.shape[0]
    bsel = lambda i, be, nx, ctl, nu, tb, tl: (be[i], 0, 0)
    anywhere = pl.BlockSpec(memory_space=pl.ANY)
    prepared = [pltpu.VMEM((f, d), BF16) for _ in range(6)]
    return pl.pallas_call(
        functools.partial(_expert_kernel, n_tiles=n_tiles),
        out_shape=jax.ShapeDtypeStruct(((n_tiles + 1) * LOCAL_ROWS, d), F32),
        grid_spec=pltpu.PrefetchScalarGridSpec(
            num_scalar_prefetch=6,
            grid=(n_blocks,),
            in_specs=[anywhere, anywhere, anywhere,
                      pl.BlockSpec((None, 1, f), bsel),
                      pl.BlockSpec((None, 1, f), bsel),
                      pl.BlockSpec((None, 1, d), bsel)],
            out_specs=anywhere,
            scratch_shapes=[pltpu.VMEM((2, EXPERT_ROWS, d), F32), pltpu.VMEM((2, EXPERT_ROWS, d), F32),
                            pltpu.VMEM((d, 2 * f), F32), pltpu.VMEM((f, d), F32), *prepared,
                            pltpu.VMEM((4, 2 * LANES, LANES), F32),
                            pltpu.SemaphoreType.DMA((2,)), pltpu.SemaphoreType.DMA((2,)),
                            pltpu.SemaphoreType.DMA((2,)), pltpu.SemaphoreType.DMA(())]),
        compiler_params=pltpu.CompilerParams(dimension_semantics=("arbitrary",),
                                             vmem_limit_bytes=VMEM_LIMIT),
        name="expert_ffn",
    )(block_expert, next_expert, ctl, n_used, table, tail, ls, w_gate_up, w_down, bg, bu, bd)


def _combine_kernel(lo_ref, lpos_ref, tw_ref, h_ref, lng_ref, lnb_ref, o_ref):
    rows_per = MXU_ROWS
    fill = jnp.zeros((LANES - 8, TILE), F32)
    lp_col = jnp.concatenate([lpos_ref[...].astype(F32), fill], axis=0).T
    tw_col = jnp.concatenate([tw_ref[...], fill], axis=0).T
    lane = lax.broadcasted_iota(I32, (TILE, rows_per), 1)
    pieces = [dict(j=j) for j in range(LOCAL_ROWS // rows_per)]
    acc = [jnp.zeros(o_ref.shape, F32)]

    def weigh(q):
        r = (lane + q["j"] * rows_per).astype(F32)
        w = jnp.zeros((TILE, rows_per), F32)
        for k in range(TOP_K):
            w = w + jnp.where(lp_col[:, k:k + 1] == r, tw_col[:, k:k + 1], 0.0)
        q["w"] = w.astype(BF16)

    def mix(q):
        rows = lo_ref[q["j"] * rows_per:(q["j"] + 1) * rows_per, :].astype(BF16)
        acc[0] = acc[0] + _dot(q.pop("w"), rows)

    _skewed(pieces, ((mix, 1), (weigh, 0)))
    o_ref[...] = _layer_norm(DEEPNORM_ALPHA * h_ref[...] + acc[0], lng_ref[...], lnb_ref[...])


def _combine_call(lo, lpos_t, tw_t, h, lng, lnb):
    t_tok, d = h.shape
    row8 = pl.BlockSpec((8, TILE), lambda i: (0, i))
    return pl.pallas_call(
        _combine_kernel,
        out_shape=jax.ShapeDtypeStruct((t_tok, d), F32),
        grid=(t_tok // TILE,),
        in_specs=[pl.BlockSpec((LOCAL_ROWS, d), lambda i: (i, 0)), row8, row8,
                  pl.BlockSpec((TILE, d), lambda i: (i, 0)),
                  pl.BlockSpec((1, d), lambda i: (0, 0)),
                  pl.BlockSpec((1, d), lambda i: (0, 0))],
        out_specs=pl.BlockSpec((TILE, d), lambda i: (i, 0)),
        compiler_params=pltpu.CompilerParams(dimension_semantics=("arbitrary",),
                                             vmem_limit_bytes=VMEM_LIMIT),
        name="moe_combine_ln",
    )(lo, lpos_t, tw_t, h, lng, lnb)


def _layer(x2, pos, batch, seq, w_in, b_gate, w_branch_sb, w_branch_moba, w_out, ln_mix_g, ln_mix_b,
           w_router, b_router, w_gate_up, b_gate_up, w_down, b_down, ln_ffn_g, ln_ffn_b):
    t_tok, d = x2.shape
    nb = seq // KV_BLOCK
    qkv_w = 6 * WIDTH
    n_tiles = t_tok // TILE

    w_qkv = w_in[:, :qkv_w].astype(BF16)
    w_g = w_in[:, qkv_w:].astype(BF16)
    pos3 = pos.astype(F32).reshape(n_tiles, 1, TILE)
    invf = (ROPE_THETA ** (-jnp.arange(ROPE_HALF, dtype=F32) / ROPE_HALF)).reshape(ROPE_HALF, 1)
    ki = jnp.arange(KV_BLOCK)
    tri = jnp.where(ki[None, :] > ki[:, None], -1.0, 0.0).astype(BF16)
    avg = jnp.where(jnp.arange(seq)[None, :] // KV_BLOCK == jnp.arange(nb)[:, None],
                    1.0 / KV_BLOCK, 0.0).astype(BF16)

    qt_sb, k_sb, vt_sb, qt_mb, k_mb, vt_mb = _qkv_call(x2, pos3, invf, w_qkv, TILE)
    a_sb = _sb_call(qt_sb, k_sb, vt_sb, tri, batch, seq)
    a_mb = _moba_call(qt_mb, k_mb, vt_mb, avg, batch, seq)

    w_r = w_router.T.astype(BF16)
    b_r = jnp.broadcast_to(b_router[:, None], (N_EXPERTS, LANES))
    h, ls, lpos_t, tw_t, cnts = _mixer_call(
        x2, a_sb, a_mb, w_g, b_gate.reshape(1, -1), w_branch_sb.astype(BF16), w_branch_moba.astype(BF16),
        w_out.astype(BF16), ln_mix_g.reshape(1, -1), ln_mix_b.reshape(1, -1), w_r, b_r)

    max_rows = t_tok * TOP_K + n_tiles * N_EXPERTS * (CHUNK - 1)
    n_blocks = -(-max_rows // EXPERT_ROWS) + N_EXPERTS
    n_blocks = -(-n_blocks // 16) * 16
    block_expert, next_expert, ctl, n_used, tail, table = _plan_call(cnts[:, :, 0].reshape(-1), n_tiles, n_blocks)
    table = table.reshape(-1)

    f = w_down.shape[1]
    lo = _expert_call(block_expert, next_expert, ctl, n_used, table, tail, ls, w_gate_up, w_down,
                      b_gate_up[:, 0::2].reshape(N_EXPERTS, 1, f), b_gate_up[:, 1::2].reshape(N_EXPERTS, 1, f),
                      b_down.reshape(N_EXPERTS, 1, d), n_tiles)
    return _combine_call(lo, lpos_t, tw_t, h, ln_ffn_g.reshape(1, -1), ln_ffn_b.reshape(1, -1))


def kernel(x, positions, w_in, b_gate, w_branch_sb, w_branch_moba, w_out, ln_mix_g, ln_mix_b, w_router,
           b_router, w_gate_up, b_gate_up, w_down, b_down, ln_ffn_g, ln_ffn_b):
    batch, seq, d = x.shape
    h = x.reshape(batch * seq, d)
    pos = positions.reshape(batch * seq)
    for layer in range(w_in.shape[0]):
        h = _layer(h, pos, batch, seq, w_in[layer], b_gate[layer], w_branch_sb[layer], w_branch_moba[layer],
                   w_out[layer], ln_mix_g[layer], ln_mix_b[layer], w_router[layer], b_router[layer],
                   w_gate_up[layer], b_gate_up[layer], w_down[layer], b_down[layer],
                   ln_ffn_g[layer], ln_ffn_b[layer])
    return h.reshape(batch, seq, d)
```

```python
import functools

import jax
import jax.numpy as jnp
from jax import lax
from jax.experimental import pallas as pl
from jax.experimental.pallas import tpu as pltpu

F32 = jnp.float32
BF16 = jnp.bfloat16
I32 = jnp.int32

HEAD_DIM = 64
N_HEADS = 8
WIDTH = N_HEADS * HEAD_DIM
LANES = 128
SUBLANES = 8
N_PAIRS = WIDTH // LANES
KV_BLOCK = 256
MOBA_TOPK = 3
ROPE_THETA = 500000.0
ROPE_HALF = 8
N_EXPERTS = 32
TOP_K = 4
SWIGLU_LIMIT = 7.0
SWIGLU_ALPHA = 1.702
EXPERT_ROWS = 512
MXU_ROWS = 256
TILE = 512
CHUNK = 8
LOCAL_ROWS = TILE * TOP_K + N_EXPERTS * CHUNK
CHUNKS_PER_BLOCK = EXPERT_ROWS // CHUNK
ZERO_ROW = LOCAL_ROWS - CHUNK
LN_EPS = 1e-5
DEPTH = 1
DEEPNORM_ALPHA = (2 * DEPTH) ** 0.25
QK_SCALE = HEAD_DIM ** -0.5
NEG_INF = float("-inf")
VMEM_LIMIT = 56 * 1024 * 1024


def _dot(a, b):
    return jnp.dot(a, b, preferred_element_type=F32)


def _qkv_kernel(x_ref, pos_ref, invf_ref, w_ref,
                qt_sb_ref, k_sb_ref, vt_sb_ref, qt_mb_ref, k_mb_ref, vt_mb_ref, *, tm):
    xb = x_ref[...].astype(BF16)
    ang = invf_ref[...] * pos_ref[0]
    cos, sin = jnp.cos(ang), jnp.sin(ang)

    def rope_t(t):
        parts = []
        for base in (0, HEAD_DIM):
            x1 = t[base:base + ROPE_HALF]
            x2 = t[base + ROPE_HALF:base + 2 * ROPE_HALF]
            parts += [x1 * cos - x2 * sin, x2 * cos + x1 * sin, t[base + 2 * ROPE_HALF:base + HEAD_DIM]]
        return jnp.concatenate(parts, axis=0)

    def store_t(ref, p, t):
        tb = t.astype(BF16)
        for blk in range(tm // KV_BLOCK):
            ref[blk, p] = tb[:, blk * KV_BLOCK:(blk + 1) * KV_BLOCK]

    for sec in range(6):
        for half in range(2):
            c0 = sec * WIDTH + half * 2 * LANES
            r = _dot(xb, w_ref[:, c0:c0 + 2 * LANES])
            for q in range(2):
                p = half * 2 + q
                t = r[:, q * LANES:(q + 1) * LANES]
                if sec == 0:
                    store_t(qt_sb_ref, p, (t * QK_SCALE).T)
                elif sec == 1:
                    k_sb_ref[:, p * LANES:(p + 1) * LANES] = t.astype(BF16)
                elif sec == 2:
                    store_t(vt_sb_ref, p, t.T)
                elif sec == 3:
                    store_t(qt_mb_ref, p, rope_t(t.T) * QK_SCALE)
                elif sec == 4:
                    k_mb_ref[:, p * LANES:(p + 1) * LANES] = rope_t(t.T).T.astype(BF16)
                else:
                    store_t(vt_mb_ref, p, t.T)


def _qkv_call(x2, pos3, invf, w_qkv, tm):
    t_tok, d = x2.shape
    nblk = t_tok // KV_BLOCK
    bpt = tm // KV_BLOCK
    t_shape = jax.ShapeDtypeStruct((nblk, N_PAIRS, LANES, KV_BLOCK), BF16)
    n_shape = jax.ShapeDtypeStruct((t_tok, WIDTH), BF16)
    t_spec = pl.BlockSpec((bpt, N_PAIRS, LANES, KV_BLOCK), lambda i: (i, 0, 0, 0))
    n_spec = pl.BlockSpec((tm, WIDTH), lambda i: (i, 0))
    return pl.pallas_call(
        functools.partial(_qkv_kernel, tm=tm),
        out_shape=(t_shape, n_shape, t_shape, t_shape, n_shape, t_shape),
        grid=(t_tok // tm,),
        in_specs=[pl.BlockSpec((tm, d), lambda i: (i, 0)),
                  pl.BlockSpec((1, 1, tm), lambda i: (i, 0, 0)),
                  pl.BlockSpec((ROPE_HALF, 1), lambda i: (0, 0)),
                  pl.BlockSpec((d, 6 * WIDTH), lambda i: (0, 0))],
        out_specs=(t_spec, n_spec, t_spec, t_spec, n_spec, t_spec),
        compiler_params=pltpu.CompilerParams(dimension_semantics=("arbitrary",),
                                             vmem_limit_bytes=VMEM_LIMIT),
        name="qkv_proj",
    )(x2, pos3, invf, w_qkv)


def _head_rows(h):
    row = lax.broadcasted_iota(I32, (LANES, KV_BLOCK), 0)
    return (row >= HEAD_DIM * h) & (row < HEAD_DIM * (h + 1))


def _tile_specs(seq):
    nb = seq // KV_BLOCK
    tiles_spec = pl.BlockSpec((nb, None, LANES, KV_BLOCK), lambda b, p, i: (b, p, 0, 0))
    col_spec = pl.BlockSpec((seq, LANES), lambda b, p, i: (b, p))
    return tiles_spec, col_spec


def _chain_list(i, nb, past_block):
    tiles = (i, nb - 1 - i)
    chains = [dict(j=tiles[a], a=a, h=h, diagonal=True, first=None) for a in range(2) for h in range(2)]
    for t in range(nb - 1):
        first = t < i
        for h in range(2):
            chains.append(dict(j=past_block(first, t), a=jnp.where(first, 0, 1), h=h, diagonal=False, first=first))
    return tiles, chains


def _skewed(chains, stages):
    n = len(chains)
    for slot in range(n + max(lag for _, lag in stages)):
        for stage, lag in stages:
            if 0 <= slot - lag < n:
                stage(chains[slot - lag])


def _sb_kernel(qt_ref, k_ref, vt_ref, tri_ref, o_ref, q_scr, acc_scr):
    i = pl.program_id(2)
    nb = qt_ref.shape[0]
    tiles, chains = _chain_list(i, nb, lambda first, t: jnp.where(first, i - 1 - t, nb - 2 - t))
    for a in range(2):
        qt = qt_ref[tiles[a]]
        for h in range(2):
            q_scr[a, h] = jnp.where(_head_rows(h), qt, jnp.zeros_like(qt))
    acc_scr[...] = jnp.zeros_like(acc_scr)
    key = lax.broadcasted_iota(I32, (KV_BLOCK, KV_BLOCK), 0)
    qry = lax.broadcasted_iota(I32, (KV_BLOCK, KV_BLOCK), 1)
    past = key < qry
    tri = tri_ref[...]

    zero = jnp.zeros((1, KV_BLOCK), F32)
    carries = [[zero, zero], [zero, zero]]

    def scores(c):
        kb = k_ref[pl.ds(pl.multiple_of(c["j"] * KV_BLOCK, KV_BLOCK), KV_BLOCK), :]
        c["z"] = _dot(kb, q_scr[c["a"], c["h"]])

    def softplus(c):
        z = c.pop("z")
        sp = jnp.maximum(z, 0.0) + jnp.log(1.0 + jnp.exp(-jnp.abs(z)))
        spm = jnp.where(past, sp, 0.0) if c["diagonal"] else sp
        c["log_beta"] = z - sp
        c["spm"] = spm.astype(BF16)
        c["sp0"] = spm[0:1, :]

    def suffix(c):
        c["after"] = _dot(tri, c.pop("spm"))

    def weights(c):
        after = c.pop("after")
        w = jnp.exp(c.pop("log_beta") + after)
        if c["diagonal"]:
            w = jnp.where(past, w, 0.0)
        c["w"] = w.astype(BF16)
        total = after[0:1, :] - c.pop("sp0")
        h, first = c["h"], c["first"]
        if first is None:
            c["carry"] = zero
            carries[c["a"]][h] = total
        else:
            c["carry"] = jnp.where(first, carries[0][h], carries[1][h])
            cout = c["carry"] + total
            carries[0][h] = jnp.where(first, cout, carries[0][h])
            carries[1][h] = jnp.where(first, carries[1][h], cout)

    def values(c):
        rows = slice(c["h"] * HEAD_DIM, (c["h"] + 1) * HEAD_DIM)
        pv = _dot(vt_ref[c["j"], rows, :], c.pop("w")) * jnp.exp(c.pop("carry"))
        acc_scr[c["a"], rows, :] += pv

    _skewed(chains, ((scores, 0), (suffix, 2), (values, 4), (softplus, 1), (weights, 3)))
    for a in range(2):
        o_ref[pl.ds(pl.multiple_of(tiles[a] * KV_BLOCK, KV_BLOCK), KV_BLOCK), :] = acc_scr[a].T.astype(BF16)


def _sb_call(qt, k, vt, tri, batch, seq):
    nb = seq // KV_BLOCK
    tiles_spec, col_spec = _tile_specs(seq)
    return pl.pallas_call(
        _sb_kernel,
        out_shape=jax.ShapeDtypeStruct((batch * seq, WIDTH), BF16),
        grid=(batch, N_PAIRS, nb // 2),
        in_specs=[tiles_spec, col_spec, tiles_spec,
                  pl.BlockSpec((KV_BLOCK, KV_BLOCK), lambda b, p, i: (0, 0))],
        out_specs=col_spec,
        scratch_shapes=[pltpu.VMEM((2, 2, LANES, KV_BLOCK), BF16), pltpu.VMEM((2, LANES, KV_BLOCK), F32)],
        compiler_params=pltpu.CompilerParams(
            dimension_semantics=("arbitrary", "arbitrary", "arbitrary"), vmem_limit_bytes=VMEM_LIMIT),
        name="stickbreak_attn",
    )(qt, k, vt, tri)


def _moba_kernel(qt_ref, k_ref, vt_ref, avg_ref, o_ref, km_ref, bias_scr, q_scr, acc_scr, s_scr, den_scr):
    i = pl.program_id(2)
    nb = qt_ref.shape[0]

    @pl.when(i == 0)
    def _():
        km_ref[...] = _dot(avg_ref[...], k_ref[...])

    tiles, chains = _chain_list(i, nb, lambda first, t: jnp.where(first, t, t - i))
    key = lax.broadcasted_iota(I32, (KV_BLOCK, KV_BLOCK), 0)
    qry = lax.broadcasted_iota(I32, (KV_BLOCK, KV_BLOCK), 1)
    causal = key <= qry
    blk = lax.broadcasted_iota(I32, (nb, KV_BLOCK), 0)
    km = km_ref[...].astype(BF16)
    for a in range(2):
        qt = qt_ref[tiles[a]]
        valid = blk < tiles[a]
        for h in range(2):
            qth = jnp.where(_head_rows(h), qt, jnp.zeros_like(qt))
            q_scr[a, h] = qth
            g = jnp.where(valid, _dot(km, qth), NEG_INF)
            rank = jnp.zeros((nb, KV_BLOCK), F32)
            for jp in range(nb):
                gj = g[jp:jp + 1, :]
                better = jnp.where(gj > g, 1.0, jnp.where(gj == g, jnp.where(blk > jp, 1.0, 0.0), 0.0))
                rank = rank + better
            bias_scr[a, h] = jnp.where(valid, jnp.where(rank < MOBA_TOPK, 0.0, NEG_INF), NEG_INF)

    for n, c in enumerate(chains):
        c["n"] = n
    low = jnp.full((SUBLANES, KV_BLOCK), NEG_INF, F32)
    maxima = [[low, low], [low, low]]
    acc_scr[...] = jnp.zeros_like(acc_scr)
    den_scr[...] = jnp.zeros_like(den_scr)
    ones = jnp.ones((2 * SUBLANES, KV_BLOCK), BF16)

    def scores(c):
        kb = k_ref[pl.ds(pl.multiple_of(c["j"] * KV_BLOCK, KV_BLOCK), KV_BLOCK), :]
        c["s"] = _dot(kb, q_scr[c["a"], c["h"]])

    def mask(c):
        h, first = c["h"], c["first"]
        if first is None:
            s = jnp.where(causal, c.pop("s"), NEG_INF)
        else:
            s = c.pop("s") + bias_scr[c["a"], h, pl.ds(c["j"], 1), :]
        s_scr[c["n"]] = s
        top = jnp.max(s.reshape(KV_BLOCK // SUBLANES, SUBLANES, KV_BLOCK), axis=0)
        if first is None:
            maxima[c["a"]][h] = top
        else:
            maxima[0][h] = jnp.where(first, jnp.maximum(maxima[0][h], top), maxima[0][h])
            maxima[1][h] = jnp.where(first, maxima[1][h], jnp.maximum(maxima[1][h], top))

    _skewed(chains, ((scores, 0), (mask, 1)))
    maxima = [[jnp.max(maxima[a][h], axis=0, keepdims=True) for h in range(2)] for a in range(2)]

    def weights(c):
        h, first = c["h"], c["first"]
        m = maxima[c["a"]][h] if first is None else jnp.where(first, maxima[0][h], maxima[1][h])
        c["p"] = jnp.exp(s_scr[c["n"]] - m).astype(BF16)

    def values(c):
        rows = slice(c["h"] * HEAD_DIM, (c["h"] + 1) * HEAD_DIM)
        pv = _dot(jnp.concatenate([vt_ref[c["j"], rows, :], ones], axis=0), c.pop("p"))
        acc_scr[c["a"], rows, :] += pv[:HEAD_DIM]
        den_scr[c["a"], c["h"]] += pv[HEAD_DIM:]

    _skewed(chains, ((values, 1), (weights, 0)))
    for a in range(2):
        for h in range(2):
            rows = slice(h * HEAD_DIM, (h + 1) * HEAD_DIM)
            acc_scr[a, rows, :] = acc_scr[a, rows, :] / den_scr[a, h, 0:1, :]
        o_ref[pl.ds(pl.multiple_of(tiles[a] * KV_BLOCK, KV_BLOCK), KV_BLOCK), :] = acc_scr[a].T.astype(BF16)


def _moba_call(qt, k, vt, avg, batch, seq):
    nb = seq // KV_BLOCK
    tiles_spec, col_spec = _tile_specs(seq)
    return pl.pallas_call(
        _moba_kernel,
        out_shape=jax.ShapeDtypeStruct((batch * seq, WIDTH), BF16),
        grid=(batch, N_PAIRS, nb // 2),
        in_specs=[tiles_spec, col_spec, tiles_spec, pl.BlockSpec((nb, seq), lambda b, p, i: (0, 0))],
        out_specs=col_spec,
        scratch_shapes=[pltpu.VMEM((nb, LANES), F32), pltpu.VMEM((2, 2, nb, KV_BLOCK), F32),
                        pltpu.VMEM((2, 2, LANES, KV_BLOCK), BF16), pltpu.VMEM((2, LANES, KV_BLOCK), F32),
                        pltpu.VMEM((2 * (nb + 1), KV_BLOCK, KV_BLOCK), F32),
                        pltpu.VMEM((2, 2, 2 * SUBLANES, KV_BLOCK), F32)],
        compiler_params=pltpu.CompilerParams(
            dimension_semantics=("arbitrary", "arbitrary", "arbitrary"), vmem_limit_bytes=VMEM_LIMIT),
        name="moba_attn",
    )(qt, k, vt, avg)


def _layer_norm(r, g, b):
    mu = jnp.mean(r, axis=-1, keepdims=True)
    d = r - mu
    var = jnp.mean(d * d, axis=-1, keepdims=True)
    return d * lax.rsqrt(var + LN_EPS) * g + b


def _mixer_kernel(x_ref, asb_ref, amb_ref, wg_ref, bg_ref, wbs_ref, wbm_ref, wo_ref, lng_ref, lnb_ref,
                  wr_ref, br_ref, h_ref, ls_ref, lpos_ref, tw_ref, cnts_ref, *, d):
    rows_per = MXU_ROWS
    chains = [dict(rows=slice(c * rows_per, (c + 1) * rows_per)) for c in range(TILE // rows_per)]
    expert = lax.broadcasted_iota(I32, (N_EXPERTS, rows_per), 0)
    slot8 = lax.broadcasted_iota(I32, (8, rows_per), 0)
    rt = lax.broadcasted_iota(I32, (rows_per, rows_per), 0)
    ct = lax.broadcasted_iota(I32, (rows_per, rows_per), 1)
    earlier = jnp.where(rt < ct, 1.0, 0.0).astype(BF16)
    seen = [jnp.zeros((N_EXPERTS, LANES), F32)]

    def project(c):
        c["gp"] = _dot(x_ref[c["rows"], :].astype(BF16), wg_ref[...])
        c["ysb"] = _dot(asb_ref[c["rows"], :], wbs_ref[...])
        c["ymb"] = _dot(amb_ref[c["rows"], :], wbm_ref[...])

    def gate(c):
        g = jax.nn.sigmoid(c.pop("gp") + bg_ref[...])
        c["mixed"] = (g[:, :d] * c.pop("ysb") + g[:, d:] * c.pop("ymb")).astype(BF16)

    def out_proj(c):
        c["mix"] = _dot(c.pop("mixed"), wo_ref[...])

    def norm(c):
        h = _layer_norm(DEEPNORM_ALPHA * x_ref[c["rows"], :] + c.pop("mix"), lng_ref[...], lnb_ref[...])
        h_ref[c["rows"], :] = h
        c["hb"] = h.astype(BF16)

    def route(c):
        nt = (((1,), (1,)), ((), ()))
        c["logits"] = lax.dot_general(wr_ref[...], c["hb"], nt, preferred_element_type=F32)

    def top_k(c):
        bias = br_ref[...]
        logits = c.pop("logits") + jnp.concatenate([bias] * (rows_per // LANES), axis=1)
        vals, idxs = [], []
        member = jnp.zeros((N_EXPERTS, rows_per), F32)
        for _ in range(TOP_K):
            mx = jnp.max(logits, axis=0, keepdims=True)
            ik = jnp.min(jnp.where(logits == mx, expert, N_EXPERTS), axis=0, keepdims=True)
            hit = expert == ik
            vals.append(mx)
            idxs.append(ik)
            member = jnp.where(hit, 1.0, member)
            logits = jnp.where(hit, NEG_INF, logits)
        es = [jnp.exp(v - vals[0]) for v in vals]
        den = es[0] + es[1] + es[2] + es[3]
        c["tw"] = [e / den for e in es]
        c["idxs"] = idxs
        c["member"] = member

    def count(c):
        member = c.pop("member")
        c["rank"] = _dot(member.astype(BF16), earlier) + jnp.concatenate([seen[0]] * (rows_per // LANES), axis=1)
        seen[0] = seen[0] + jnp.sum(member, axis=1, keepdims=True)

    _skewed(chains, ((project, 0), (out_proj, 2), (route, 4), (count, 6), (gate, 1), (norm, 3), (top_k, 5)))

    counts = seen[0]
    cnts_ref[...] = counts.astype(I32)
    seg = jnp.ceil(counts * (1.0 / CHUNK)) * CHUNK
    e_r = lax.broadcasted_iota(I32, (N_EXPERTS, N_EXPERTS), 0)
    e_c = lax.broadcasted_iota(I32, (N_EXPERTS, N_EXPERTS), 1)
    loff = _dot(jnp.where(e_c < e_r, 1.0, 0.0).astype(BF16), seg.astype(BF16))
    loff = jnp.concatenate([loff] * (rows_per // LANES), axis=1)
    lpos, tws = [], []
    for c in chains:
        where_to = c.pop("rank") + loff
        lp = jnp.full((8, rows_per), -1.0, F32)
        tw = jnp.zeros((8, rows_per), F32)
        for k, ik in enumerate(c.pop("idxs")):
            pk = jnp.sum(jnp.where(expert == ik, where_to, 0.0), axis=0, keepdims=True)
            lp = jnp.where(slot8 == k, pk, lp)
            tw = jnp.where(slot8 == k, c["tw"][k], tw)
        lpos.append(lp)
        tws.append(tw)
    lpos = jnp.concatenate(lpos, axis=1)
    lpos_ref[...] = lpos.astype(I32)
    tw_ref[...] = jnp.concatenate(tws, axis=1)

    hb = jnp.concatenate([c["hb"] for c in chains], axis=0)
    pieces = [dict(j=j) for j in range(LOCAL_ROWS // rows_per)]
    row = lax.broadcasted_iota(I32, (rows_per, TILE), 0)

    def select(q):
        r = (row + q["j"] * rows_per).astype(F32)
        sel = jnp.zeros((rows_per, TILE), F32)
        for k in range(TOP_K):
            sel = jnp.where(lpos[k:k + 1, :] == r, 1.0, sel)
        q["sel"] = sel.astype(BF16)

    def place(q):
        ls_ref[q["j"] * rows_per:(q["j"] + 1) * rows_per, :] = _dot(q.pop("sel"), hb)

    _skewed(pieces, ((place, 1), (select, 0)))


def _mixer_call(x2, a_sb, a_mb, wg, bg, wbs, wbm, wo, lng, lnb, wr, br):
    t_tok, d = x2.shape
    n_tiles = t_tok // TILE
    const = lambda shape: pl.BlockSpec(shape, lambda i: (0,) * len(shape))
    row8 = pl.BlockSpec((8, TILE), lambda i: (0, i))
    return pl.pallas_call(
        functools.partial(_mixer_kernel, d=d),
        out_shape=(jax.ShapeDtypeStruct((t_tok, d), F32),
                   jax.ShapeDtypeStruct((n_tiles * LOCAL_ROWS, d), F32),
                   jax.ShapeDtypeStruct((8, t_tok), I32),
                   jax.ShapeDtypeStruct((8, t_tok), F32),
                   jax.ShapeDtypeStruct((n_tiles, N_EXPERTS, LANES), I32)),
        grid=(n_tiles,),
        in_specs=[pl.BlockSpec((TILE, d), lambda i: (i, 0)),
                  pl.BlockSpec((TILE, WIDTH), lambda i: (i, 0)),
                  pl.BlockSpec((TILE, WIDTH), lambda i: (i, 0)),
                  const((d, 2 * d)), const((1, 2 * d)),
                  const((WIDTH, d)), const((WIDTH, d)), const((d, d)),
                  const((1, d)), const((1, d)),
                  const((N_EXPERTS, d)), const((N_EXPERTS, LANES))],
        out_specs=(pl.BlockSpec((TILE, d), lambda i: (i, 0)),
                   pl.BlockSpec((LOCAL_ROWS, d), lambda i: (i, 0)),
                   row8, row8,
                   pl.BlockSpec((None, N_EXPERTS, LANES), lambda i: (i, 0, 0))),
        compiler_params=pltpu.CompilerParams(dimension_semantics=("arbitrary",),
                                             vmem_limit_bytes=VMEM_LIMIT),
        name="mixer_ln_router",
    )(x2, a_sb, a_mb, wg, bg, wbs, wbm, wo, lng, lnb, wr, br)


CTL_FIRST = 1
CTL_LAST = 2
CTL_ODD = 4


def _plan_kernel(cnt_ref, be_ref, nx_ref, ctl_ref, nu_ref, tail_ref, table_ref,
                 size_ref, follow_ref, first_ref, count_ref, base_ref, *, n_tiles):
    def seg(t, e):
        return (cnt_ref[t * N_EXPERTS + e] + CHUNK - 1) // CHUNK * CHUNK

    def size_of(e, _):
        size_ref[e] = lax.fori_loop(0, n_tiles, lambda t, a: a + seg(t, e), 0, unroll=8)
        return 0

    lax.fori_loop(0, N_EXPERTS, size_of, 0)

    def nonempty_after(t, nxt):
        e = N_EXPERTS - 1 - t
        follow_ref[e] = nxt
        return jnp.where(size_ref[e] > 0, e, nxt)

    lax.fori_loop(0, N_EXPERTS, nonempty_after, -1)

    def clear_block(b, _):
        be_ref[b] = 0
        nx_ref[b] = 0
        ctl_ref[b] = 0
        return 0

    lax.fori_loop(0, be_ref.shape[0], clear_block, 0)

    def clear_tile(t, _):
        tail_ref[t] = 0
        return 0

    lax.fori_loop(0, n_tiles, clear_tile, 0)

    def blocks_of(e, state):
        b0, odd = state
        n = (size_ref[e] + EXPERT_ROWS - 1) // EXPERT_ROWS
        follows = follow_ref[e] >= 0

        def block(j, _):
            be_ref[b0 + j] = e
            nx_ref[b0 + j] = jnp.maximum(follow_ref[e], 0)
            ctl_ref[b0 + j] = (jnp.where(follows & (j == 0), CTL_FIRST, 0)
                               + jnp.where(follows & (j == n - 1), CTL_LAST, 0) + odd * CTL_ODD)
            return 0

        lax.fori_loop(0, n, block, 0)

        def segment(t, pos):
            lo = tail_ref[t]
            chunks = seg(t, e) // CHUNK
            first_ref[t * N_EXPERTS + e] = pos
            count_ref[t * N_EXPERTS + e] = chunks
            base_ref[t * N_EXPERTS + e] = t * LOCAL_ROWS + lo
            tail_ref[t] = lo + chunks * CHUNK
            return pos + chunks

        lax.fori_loop(0, n_tiles, segment, b0 * CHUNKS_PER_BLOCK, unroll=4)
        return b0 + n, jnp.where(n > 0, 1 - odd, odd)

    n_used, _ = lax.fori_loop(0, N_EXPERTS, blocks_of, (0, 0))
    nu_ref[0] = n_used

    slot = (lax.broadcasted_iota(I32, table_ref.shape, 0) * LANES
            + lax.broadcasted_iota(I32, table_ref.shape, 1))

    def fill(s, table):
        first = first_ref[s]
        inside = (slot >= first) & (slot < first + count_ref[s])
        return jnp.where(inside, base_ref[s] + (slot - first) * CHUNK, table)

    table_ref[...] = lax.fori_loop(0, n_tiles * N_EXPERTS, fill, jnp.full(table_ref.shape, -1, I32), unroll=4)


def _plan_call(tile_counts, n_tiles, n_blocks):
    smem = pl.BlockSpec(memory_space=pltpu.SMEM)
    blocks = jax.ShapeDtypeStruct((n_blocks,), I32)
    segments = pltpu.SMEM((n_tiles * N_EXPERTS,), I32)
    return pl.pallas_call(
        functools.partial(_plan_kernel, n_tiles=n_tiles),
        out_shape=(blocks, blocks, blocks, jax.ShapeDtypeStruct((1,), I32),
                   jax.ShapeDtypeStruct((n_tiles,), I32),
                   jax.ShapeDtypeStruct((n_blocks * CHUNKS_PER_BLOCK // LANES, LANES), I32)),
        in_specs=[smem],
        out_specs=(smem,) * 5 + (pl.BlockSpec(memory_space=pltpu.VMEM),),
        scratch_shapes=[pltpu.SMEM((N_EXPERTS,), I32), pltpu.SMEM((N_EXPERTS,), I32), segments, segments, segments],
        name="moe_plan",
    )(tile_counts)


def _sink_pieces():
    return [(off, min(EXPERT_ROWS, LOCAL_ROWS - off)) for off in range(0, LOCAL_ROWS, EXPERT_ROWS)]


def _expert_kernel(be_ref, nx_ref, ctl_ref, nu_ref, table_ref, tail_ref,
                   ls_hbm, wgu_hbm, wd_hbm, bg_ref, bu_ref, bd_ref, lo_hbm,
                   x_buf, o_buf, wgu_stage, wd_stage, wgt0, wut0, wdb0, wgt1, wut1, wdb1, tmp_ref,
                   wsems, xsems, osems, zsem, *, n_tiles):
    i = pl.program_id(0)
    slot = i % 2
    n_used = nu_ref[0]
    live = i < n_used
    ctl = ctl_ref[i]
    sets = ((wgt0, wut0, wdb0), (wgt1, wut1, wdb1))
    chunk = 2 * LANES
    spare = n_tiles * LOCAL_ROWS

    def fetch(e):
        return (pltpu.make_async_copy(wgu_hbm.at[e], wgu_stage, wsems.at[0]),
                pltpu.make_async_copy(wd_hbm.at[e], wd_stage, wsems.at[1]))

    def prepare(wgt_ref, wut_ref, wdb_ref):
        for c in range(wgu_stage.shape[1] // chunk):
            rows = slice(c * LANES, (c + 1) * LANES)
            for k in range(wgu_stage.shape[0] // LANES):
                cols = slice(k * LANES, (k + 1) * LANES)
                tslot = (c * (wgu_stage.shape[0] // LANES) + k) % tmp_ref.shape[0]
                tmp_ref[tslot] = wgu_stage[cols, c * chunk:(c + 1) * chunk].T
                wgt_ref[rows, cols] = tmp_ref[tslot, pl.ds(0, LANES, stride=2), :].astype(BF16)
                wut_ref[rows, cols] = tmp_ref[tslot, pl.ds(1, LANES, stride=2), :].astype(BF16)
        wdb_ref[...] = wd_stage[...].astype(BF16)

    def rows_in(b, s):
        def one(j, _):
            a = table_ref[b * CHUNKS_PER_BLOCK + j]
            src = pl.multiple_of(jnp.where(a >= 0, a, ZERO_ROW), CHUNK)
            dst = x_buf.at[s, pl.ds(pl.multiple_of(j * CHUNK, CHUNK), CHUNK)]
            pltpu.make_async_copy(ls_hbm.at[pl.ds(src, CHUNK)], dst, xsems.at[s]).start()
            return 0
        lax.fori_loop(0, CHUNKS_PER_BLOCK, one, 0)

    def block_done(buf, sems, s):
        pltpu.make_async_copy(buf.at[s], buf.at[s], sems.at[s]).wait()

    def ffn(wgt_ref, wut_ref, wdb_ref):
        nxt = jnp.minimum(i + 1, pl.num_programs(0) - 1)
        for j in range(CHUNKS_PER_BLOCK):
            a = table_ref[nxt * CHUNKS_PER_BLOCK + j]
            src = pl.multiple_of(jnp.where(a >= 0, a, ZERO_ROW), CHUNK)
            dst = x_buf.at[1 - slot, pl.ds(j * CHUNK, CHUNK)]
            pltpu.make_async_copy(ls_hbm.at[pl.ds(src, CHUNK)], dst, xsems.at[1 - slot]).start()

        nt = (((1,), (1,)), ((), ()))
        chains = [dict(rows=slice(c * MXU_ROWS, (c + 1) * MXU_ROWS)) for c in range(EXPERT_ROWS // MXU_ROWS)]

        def gate_up(c):
            x = x_buf[slot, c["rows"], :].astype(BF16)
            c["gate"] = lax.dot_general(x, wgt_ref[...], nt, preferred_element_type=F32)
            c["up"] = lax.dot_general(x, wut_ref[...], nt, preferred_element_type=F32)

        def activation(c):
            gate = jnp.minimum(c.pop("gate") + bg_ref[...], SWIGLU_LIMIT)
            up = jnp.clip(c.pop("up") + bu_ref[...], -SWIGLU_LIMIT, SWIGLU_LIMIT)
            c["act"] = ((up + 1.0) * gate * jax.nn.sigmoid(SWIGLU_ALPHA * gate)).astype(BF16)

        def down(c):
            o_buf[slot, c["rows"], :] = _dot(c.pop("act"), wdb_ref[...]) + bd_ref[...]
            for j in range(c["rows"].start // CHUNK, c["rows"].stop // CHUNK):
                a = table_ref[i * CHUNKS_PER_BLOCK + j]
                pad = spare + slot * EXPERT_ROWS + j * CHUNK
                dst = pl.multiple_of(jnp.where(a >= 0, a, pad), CHUNK)
                src = o_buf.at[slot, pl.ds(j * CHUNK, CHUNK)]
                pltpu.make_async_copy(src, lo_hbm.at[pl.ds(dst, CHUNK)], osems.at[slot]).start()

        _skewed(chains, ((gate_up, 0), (down, 2), (activation, 1)))

    @pl.when(i == 0)
    def _():
        o_buf[0] = jnp.zeros_like(o_buf[0])

        def tails(start):
            def tile(t, _):
                def one(q, _):
                    dst = pl.multiple_of(t * LOCAL_ROWS + tail_ref[t] + q * CHUNK, CHUNK)
                    cp = pltpu.make_async_copy(o_buf.at[0, pl.ds(0, CHUNK)], lo_hbm.at[pl.ds(dst, CHUNK)], zsem)
                    if start:
                        cp.start()
                    else:
                        cp.wait()
                    return 0
                lax.fori_loop(0, (LOCAL_ROWS - tail_ref[t]) // CHUNK, one, 0)
                return 0
            lax.fori_loop(0, n_tiles, tile, 0)

        sink = [pltpu.make_async_copy(o_buf.at[0, pl.ds(0, n)], lo_hbm.at[pl.ds(spare + off, n)], zsem)
                for off, n in _sink_pieces()]
        tails(True)
        for cp in sink:
            cp.start()
        tails(False)
        for cp in sink:
            cp.wait()
        rows_in(0, 0)
        for cp in fetch(be_ref[0]):
            cp.start()
        for cp in fetch(be_ref[0]):
            cp.wait()
        prepare(*sets[0])

    @pl.when(live & ((ctl & CTL_FIRST) != 0))
    def _():
        for cp in fetch(nx_ref[i]):
            cp.start(priority=1)

    @pl.when(live)
    def _():
        block_done(x_buf, xsems, slot)

    @pl.when(live & (i >= 2))
    def _():
        block_done(o_buf, osems, slot)

    for odd in range(2):
        mine = live & (((ctl & CTL_ODD) != 0) == bool(odd))

        @pl.when(mine & ((ctl & CTL_LAST) == 0))
        def _(odd=odd):
            ffn(*sets[odd])

        @pl.when(mine & ((ctl & CTL_LAST) != 0))
        def _(odd=odd):
            for cp in fetch(nx_ref[i]):
                cp.wait()
            ffn(*sets[odd])
            prepare(*sets[1 - odd])

    @pl.when(i == n_used - 1)
    def _():
        block_done(o_buf, osems, slot)
        block_done(x_buf, xsems, 1 - slot)

        @pl.when(i >= 1)
        def _():
            block_done(o_buf, osems, 1 - slot)


def _expert_call(block_expert, next_expert, ctl, n_used, table, tail, ls, w_gate_up, w_down, bg, bu, bd, n_tiles):
    d = ls.shape[1]
    f = w_down.shape[1]
    n_blocks = block_expert.shape[0]
    bsel = lambda i, be, nx, ctl, nu, tb, tl: (be[i], 0, 0)
    anywhere = pl.BlockSpec(memory_space=pl.ANY)
    prepared = [pltpu.VMEM((f, d), BF16) for _ in range(6)]
    return pl.pallas_call(
        functools.partial(_expert_kernel, n_tiles=n_tiles),
        out_shape=jax.ShapeDtypeStruct(((n_tiles + 1) * LOCAL_ROWS, d), F32),
        grid_spec=pltpu.PrefetchScalarGridSpec(
            num_scalar_prefetch=6,
            grid=(n_blocks,),
            in_specs=[anywhere, anywhere, anywhere,
                      pl.BlockSpec((None, 1, f), bsel),
                      pl.BlockSpec((None, 1, f), bsel),
                      pl.BlockSpec((None, 1, d), bsel)],
            out_specs=anywhere,
            scratch_shapes=[pltpu.VMEM((2, EXPERT_ROWS, d), F32), pltpu.VMEM((2, EXPERT_ROWS, d), F32),
                            pltpu.VMEM((d, 2 * f), F32), pltpu.VMEM((f, d), F32), *prepared,
                            pltpu.VMEM((4, 2 * LANES, LANES), F32),
                            pltpu.SemaphoreType.DMA((2,)), pltpu.SemaphoreType.DMA((2,)),
                            pltpu.SemaphoreType.DMA((2,)), pltpu.SemaphoreType.DMA(())]),
        compiler_params=pltpu.CompilerParams(dimension_semantics=("arbitrary",),
                                             vmem_limit_bytes=VMEM_LIMIT),
        name="expert_ffn",
    )(block_expert, next_expert, ctl, n_used, table, tail, ls, w_gate_up, w_down, bg, bu, bd)


def _combine_kernel(lo_ref, lpos_ref, tw_ref, h_ref, lng_ref, lnb_ref, o_ref):
    rows_per = MXU_ROWS
    fill = jnp.zeros((LANES - 8, TILE), F32)
    lp_col = jnp.concatenate([lpos_ref[...].astype(F32), fill], axis=0).T
    tw_col = jnp.concatenate([tw_ref[...], fill], axis=0).T
    lane = lax.broadcasted_iota(I32, (TILE, rows_per), 1)
    group = 3
    pieces = [dict(j=j) for j in range(0, LOCAL_ROWS // rows_per, group)]
    acc = [jnp.zeros(o_ref.shape, F32)]

    def weigh(q):
        ws = []
        for j in range(q["j"], q["j"] + group):
            r = (lane + j * rows_per).astype(F32)
            w = jnp.zeros((TILE, rows_per), F32)
            for k in range(TOP_K):
                w = jnp.where(lp_col[:, k:k + 1] == r, tw_col[:, k:k + 1], w)
            ws.append(w.astype(BF16))
        q["w"] = jnp.concatenate(ws, axis=1)

    def mix(q):
        rows = lo_ref[q["j"] * rows_per:(q["j"] + group) * rows_per, :].astype(BF16)
        acc[0] = acc[0] + _dot(q.pop("w"), rows)

    _skewed(pieces, ((mix, 1), (weigh, 0)))
    o_ref[...] = _layer_norm(DEEPNORM_ALPHA * h_ref[...] + acc[0], lng_ref[...], lnb_ref[...])


def _combine_call(lo, lpos_t, tw_t, h, lng, lnb):
    t_tok, d = h.shape
    row8 = pl.BlockSpec((8, TILE), lambda i: (0, i))
    return pl.pallas_call(
        _combine_kernel,
        out_shape=jax.ShapeDtypeStruct((t_tok, d), F32),
        grid=(t_tok // TILE,),
        in_specs=[pl.BlockSpec((LOCAL_ROWS, d), lambda i: (i, 0)), row8, row8,
                  pl.BlockSpec((TILE, d), lambda i: (i, 0)),
                  pl.BlockSpec((1, d), lambda i: (0, 0)),
                  pl.BlockSpec((1, d), lambda i: (0, 0))],
        out_specs=pl.BlockSpec((TILE, d), lambda i: (i, 0)),
        compiler_params=pltpu.CompilerParams(dimension_semantics=("arbitrary",),
                                             vmem_limit_bytes=VMEM_LIMIT),
        name="moe_combine_ln",
    )(lo, lpos_t, tw_t, h, lng, lnb)


def _layer(x2, pos, batch, seq, w_in, b_gate, w_branch_sb, w_branch_moba, w_out, ln_mix_g, ln_mix_b,
           w_router, b_router, w_gate_up, b_gate_up, w_down, b_down, ln_ffn_g, ln_ffn_b):
    t_tok, d = x2.shape
    nb = seq // KV_BLOCK
    qkv_w = 6 * WIDTH
    n_tiles = t_tok // TILE

    w_bf = w_in.astype(BF16)
    w_qkv = w_bf[:, :qkv_w]
    w_g = w_bf[:, qkv_w:]
    pos3 = pos.astype(F32).reshape(n_tiles, 1, TILE)
    invf = (ROPE_THETA ** (-jnp.arange(ROPE_HALF, dtype=F32) / ROPE_HALF)).reshape(ROPE_HALF, 1)
    ki = jnp.arange(KV_BLOCK)
    tri = jnp.where(ki[None, :] > ki[:, None], -1.0, 0.0).astype(BF16)
    avg = jnp.where(jnp.arange(seq)[None, :] // KV_BLOCK == jnp.arange(nb)[:, None],
                    1.0 / KV_BLOCK, 0.0).astype(BF16)

    qt_sb, k_sb, vt_sb, qt_mb, k_mb, vt_mb = _qkv_call(x2, pos3, invf, w_qkv, TILE)
    a_sb = _sb_call(qt_sb, k_sb, vt_sb, tri, batch, seq)
    a_mb = _moba_call(qt_mb, k_mb, vt_mb, avg, batch, seq)

    w_r = w_router.T.astype(BF16)
    b_r = jnp.broadcast_to(b_router[:, None], (N_EXPERTS, LANES))
    h, ls, lpos_t, tw_t, cnts = _mixer_call(
        x2, a_sb, a_mb, w_g, b_gate.reshape(1, -1), w_branch_sb.astype(BF16), w_branch_moba.astype(BF16),
        w_out.astype(BF16), ln_mix_g.reshape(1, -1), ln_mix_b.reshape(1, -1), w_r, b_r)

    max_rows = t_tok * TOP_K + n_tiles * N_EXPERTS * (CHUNK - 1)
    n_blocks = -(-max_rows // EXPERT_ROWS) + N_EXPERTS
    n_blocks = -(-n_blocks // 16) * 16
    block_expert, next_expert, ctl, n_used, tail, table = _plan_call(cnts[:, :, 0].reshape(-1), n_tiles, n_blocks)
    table = table.reshape(-1)

    f = w_down.shape[1]
    lo = _expert_call(block_expert, next_expert, ctl, n_used, table, tail, ls, w_gate_up, w_down,
                      b_gate_up[:, 0::2].reshape(N_EXPERTS, 1, f), b_gate_up[:, 1::2].reshape(N_EXPERTS, 1, f),
                      b_down.reshape(N_EXPERTS, 1, d), n_tiles)
    return _combine_call(lo, lpos_t, tw_t, h, ln_ffn_g.reshape(1, -1), ln_ffn_b.reshape(1, -1))


def kernel(x, positions, w_in, b_gate, w_branch_sb, w_branch_moba, w_out, ln_mix_g, ln_mix_b, w_router,
           b_router, w_gate_up, b_gate_up, w_down, b_down, ln_ffn_g, ln_ffn_b):
    batch, seq, d = x.shape
    h = x.reshape(batch * seq, d)
    pos = positions.reshape(batch * seq)
    for layer in range(w_in.shape[0]):
        h = _layer(h, pos, batch, seq, w_in[layer], b_gate[layer], w_branch_sb[layer], w_branch_moba[layer],
                   w_out[layer], ln_mix_g[layer], ln_mix_b[layer], w_router[layer], b_router[layer],
                   w_gate_up[layer], b_gate_up[layer], w_down[layer], b_down[layer],
                   ln_ffn_g[layer], ln_ffn_b[layer])
    return h.reshape(batch, seq, d)
```
